```python
import math
import jax, jax.numpy as jnp
from jax import lax
import numpy as np

D_MODEL = 1024
BATCH = 8
SEQ = 4096
DEPTH = 4

CHUNK = 64
Q_BLOCK = 128
ROPE_THETA = 10000.0
NORM_EPS = 1e-6

N_EVEN = (DEPTH + 1) // 2
N_ODD = DEPTH // 2

A_HEADS = D_MODEL // 128
A_HEAD_DIM = 64
A_LEFT_CHUNKS = 8
A_BAND = (A_LEFT_CHUNKS + 1) * CHUNK
A_MAX_REL = 256
A_WIDTH = A_HEADS * A_HEAD_DIM

B_HEADS = D_MODEL // 128
B_NOPE_DIM = 64
B_ROPE_DIM = 32
B_V_DIM = 64
B_Q_LORA = D_MODEL // 4
B_KV_LORA = D_MODEL // 8
B_WIDTH = B_HEADS * B_V_DIM

EVEN_IN = 3 * A_WIDTH + B_Q_LORA + B_KV_LORA + B_ROPE_DIM
EVEN_MIX = A_WIDTH + B_WIDTH

C_HEAD_DIM = 64
C_HEADS = D_MODEL // (2 * C_HEAD_DIM)
C_WIDTH = C_HEADS * 2 * C_HEAD_DIM

FF_DENSE = 2816
N_EXPERTS = 8
TOP_K = 2
FF_EXPERT = 3584

kernel_name = "hybrid_chunk_causal_relpos_mla_diffattn_moe"


def rms_norm(x, g, eps=NORM_EPS):
    x32 = x.astype(jnp.float32)
    y = x32 * lax.rsqrt(jnp.mean(x32 * x32, axis=-1, keepdims=True) + eps)
    return (y * g.astype(jnp.float32)).astype(x.dtype)


def rope_tables(positions, dim):
    inv_freq = 1.0 / jnp.power(ROPE_THETA, jnp.arange(0, dim, 2, dtype=jnp.float32) / dim)
    ang = positions.astype(jnp.float32)[..., None] * inv_freq
    return jnp.cos(ang), jnp.sin(ang)


def apply_rope(x, cos, sin):
    x1, x2 = jnp.split(x, 2, axis=-1)
    c = cos[:, None].astype(x.dtype)
    s = sin[:, None].astype(x.dtype)
    return jnp.concatenate([x1 * c - x2 * s, x1 * s + x2 * c], axis=-1)


def to_heads(t, n_heads):
    b, s, _ = t.shape
    return t.reshape(b, s, n_heads, -1).transpose(0, 2, 1, 3)


def from_heads(t):
    b, h, s, d = t.shape
    return t.transpose(0, 2, 1, 3).reshape(b, s, h * d)


def chunk_causal_mask(start, seq):
    q_chunk = (start + jnp.arange(Q_BLOCK)) // CHUNK
    k_chunk = jnp.arange(seq) // CHUNK
    return k_chunk[None, :] <= q_chunk[:, None]


def masked_softmax(s, mask):
    s = jnp.where(mask, s.astype(jnp.float32), -jnp.inf)
    return jax.nn.softmax(s, axis=-1)


def sweep_query_blocks(block_fn, qs):
    b, h, s, _ = qs[0].shape
    nb = s // Q_BLOCK
    blocked = tuple(q.reshape(b, h, nb, Q_BLOCK, q.shape[-1]).transpose(2, 0, 1, 3, 4) for q in qs)
    starts = jnp.arange(nb, dtype=jnp.int32) * Q_BLOCK
    out = lax.map(lambda args: block_fn(args[0], *args[1:]), (starts,) + blocked)
    return out.transpose(1, 2, 0, 3, 4).reshape(b, h, s, out.shape[-1])


def chunked_relpos_attention(q, k, v, rel_bias):
    b, h, s, dh = q.shape
    nc = s // CHUNK
    qc = q.reshape(b, h, nc, CHUNK, dh)
    pad = ((0, 0), (0, 0), (A_LEFT_CHUNKS * CHUNK, 0), (0, 0))
    kp = jnp.pad(k, pad).reshape(b, h, nc + A_LEFT_CHUNKS, CHUNK, dh)
    vp = jnp.pad(v, pad).reshape(b, h, nc + A_LEFT_CHUNKS, CHUNK, dh)
    kb = jnp.concatenate([kp[:, :, w:w + nc] for w in range(A_LEFT_CHUNKS + 1)], axis=3)
    vb = jnp.concatenate([vp[:, :, w:w + nc] for w in range(A_LEFT_CHUNKS + 1)], axis=3)
    rel = A_LEFT_CHUNKS * CHUNK + jnp.arange(CHUNK)[:, None] - jnp.arange(A_BAND)[None, :]
    rel = jnp.clip(rel, -A_MAX_REL, A_MAX_REL) + A_MAX_REL
    bias = rel_bias[:, rel].astype(jnp.float32)
    key_chunk = jnp.arange(nc)[:, None] - A_LEFT_CHUNKS + (jnp.arange(A_BAND) // CHUNK)[None, :]
    valid = key_chunk >= 0
    sc = jnp.einsum('bhnqd,bhnkd->bhnqk', qc, kb).astype(jnp.float32) * (dh ** -0.5) + bias[:, None]
    p = masked_softmax(sc, valid[:, None, :])
    o = jnp.einsum('bhnqk,bhnkd->bhnqd', p.astype(v.dtype), vb)
    return o.reshape(b, h, s, dh)


def mla_attention(c_q, c_kv, k_rope_raw, cos, sin, q_norm_g, w_uq, kv_norm_g, w_ukv):
    b, s, _ = c_q.shape
    q = to_heads(rms_norm(c_q, q_norm_g) @ w_uq, B_HEADS)
    q_nope, q_rope = q[..., :B_NOPE_DIM], q[..., B_NOPE_DIM:]
    kv = to_heads(rms_norm(c_kv, kv_norm_g) @ w_ukv, B_HEADS)
    k_nope, v = kv[..., :B_NOPE_DIM], kv[..., B_NOPE_DIM:]
    q_rope = apply_rope(q_rope, cos, sin)
    k_rope = apply_rope(k_rope_raw[:, None], cos, sin)
    q_full = jnp.concatenate([q_nope, q_rope], axis=-1)
    k_full = jnp.concatenate([k_nope, jnp.broadcast_to(k_rope, (b, B_HEADS, s, B_ROPE_DIM))], axis=-1)
    scale = (B_NOPE_DIM + B_ROPE_DIM) ** -0.5

    def block(start, qb):
        sc = jnp.einsum('bhqd,bhkd->bhqk', qb, k_full) * scale
        p = masked_softmax(sc, chunk_causal_mask(start, s))
        return jnp.einsum('bhqk,bhkd->bhqd', p.astype(v.dtype), v)

    return sweep_query_blocks(block, (q_full,))


def diff_attention(q, k, v, cos, sin, lam_q1, lam_k1, lam_q2, lam_k2, subln_g, lambda_init):
    s = q.shape[2]
    q1, q2 = jnp.split(q, 2, axis=-1)
    k1, k2 = jnp.split(k, 2, axis=-1)
    q1, q2 = apply_rope(q1, cos, sin), apply_rope(q2, cos, sin)
    k1, k2 = apply_rope(k1, cos, sin), apply_rope(k2, cos, sin)
    f32 = jnp.float32
    lam = (jnp.exp(jnp.sum(lam_q1.astype(f32) * lam_k1.astype(f32)))
           - jnp.exp(jnp.sum(lam_q2.astype(f32) * lam_k2.astype(f32))) + lambda_init)
    scale = C_HEAD_DIM ** -0.5

    def block(start, q1b, q2b):
        mask = chunk_causal_mask(start, s)
        p1 = masked_softmax(jnp.einsum('bhqd,bhkd->bhqk', q1b, k1) * scale, mask)
        p2 = masked_softmax(jnp.einsum('bhqd,bhkd->bhqk', q2b, k2) * scale, mask)
        a = p1 - lam * p2
        return jnp.einsum('bhqk,bhkd->bhqd', a.astype(v.dtype), v)

    o = sweep_query_blocks(block, (q1, q2))
    return rms_norm(o, subln_g) * (1.0 - lambda_init)


def swiglu(x, w_gu, w_down):
    gate, up = jnp.split(x @ w_gu, 2, axis=-1)
    return (jax.nn.silu(gate) * up) @ w_down


def moe_swiglu(x, w_router, w_gu, w_down):
    b, s, d = x.shape
    t = x.reshape(b * s, d)
    logits = (t @ w_router).astype(jnp.float32)
    top_v, top_i = lax.top_k(logits, TOP_K)
    gates = jax.nn.softmax(top_v, axis=-1)
    combine = jnp.sum(jax.nn.one_hot(top_i, N_EXPERTS, dtype=jnp.float32) * gates[..., None], axis=1)
    y = jnp.zeros_like(t)
    for e in range(N_EXPERTS):
        y = y + combine[:, e:e + 1].astype(t.dtype) * swiglu(t, w_gu[e], w_down[e])
    return y.reshape(b, s, d)


def diff_lambda_init(layer):
    return 0.8 - 0.6 * math.exp(-0.3 * layer)


def setup_inputs(seed: int = 0) -> dict:
    key = jax.random.key(seed)
    ks = iter(jax.random.split(key, 32))
    f32 = jnp.float32

    def w(shape, fan_in):
        return jax.random.normal(next(ks), shape, f32) * (fan_in ** -0.5)

    def gain(shape):
        return 1.0 + 0.02 * jax.random.normal(next(ks), shape, f32)

    x = jax.random.normal(next(ks), (BATCH, SEQ, D_MODEL), f32)
    offset = jax.random.randint(next(ks), (BATCH, 1), 0, 16, dtype=jnp.int32) * CHUNK
    positions = (offset + jnp.arange(SEQ, dtype=jnp.int32)[None, :]).astype(jnp.int32)
    return {
        "x": x,
        "positions": positions,
        "mix_norm_g": gain((DEPTH, D_MODEL)),
        "ffn_norm_g": gain((DEPTH, D_MODEL)),
        "w_in_even": w((N_EVEN, D_MODEL, EVEN_IN), D_MODEL),
        "rel_bias_a": 0.5 * jax.random.normal(next(ks), (N_EVEN, A_HEADS, 2 * A_MAX_REL + 1), f32),
        "q_norm_b": gain((N_EVEN, B_Q_LORA)),
        "w_uq_b": w((N_EVEN, B_Q_LORA, B_HEADS * (B_NOPE_DIM + B_ROPE_DIM)), B_Q_LORA),
        "kv_norm_b": gain((N_EVEN, B_KV_LORA)),
        "w_ukv_b": w((N_EVEN, B_KV_LORA, B_HEADS * (B_NOPE_DIM + B_V_DIM)), B_KV_LORA),
        "w_out_even": w((N_EVEN, EVEN_MIX, D_MODEL), EVEN_MIX),
        "w_in_odd": w((N_ODD, D_MODEL, 3 * C_WIDTH), D_MODEL),
        "lambda_q1": 0.1 * jax.random.normal(next(ks), (N_ODD, C_HEAD_DIM), f32),
        "lambda_k1": 0.1 * jax.random.normal(next(ks), (N_ODD, C_HEAD_DIM), f32),
        "lambda_q2": 0.1 * jax.random.normal(next(ks), (N_ODD, C_HEAD_DIM), f32),
        "lambda_k2": 0.1 * jax.random.normal(next(ks), (N_ODD, C_HEAD_DIM), f32),
        "subln_g": gain((N_ODD, 2 * C_HEAD_DIM)),
        "w_out_odd": w((N_ODD, C_WIDTH, D_MODEL), C_WIDTH),
        "w_gu_dense": w((N_EVEN, D_MODEL, 2 * FF_DENSE), D_MODEL),
        "w_down_dense": w((N_EVEN, FF_DENSE, D_MODEL), FF_DENSE),
        "w_router": w((N_ODD, D_MODEL, N_EXPERTS), D_MODEL),
        "w_gu_moe": w((N_ODD, N_EXPERTS, D_MODEL, 2 * FF_EXPERT), D_MODEL),
        "w_down_moe": w((N_ODD, N_EXPERTS, FF_EXPERT, D_MODEL), FF_EXPERT),
        "final_norm_g": gain((D_MODEL,)),
    }


def reference(x, positions, mix_norm_g, ffn_norm_g, w_in_even, rel_bias_a, q_norm_b, w_uq_b,
              kv_norm_b, w_ukv_b, w_out_even, w_in_odd, lambda_q1, lambda_k1, lambda_q2,
              lambda_k2, subln_g, w_out_odd, w_gu_dense, w_down_dense, w_router, w_gu_moe,
              w_down_moe, final_norm_g):
    cos_b, sin_b = rope_tables(positions, B_ROPE_DIM)
    cos_c, sin_c = rope_tables(positions, C_HEAD_DIM)
    h = x
    for layer in range(DEPTH):
        i = layer // 2
        hn = rms_norm(h, mix_norm_g[layer])
        if layer % 2 == 0:
            proj = hn @ w_in_even[i]
            o1 = A_WIDTH
            o2 = o1 + A_WIDTH
            o3 = o2 + A_WIDTH
            o4 = o3 + B_Q_LORA
            o5 = o4 + B_KV_LORA
            qa = to_heads(proj[..., :o1], A_HEADS)
            ka = to_heads(proj[..., o1:o2], A_HEADS)
            va = to_heads(proj[..., o2:o3], A_HEADS)
            out_a = chunked_relpos_attention(qa, ka, va, rel_bias_a[i])
            out_b = mla_attention(proj[..., o3:o4], proj[..., o4:o5], proj[..., o5:], cos_b, sin_b,
                                  q_norm_b[i], w_uq_b[i], kv_norm_b[i], w_ukv_b[i])
            mix = jnp.concatenate([from_heads(out_a), from_heads(out_b)], axis=-1)
            h = h + mix @ w_out_even[i]
            h = h + swiglu(rms_norm(h, ffn_norm_g[layer]), w_gu_dense[i], w_down_dense[i])
        else:
            qkv = hn @ w_in_odd[i]
            qc = to_heads(qkv[..., :C_WIDTH], C_HEADS)
            kc = to_heads(qkv[..., C_WIDTH:2 * C_WIDTH], C_HEADS)
            vc = to_heads(qkv[..., 2 * C_WIDTH:], C_HEADS)
            out_c = diff_attention(qc, kc, vc, cos_c, sin_c, lambda_q1[i], lambda_k1[i],
                                   lambda_q2[i], lambda_k2[i], subln_g[i], diff_lambda_init(layer))
            h = h + from_heads(out_c) @ w_out_odd[i]
            h = h + moe_swiglu(rms_norm(h, ffn_norm_g[layer]), w_router[i], w_gu_moe[i], w_down_moe[i])
    return rms_norm(h, final_norm_g)
```

```python
import functools
import math

import jax
import jax.numpy as jnp
from jax import lax
from jax.experimental import pallas as pl
from jax.experimental.pallas import tpu as pltpu

F32 = jnp.float32
BF16 = jnp.bfloat16
I32 = jnp.int32

NORM_EPS = 1e-6
ROPE_THETA = 10000.0
NEG = -1e30
LANES = 128
CHUNK = 64

A_LEFT_CHUNKS = 8
A_MAX_REL = 256
A_HEAD_DIM = 64
B_NOPE_DIM = 64
B_ROPE_DIM = 32
B_V_DIM = 64
C_HEAD_DIM = 64
N_EXPERTS = 8

VMEM_LIMIT = 56 * 1024 * 1024


def _params(*sem):
    return pltpu.CompilerParams(dimension_semantics=sem, vmem_limit_bytes=VMEM_LIMIT)


def _rms(x, g):
    ms = jnp.mean(x * x, axis=-1, keepdims=True)
    return x * lax.rsqrt(ms + NORM_EPS) * g


def _dot(a, b):
    return jnp.dot(a, b, preferred_element_type=F32)


def _dot_nt(a, b):
    return lax.dot_general(a, b, (((1,), (1,)), ((), ())), preferred_element_type=F32)


def _rope_lanes(x, cos, sin, half):
    lane = lax.broadcasted_iota(I32, x.shape, 1)
    fwd = pltpu.roll(x, LANES - half, 1)
    bwd = pltpu.roll(x, half, 1)
    swapped = jnp.where((lane & half) == 0, fwd, bwd)
    return x * cos + swapped * sin


def _rms_matmul_kernel(x_ref, g_ref, w_ref, o_ref, xn_ref):
    @pl.when(pl.program_id(1) == 0)
    def _():
        xn_ref[...] = _rms(x_ref[...], g_ref[...]).astype(BF16)

    o_ref[...] = _dot(xn_ref[...], w_ref[...]).astype(o_ref.dtype)


def _rms_matmul_rope_kernel(x_ref, g_ref, w_ref, cos_ref, sin_ref, o_ref, xn_ref, *,
                            n_rope_tiles, half):
    j = pl.program_id(1)

    @pl.when(j == 0)
    def _():
        xn_ref[...] = _rms(x_ref[...], g_ref[...]).astype(BF16)

    acc = _dot(xn_ref[...], w_ref[...])

    @pl.when(j < n_rope_tiles)
    def _():
        cos = cos_ref[...]
        sin = sin_ref[...]
        for c in range(acc.shape[1] // LANES):
            sl = slice(c * LANES, (c + 1) * LANES)
            o_ref[:, sl] = _rope_lanes(acc[:, sl], cos, sin, half).astype(o_ref.dtype)

    @pl.when(j >= n_rope_tiles)
    def _():
        o_ref[...] = acc.astype(o_ref.dtype)


def rms_matmul(x, g, w, *, tm, tn, rope=None):
    t, d = x.shape
    n = w.shape[1]
    grid = (t // tm, n // tn)
    x_spec = pl.BlockSpec((tm, d), lambda i, j: (i, 0))
    g_spec = pl.BlockSpec((1, d), lambda i, j: (0, 0))
    w_spec = pl.BlockSpec((d, tn), lambda i, j: (0, j))
    o_spec = pl.BlockSpec((tm, tn), lambda i, j: (i, j))
    scratch = [pltpu.VMEM((tm, d), BF16)]
    out_shape = jax.ShapeDtypeStruct((t, n), BF16)
    if rope is None:
        return pl.pallas_call(
            _rms_matmul_kernel, grid=grid, in_specs=[x_spec, g_spec, w_spec], out_specs=o_spec,
            out_shape=out_shape, scratch_shapes=scratch,
            compiler_params=_params("parallel", "arbitrary"), name="rms_matmul",
        )(x, g, w)
    tables, per_part, half = rope
    cos_spec = pl.BlockSpec((None, None, tm, LANES),
                            lambda i, j: (jnp.minimum(j // per_part, 1), 0, i, 0))
    sin_spec = pl.BlockSpec((None, None, tm, LANES),
                            lambda i, j: (jnp.minimum(j // per_part, 1), 1, i, 0))
    kern = functools.partial(_rms_matmul_rope_kernel, n_rope_tiles=2 * per_part, half=half)
    return pl.pallas_call(
        kern, grid=grid, in_specs=[x_spec, g_spec, w_spec, cos_spec, sin_spec], out_specs=o_spec,
        out_shape=out_shape, scratch_shapes=scratch,
        compiler_params=_params("parallel", "arbitrary"), name="rms_matmul_rope",
    )(x, g, w, tables, tables)


def _mm_res_kernel(*refs, n_in):
    res_ref = refs[2 * n_in]
    o_ref = refs[2 * n_in + 1]
    acc = res_ref[...]
    for k in range(n_in):
        acc = acc + _dot(refs[k][...], refs[n_in + k][...])
    o_ref[...] = acc


def matmul_residual(a_list, w_list, res, *, tm):
    t, n = res.shape
    n_in = len(a_list)
    in_specs = [pl.BlockSpec((tm, a.shape[1]), lambda i: (i, 0)) for a in a_list]
    in_specs += [pl.BlockSpec(w.shape, lambda i: (0, 0)) for w in w_list]
    in_specs += [pl.BlockSpec((tm, n), lambda i: (i, 0))]
    return pl.pallas_call(
        functools.partial(_mm_res_kernel, n_in=n_in), grid=(t // tm,), in_specs=in_specs,
        out_specs=pl.BlockSpec((tm, n), lambda i: (i, 0)),
        out_shape=jax.ShapeDtypeStruct((t, n), F32),
        compiler_params=_params("parallel"), name="matmul_residual",
    )(*a_list, *w_list, res)


def _ffn_kernel(te_ref, nv_ref, x_ref, g_ref, wg_ref, wu_ref, wd_ref, o_ref, xn_ref, acc_ref, *,
                add_res):
    del te_ref
    i = pl.program_id(0)
    f = pl.program_id(1)
    nf = pl.num_programs(1)
    valid = i < nv_ref[0]

    @pl.when(valid & (f == 0))
    def _():
        xn_ref[...] = _rms(x_ref[...], g_ref[...]).astype(BF16)
        acc_ref[...] = jnp.zeros_like(acc_ref)

    @pl.when(valid)
    def _():
        xn = xn_ref[...]
        gate = _dot(xn, wg_ref[...])
        up = _dot(xn, wu_ref[...])
        act = (gate * jax.nn.sigmoid(gate) * up).astype(BF16)
        acc_ref[...] += _dot(act, wd_ref[...])

    @pl.when(valid & (f == nf - 1))
    def _():
        if add_res:
            o_ref[...] = x_ref[...] + acc_ref[...]
        else:
            o_ref[...] = acc_ref[...]

    @pl.when(jnp.logical_not(valid) & (f == nf - 1))
    def _():
        o_ref[...] = jnp.zeros_like(o_ref)


def grouped_swiglu(x, g, w_gu, w_down, tile_expert, n_valid, *, tm, tf, add_res):
    p, d = x.shape
    ff = w_down.shape[1]
    nf = ff // tf
    grid = (p // tm, nf)

    def f_eff(i, f, nv):
        return jnp.where(i < nv[0], f, nf - 1)

    in_specs = [
        pl.BlockSpec((tm, d), lambda i, f, te, nv: (i, 0)),
        pl.BlockSpec((1, d), lambda i, f, te, nv: (0, 0)),
        pl.BlockSpec((None, d, tf), lambda i, f, te, nv: (te[i], 0, f_eff(i, f, nv))),
        pl.BlockSpec((None, d, tf), lambda i, f, te, nv: (te[i], 0, nf + f_eff(i, f, nv))),
        pl.BlockSpec((None, tf, d), lambda i, f, te, nv: (te[i], f_eff(i, f, nv), 0)),
    ]
    grid_spec = pltpu.PrefetchScalarGridSpec(
        num_scalar_prefetch=2, grid=grid, in_specs=in_specs,
        out_specs=pl.BlockSpec((tm, d), lambda i, f, te, nv: (i, 0)),
        scratch_shapes=[pltpu.VMEM((tm, d), BF16), pltpu.VMEM((tm, d), F32)],
    )
    return pl.pallas_call(
        functools.partial(_ffn_kernel, add_res=add_res), grid_spec=grid_spec,
        out_shape=jax.ShapeDtypeStruct((p, d), F32),
        compiler_params=_params("parallel", "arbitrary"), name="grouped_swiglu",
    )(tile_expert, n_valid, x, g, w_gu, w_gu, w_down)


def _flash_kernel(lam_ref, q_ref, k_ref, v_ref, g_ref, o_ref, m_ref, l_ref, acc_ref, *,
                  mode, tq, out_scale):
    i = pl.program_id(2)
    lane = lax.broadcasted_iota(I32, (tq, LANES), 1)
    low = lane < (LANES // 2)

    if mode == "mla":
        qs = (q_ref[:, :LANES], q_ref[:, LANES:])
    else:
        q = q_ref[...].astype(F32)
        qs = (jnp.where(low, q, 0.0).astype(BF16), jnp.where(low, 0.0, q).astype(BF16))

    m_ref[...] = jnp.full_like(m_ref, NEG)
    l_ref[...] = jnp.zeros_like(l_ref)
    acc_ref[...] = jnp.zeros_like(acc_ref)

    def k_tiles(start):
        if mode == "mla":
            return (k_ref[pl.ds(start, tq), :LANES], k_ref[pl.ds(start, tq), LANES:])
        kt = k_ref[pl.ds(start, tq), :]
        return (kt, kt)

    def update(idx, s, v_tile):
        m_prev = m_ref[idx]
        m_new = jnp.maximum(m_prev, jnp.max(s, axis=-1, keepdims=True))
        alpha = jnp.exp(m_prev - m_new)
        p = jnp.exp(s - m_new)
        l_ref[idx] = alpha * l_ref[idx] + jnp.sum(p, axis=-1, keepdims=True)
        acc_ref[idx] = alpha * acc_ref[idx] + _dot(p.astype(BF16), v_tile)
        m_ref[idx] = m_new

    def full_tile(t, carry):
        start = pl.multiple_of(t * tq, tq)
        kts = k_tiles(start)
        v_tile = v_ref[pl.ds(start, tq), :]
        for idx in range(2):
            update(idx, _dot_nt(qs[idx], kts[idx]), v_tile)
        return carry

    lax.fori_loop(0, i, full_tile, 0)

    start = pl.multiple_of(i * tq, tq)
    kts = k_tiles(start)
    v_tile = v_ref[pl.ds(start, tq), :]
    row = lax.broadcasted_iota(I32, (tq, tq), 0)
    col = lax.broadcasted_iota(I32, (tq, tq), 1)
    shift = CHUNK.bit_length() - 1
    visible = (col >> shift) <= (row >> shift)
    for idx in range(2):
        s = jnp.where(visible, _dot_nt(qs[idx], kts[idx]), NEG)
        update(idx, s, v_tile)

    o0 = acc_ref[0] * (1.0 / l_ref[0])
    o1 = acc_ref[1] * (1.0 / l_ref[1])
    if mode == "mla":
        o_ref[...] = jnp.where(low, o0, o1).astype(o_ref.dtype)
    else:
        o = o0 - lam_ref[0] * o1
        o_ref[...] = (_rms(o, g_ref[...]) * out_scale).astype(o_ref.dtype)


def flash_attention(q_arr, k_arr, v_arr, *, mode, q_blk0, k_blk0, v_blk0, n_groups, tq,
                    lam=None, subln_g=None, out_scale=1.0):
    b, s, _ = q_arr.shape
    wqk = 2 * LANES if mode == "mla" else LANES
    if lam is None:
        lam = jnp.zeros((1,), F32)
    if subln_g is None:
        subln_g = jnp.ones((1, LANES), F32)
    grid = (b, n_groups, s // tq)
    in_specs = [
        pl.BlockSpec((None, tq, wqk), lambda bb, g, i, lam_r: (bb, i, q_blk0 + g)),
        pl.BlockSpec((None, s, wqk), lambda bb, g, i, lam_r: (bb, 0, k_blk0 + g)),
        pl.BlockSpec((None, s, LANES), lambda bb, g, i, lam_r: (bb, 0, v_blk0 + g)),
        pl.BlockSpec((1, LANES), lambda bb, g, i, lam_r: (0, 0)),
    ]
    grid_spec = pltpu.PrefetchScalarGridSpec(
        num_scalar_prefetch=1, grid=grid, in_specs=in_specs,
        out_specs=pl.BlockSpec((None, tq, LANES), lambda bb, g, i, lam_r: (bb, i, g)),
        scratch_shapes=[pltpu.VMEM((2, tq, 1), F32), pltpu.VMEM((2, tq, 1), F32),
                        pltpu.VMEM((2, tq, LANES), F32)],
    )
    kern = functools.partial(_flash_kernel, mode=mode, tq=tq, out_scale=out_scale)
    return pl.pallas_call(
        kern, grid_spec=grid_spec,
        out_shape=jax.ShapeDtypeStruct((b, s, n_groups * LANES), BF16),
        compiler_params=_params("parallel", "parallel", "arbitrary"), name="flash_" + mode,
    )(lam, q_arr, k_arr, v_arr, subln_g)


BAND_TQ = 256
BAND_TILES = 3


def _band_kernel(q_ref, k0_ref, k1_ref, k2_ref, v0_ref, v1_ref, v2_ref, bias_ref, o_ref, *, scale):
    i = pl.program_id(2)
    lane = lax.broadcasted_iota(I32, (BAND_TQ, LANES), 1)
    low = lane < (LANES // 2)
    q = q_ref[...].astype(F32) * scale
    qs = (jnp.where(low, q, 0.0).astype(BF16), jnp.where(low, 0.0, q).astype(BF16))
    k_refs = (k0_ref, k1_ref, k2_ref)
    v_refs = (v0_ref, v1_ref, v2_ref)
    outs = []
    for h in range(2):
        qh = qs[h]
        scores = []
        for j in range(BAND_TILES):
            pen = jnp.where(i + j >= BAND_TILES - 1, 0.0, NEG).astype(F32)
            scores.append(_dot_nt(qh, k_refs[j][...]) + bias_ref[h, j] + pen)
        m = jnp.maximum(jnp.maximum(jnp.max(scores[0], axis=-1, keepdims=True),
                                    jnp.max(scores[1], axis=-1, keepdims=True)),
                        jnp.max(scores[2], axis=-1, keepdims=True))
        l = jnp.zeros_like(m)
        o = jnp.zeros((BAND_TQ, LANES), F32)
        for j in range(BAND_TILES):
            p = jnp.exp(scores[j] - m)
            l = l + jnp.sum(p, axis=-1, keepdims=True)
            o = o + _dot(p.astype(BF16), v_refs[j][...])
        outs.append(o * (1.0 / l))
    o_ref[...] = jnp.where(low, outs[0], outs[1]).astype(o_ref.dtype)


def band_bias_tiles(rel_bias):
    h = rel_bias.shape[0]
    r = jnp.arange(BAND_TQ)[:, None]
    c = jnp.arange(BAND_TQ)[None, :]
    tiles = []
    for j in range(BAND_TILES):
        rel = (BAND_TILES - 1 - j) * BAND_TQ + r - c
        idx = jnp.clip(rel, -A_MAX_REL, A_MAX_REL) + A_MAX_REL
        dist = (r // CHUNK) - (c // CHUNK) + (BAND_TILES - 1 - j) * (BAND_TQ // CHUNK)
        ok = (dist >= 0) & (dist <= A_LEFT_CHUNKS)
        tiles.append(jnp.where(ok[None], rel_bias[:, idx].astype(F32), NEG))
    t = jnp.stack(tiles, axis=1)
    return t.reshape(h // 2, 2, BAND_TILES, BAND_TQ, BAND_TQ)


def band_attention(proj, bias_tiles, *, q_blk0, k_blk0, v_blk0):
    b, s, _ = proj.shape
    n_pairs = bias_tiles.shape[0]
    grid = (n_pairs, b, s // BAND_TQ)

    def kv_spec(blk0, j):
        return pl.BlockSpec(
            (None, BAND_TQ, LANES),
            lambda p, bb, i: (bb, jnp.maximum(i + j - (BAND_TILES - 1), 0), blk0 + p))

    in_specs = [pl.BlockSpec((None, BAND_TQ, LANES), lambda p, bb, i: (bb, i, q_blk0 + p))]
    in_specs += [kv_spec(k_blk0, j) for j in range(BAND_TILES)]
    in_specs += [kv_spec(v_blk0, j) for j in range(BAND_TILES)]
    in_specs += [pl.BlockSpec((None, 2, BAND_TILES, BAND_TQ, BAND_TQ),
                              lambda p, bb, i: (p, 0, 0, 0, 0))]
    return pl.pallas_call(
        functools.partial(_band_kernel, scale=A_HEAD_DIM ** -0.5), grid=grid, in_specs=in_specs,
        out_specs=pl.BlockSpec((None, BAND_TQ, LANES), lambda p, bb, i: (bb, i, p)),
        out_shape=jax.ShapeDtypeStruct((b, s, n_pairs * LANES), BF16),
        compiler_params=_params("parallel", "parallel", "parallel"), name="band_attention",
    )(proj, proj, proj, proj, proj, proj, proj, bias_tiles)


def _mla_prep_kernel(cq_ref, ckv_ref, kr_ref, gq_ref, gkv_ref, wq_ref, wk_ref, wv_ref,
                     cq_cos_ref, cq_sin_ref, ck_cos_ref, ck_sin_ref, q_ref, k_ref, v_ref):
    half = B_ROPE_DIM // 2
    nq = _rms(cq_ref[...].astype(F32), gq_ref[...]).astype(BF16)
    nkv = _rms(ckv_ref[...].astype(F32), gkv_ref[...]).astype(BF16)
    q = _dot(nq, wq_ref[...])
    k = _dot(nkv, wk_ref[...])
    v_ref[...] = _dot(nkv, wv_ref[...]).astype(v_ref.dtype)
    kr = _rope_lanes(kr_ref[...].astype(F32), ck_cos_ref[...], ck_sin_ref[...], half)
    lane = lax.broadcasted_iota(I32, kr.shape, 1)
    kr = jnp.where(lane < B_ROPE_DIM, kr, 0.0)
    kr = pltpu.roll(kr, B_NOPE_DIM, 1)
    cos = cq_cos_ref[...]
    sin = cq_sin_ref[...]
    for c in range(q.shape[1] // LANES):
        sl = slice(c * LANES, (c + 1) * LANES)
        q_ref[:, sl] = _rope_lanes(q[:, sl], cos, sin, half).astype(q_ref.dtype)
        k_ref[:, sl] = (k[:, sl] + kr).astype(k_ref.dtype)


def mla_prep(proj, gq, gkv, wq, wk, wv, tables, *, tm, cq_blk, ckv_blk, kr_blk):
    t = proj.shape[0]
    nq = wq.shape[0]
    nkv = wk.shape[0]
    whole = lambda a: pl.BlockSpec(a.shape, lambda i: (0,) * a.ndim)
    tab = lambda k: pl.BlockSpec((None, tm, LANES), lambda i: (k, i, 0))
    in_specs = [
        pl.BlockSpec((tm, nq), lambda i: (i, cq_blk)),
        pl.BlockSpec((tm, nkv), lambda i: (i, ckv_blk)),
        pl.BlockSpec((tm, LANES), lambda i: (i, kr_blk)),
        whole(gq), whole(gkv), whole(wq), whole(wk), whole(wv),
        tab(0), tab(1), tab(2), tab(3),
    ]
    out_specs = [pl.BlockSpec((tm, wq.shape[1]), lambda i: (i, 0)),
                 pl.BlockSpec((tm, wk.shape[1]), lambda i: (i, 0)),
                 pl.BlockSpec((tm, wv.shape[1]), lambda i: (i, 0))]
    out_shape = [jax.ShapeDtypeStruct((t, wq.shape[1]), BF16),
                 jax.ShapeDtypeStruct((t, wk.shape[1]), BF16),
                 jax.ShapeDtypeStruct((t, wv.shape[1]), BF16)]
    return pl.pallas_call(
        _mla_prep_kernel, grid=(t // tm,), in_specs=in_specs, out_specs=out_specs,
        out_shape=out_shape, compiler_params=_params("parallel"), name="mla_prep",
    )(proj, proj, proj, gq, gkv, wq, wk, wv, tables, tables, tables, tables)


def _router_kernel(h_ref, g_ref, wr_ref, gate_ref, idx_ref, cnt_ref, carry_ref):
    tb = h_ref.shape[0]

    @pl.when(pl.program_id(0) == 0)
    def _():
        carry_ref[...] = jnp.zeros_like(carry_ref)

    n = _rms(h_ref[...], g_ref[...])
    logits = jnp.dot(n, wr_ref[...], preferred_element_type=F32, precision=lax.Precision.HIGHEST)
    lane = lax.broadcasted_iota(I32, (tb, LANES), 1).astype(F32)
    neg_inf = jnp.float32(-jnp.inf)
    lg = jnp.where(lane < N_EXPERTS, logits, neg_inf)
    m1 = jnp.max(lg, axis=-1, keepdims=True)
    e0 = jnp.min(jnp.where(lg == m1, lane, float(LANES)), axis=-1, keepdims=True)
    lg2 = jnp.where(lane == e0, neg_inf, lg)
    m2 = jnp.max(lg2, axis=-1, keepdims=True)
    e1 = jnp.min(jnp.where(lg2 == m2, lane, float(LANES)), axis=-1, keepdims=True)
    t = jnp.exp(m2 - m1)
    g0 = 1.0 / (1.0 + t)
    g1 = t / (1.0 + t)

    sel = ((lane == e0) | (lane == e1)).astype(F32)
    row = lax.broadcasted_iota(I32, (tb, tb), 0)
    col = lax.broadcasted_iota(I32, (tb, tb), 1)
    strict_lower = (col < row).astype(BF16)
    carry = carry_ref[...]
    excl = _dot(strict_lower, sel.astype(BF16)) + carry
    r0 = jnp.sum(jnp.where(lane == e0, excl, 0.0), axis=-1, keepdims=True)
    r1 = jnp.sum(jnp.where(lane == e1, excl, 0.0), axis=-1, keepdims=True)
    carry = carry + jnp.sum(sel, axis=0, keepdims=True)
    carry_ref[...] = carry

    gate_ref[...] = jnp.where(lane == 0, g0, jnp.where(lane == 1, g1, 0.0))
    idx_ref[...] = jnp.where(
        lane == 0, e0, jnp.where(lane == 1, e1, jnp.where(
            lane == 2, r0, jnp.where(lane == 3, r1, 0.0)))).astype(I32)
    cnt_ref[...] = jnp.broadcast_to(carry, cnt_ref.shape).astype(I32)


def router(h, g, wr_pad, *, tb):
    t, d = h.shape
    return pl.pallas_call(
        _router_kernel, grid=(t // tb,),
        in_specs=[pl.BlockSpec((tb, d), lambda i: (i, 0)),
                  pl.BlockSpec((1, d), lambda i: (0, 0)),
                  pl.BlockSpec((d, LANES), lambda i: (0, 0))],
        out_specs=[pl.BlockSpec((tb, LANES), lambda i: (i, 0)),
                   pl.BlockSpec((tb, LANES), lambda i: (i, 0)),
                   pl.BlockSpec((8, LANES), lambda i: (0, 0))],
        out_shape=[jax.ShapeDtypeStruct((t, LANES), F32),
                   jax.ShapeDtypeStruct((t, LANES), I32),
                   jax.ShapeDtypeStruct((8, LANES), I32)],
        scratch_shapes=[pltpu.VMEM((1, LANES), F32)],
        compiler_params=_params("arbitrary"), name="router",
    )(h, g, wr_pad)


def _row_copy(src_hbm, row, dst_ref, r, sem):
    return pltpu.make_async_copy(src_hbm.at[pl.ds(row, 1), :], dst_ref.at[pl.ds(r, 1), :], sem)


def _gather_kernel(idx_ref, src_hbm, o_ref, sem):
    rows = o_ref.shape[0]

    def start(r, carry):
        _row_copy(src_hbm, idx_ref[0, 0, r], o_ref, r, sem).start()
        return carry

    def wait(r, carry):
        _row_copy(src_hbm, 0, o_ref, r, sem).wait()
        return carry

    lax.fori_loop(0, rows, start, 0)
    lax.fori_loop(0, rows, wait, 0)


def gather_rows(src, idx, *, rows):
    p = idx.shape[0]
    d = src.shape[1]
    idx3 = idx.reshape(p // rows, 1, rows)
    return pl.pallas_call(
        _gather_kernel, grid=(p // rows,),
        in_specs=[pl.BlockSpec((1, 1, rows), lambda i: (i, 0, 0), memory_space=pltpu.SMEM),
                  pl.BlockSpec(memory_space=pl.ANY)],
        out_specs=pl.BlockSpec((rows, d), lambda i: (i, 0)),
        out_shape=jax.ShapeDtypeStruct((p, d), src.dtype),
        scratch_shapes=[pltpu.SemaphoreType.DMA(())],
        compiler_params=_params("arbitrary"), name="gather_rows",
    )(idx3, src)


def _combine_kernel(p0_ref, p1_ref, h_ref, gate_ref, ys_hbm, o_ref, buf0, buf1, sem):
    rows = h_ref.shape[0]

    def start(r, carry):
        _row_copy(ys_hbm, p0_ref[0, 0, r], buf0, r, sem).start()
        _row_copy(ys_hbm, p1_ref[0, 0, r], buf1, r, sem).start()
        return carry

    def wait(r, carry):
        _row_copy(ys_hbm, 0, buf0, r, sem).wait()
        _row_copy(ys_hbm, 0, buf1, r, sem).wait()
        return carry

    lax.fori_loop(0, rows, start, 0)
    lax.fori_loop(0, rows, wait, 0)
    gates = gate_ref[...]
    o_ref[...] = h_ref[...] + gates[:, 0:1] * buf0[...] + gates[:, 1:2] * buf1[...]


def combine_rows(h, gates, ys, pos0, pos1, *, rows):
    t, d = h.shape
    smem = lambda: pl.BlockSpec((1, 1, rows), lambda i: (i, 0, 0), memory_space=pltpu.SMEM)
    return pl.pallas_call(
        _combine_kernel, grid=(t // rows,),
        in_specs=[smem(), smem(),
                  pl.BlockSpec((rows, d), lambda i: (i, 0)),
                  pl.BlockSpec((rows, LANES), lambda i: (i, 0)),
                  pl.BlockSpec(memory_space=pl.ANY)],
        out_specs=pl.BlockSpec((rows, d), lambda i: (i, 0)),
        out_shape=jax.ShapeDtypeStruct((t, d), F32),
        scratch_shapes=[pltpu.VMEM((rows, d), F32), pltpu.VMEM((rows, d), F32),
                        pltpu.SemaphoreType.DMA(())],
        compiler_params=_params("arbitrary"), name="combine_rows",
    )(pos0.reshape(t // rows, 1, rows), pos1.reshape(t // rows, 1, rows), h, gates, ys)


def _rmsnorm_kernel(x_ref, g_ref, o_ref):
    o_ref[...] = _rms(x_ref[...], g_ref[...])


def rmsnorm(x, g, *, tm):
    t, d = x.shape
    return pl.pallas_call(
        _rmsnorm_kernel, grid=(t // tm,),
        in_specs=[pl.BlockSpec((tm, d), lambda i: (i, 0)), pl.BlockSpec((1, d), lambda i: (0, 0))],
        out_specs=pl.BlockSpec((tm, d), lambda i: (i, 0)),
        out_shape=jax.ShapeDtypeStruct((t, d), F32),
        compiler_params=_params("parallel"), name="final_rmsnorm",
    )(x, g)


def _rope_cos_sin(positions, dim):
    inv_freq = 1.0 / jnp.power(ROPE_THETA, jnp.arange(0, dim, 2, dtype=F32) / dim)
    ang = positions.reshape(-1).astype(F32)[:, None] * inv_freq
    return jnp.cos(ang), jnp.sin(ang)


def _diff_tables(positions):
    cos, sin = _rope_cos_sin(positions, C_HEAD_DIM)
    cos_l = jnp.tile(cos, (1, 4))
    sin_l = jnp.tile(jnp.concatenate([-sin, sin], axis=1), (1, 2))
    k_tab = jnp.stack([cos_l, sin_l])
    return jnp.stack([k_tab * (C_HEAD_DIM ** -0.5), k_tab])


def _mla_tables(positions):
    cos, sin = _rope_cos_sin(positions, B_ROPE_DIM)
    t = cos.shape[0]
    scale = (B_NOPE_DIM + B_ROPE_DIM) ** -0.5
    ones = lambda n: jnp.ones((t, n), F32)
    zeros = lambda n: jnp.zeros((t, n), F32)
    q_cos = jnp.concatenate([ones(B_NOPE_DIM), cos, cos, ones(32)], axis=1) * scale
    q_sin = jnp.concatenate([zeros(B_NOPE_DIM), -sin, sin, zeros(32)], axis=1) * scale
    k_cos = jnp.concatenate([cos, cos, ones(96)], axis=1)
    k_sin = jnp.concatenate([-sin, sin, zeros(96)], axis=1)
    return jnp.stack([q_cos, q_sin, k_cos, k_sin])


def _routing_plan(idx_out, counts, *, tm, n_tiles):
    t = idx_out.shape[0]
    e0, e1, r0, r1 = idx_out[:, 0], idx_out[:, 1], idx_out[:, 2], idx_out[:, 3]
    padded = ((counts + tm - 1) // tm) * tm
    ends = jnp.cumsum(padded)
    offs = ends - padded
    pos0 = offs[e0] + r0
    pos1 = offs[e1] + r1
    tok = jnp.arange(t, dtype=I32)
    src = jnp.zeros((n_tiles * tm,), I32).at[pos0].set(tok).at[pos1].set(tok)
    n_valid = (ends[-1] // tm).astype(I32)
    tile = jnp.minimum(jnp.arange(n_tiles, dtype=I32), n_valid - 1)
    tile_expert = jnp.minimum(jnp.searchsorted(ends, tile * tm, side="right"), N_EXPERTS - 1)
    return pos0.astype(I32), pos1.astype(I32), src, tile_expert.astype(I32), n_valid.reshape(1)


def _pad_cols(w, n):
    return jnp.pad(w, ((0, 0), (0, n - w.shape[1])))


def kernel(x, positions, mix_norm_g, ffn_norm_g, w_in_even, rel_bias_a, q_norm_b, w_uq_b, kv_norm_b, w_ukv_b, w_out_even, w_in_odd, lambda_q1, lambda_k1, lambda_q2, lambda_k2, subln_g, w_out_odd, w_gu_dense, w_down_dense, w_router, w_gu_moe, w_down_moe, final_norm_g):
    b, s, d = x.shape
    t = b * s
    depth = mix_norm_g.shape[0]
    a_width = 4 * LANES
    tm = 512
    tm_moe = 512
    n_tiles_moe = (2 * t) // tm_moe + N_EXPERTS

    diff_tab = _diff_tables(positions)
    mla_tab = _mla_tables(positions)
    row = lambda v: v.reshape(1, -1).astype(F32)

    h = x.reshape(t, d).astype(F32)
    for layer in range(depth):
        i = layer // 2
        if layer % 2 == 0:
            w_in = _pad_cols(w_in_even[i], 2048).astype(BF16)
            proj = rms_matmul(h, row(mix_norm_g[layer]), w_in, tm=tm, tn=1024)
            proj3 = proj.reshape(b, s, -1)
            out_a = band_attention(proj3, band_bias_tiles(rel_bias_a[i]),
                                   q_blk0=0, k_blk0=4, v_blk0=8)
            n_heads = w_uq_b.shape[2] // (B_NOPE_DIM + B_ROPE_DIM)
            wq = jnp.pad(w_uq_b[i].reshape(-1, n_heads, B_NOPE_DIM + B_ROPE_DIM),
                         ((0, 0), (0, 0), (0, 32))).reshape(-1, n_heads * LANES).astype(BF16)
            wkv = w_ukv_b[i].reshape(-1, n_heads, B_NOPE_DIM + B_V_DIM)
            wk = jnp.pad(wkv[:, :, :B_NOPE_DIM], ((0, 0), (0, 0), (0, LANES - B_NOPE_DIM)))
            wk = wk.reshape(-1, n_heads * LANES).astype(BF16)
            wv = wkv[:, :, B_NOPE_DIM:].reshape(-1, n_heads * B_V_DIM).astype(BF16)
            q_b, k_b, v_b = mla_prep(proj, row(q_norm_b[i]), row(kv_norm_b[i]), wq, wk, wv, mla_tab,
                                     tm=tm, cq_blk=6, ckv_blk=14, kr_blk=15)
            out_b = flash_attention(q_b.reshape(b, s, -1), k_b.reshape(b, s, -1),
                                    v_b.reshape(b, s, -1), mode="mla", q_blk0=0, k_blk0=0,
                                    v_blk0=0, n_groups=n_heads // 2, tq=512)
            w_out = w_out_even[i].astype(BF16)
            h = matmul_residual([out_a.reshape(t, -1), out_b.reshape(t, -1)],
                                [w_out[:a_width], w_out[a_width:]], h, tm=tm)
            n_tiles = t // tm
            h = grouped_swiglu(h, row(ffn_norm_g[layer]), w_gu_dense[i][None].astype(BF16),
                               w_down_dense[i][None].astype(BF16), jnp.zeros((n_tiles,), I32),
                               jnp.full((1,), n_tiles, I32), tm=tm, tf=1408, add_res=True)
        else:
            c_width = w_out_odd.shape[1]
            n_heads = c_width // LANES
            qkv = rms_matmul(h, row(mix_norm_g[layer]), w_in_odd[i].astype(BF16), tm=tm, tn=512,
                             rope=(diff_tab, c_width // 512, C_HEAD_DIM // 2))
            lam_init = 0.8 - 0.6 * math.exp(-0.3 * layer)
            lam = (jnp.exp(jnp.sum(lambda_q1[i].astype(F32) * lambda_k1[i].astype(F32)))
                   - jnp.exp(jnp.sum(lambda_q2[i].astype(F32) * lambda_k2[i].astype(F32)))
                   + lam_init).reshape(1).astype(F32)
            qkv3 = qkv.reshape(b, s, -1)
            out_c = flash_attention(qkv3, qkv3, qkv3, mode="diff", q_blk0=0, k_blk0=n_heads,
                                    v_blk0=2 * n_heads, n_groups=n_heads, tq=512, lam=lam,
                                    subln_g=row(subln_g[i]), out_scale=1.0 - lam_init)
            h = matmul_residual([out_c.reshape(t, -1)], [w_out_odd[i].astype(BF16)], h, tm=tm)

            g_ffn = row(ffn_norm_g[layer])
            wr_pad = _pad_cols(w_router[i].astype(F32), LANES)
            gates, idx_out, cnt = router(h, g_ffn, wr_pad, tb=512)
            pos0, pos1, src, tile_expert, n_valid = _routing_plan(
                idx_out, cnt[0, :N_EXPERTS], tm=tm_moe, n_tiles=n_tiles_moe)
            xs = gather_rows(h, src, rows=256)
            ys = grouped_swiglu(xs, g_ffn, w_gu_moe[i].astype(BF16), w_down_moe[i].astype(BF16),
                                tile_expert, n_valid, tm=tm_moe, tf=512, add_res=False)
            h = combine_rows(h, gates, ys, pos0, pos1, rows=256)

    out = rmsnorm(h, row(final_norm_g), tm=tm)
    return out.reshape(b, s, d)
```

```python
import functools
import math

import jax
import jax.numpy as jnp
import numpy as np
from jax import lax
from jax.experimental import pallas as pl
from jax.experimental.pallas import tpu as pltpu

F32 = jnp.float32
BF16 = jnp.bfloat16
I32 = jnp.int32

NORM_EPS = 1e-6
ROPE_THETA = 10000.0
NEG = -1e30
LOG2E = math.log2(math.e)
LANES = 128
CHUNK = 64

A_LEFT_CHUNKS = 8
A_MAX_REL = 256
A_HEAD_DIM = 64
B_NOPE_DIM = 64
B_ROPE_DIM = 32
B_V_DIM = 64
C_HEAD_DIM = 64
N_EXPERTS = 8

VMEM_LIMIT = 56 * 1024 * 1024


def _params(*sem):
    return pltpu.CompilerParams(dimension_semantics=sem, vmem_limit_bytes=VMEM_LIMIT)


def _rms(x, g):
    ms = jnp.mean(x * x, axis=-1, keepdims=True)
    return x * lax.rsqrt(ms + NORM_EPS) * g


def _dot(a, b):
    return jnp.dot(a, b, preferred_element_type=F32)


def _dot_nt(a, b):
    return lax.dot_general(a, b, (((1,), (1,)), ((), ())), preferred_element_type=F32)


def _rope_lanes(x, cos, sin, half):
    lane = lax.broadcasted_iota(I32, x.shape, 1)
    fwd = pltpu.roll(x, LANES - half, 1)
    bwd = pltpu.roll(x, half, 1)
    swapped = jnp.where((lane & half) == 0, fwd, bwd)
    return x * cos + swapped * sin


def _rms_matmul_kernel(x_ref, g_ref, w_ref, o_ref, xn_ref):
    @pl.when(pl.program_id(1) == 0)
    def _():
        xn_ref[...] = _rms(x_ref[...], g_ref[...]).astype(BF16)

    o_ref[...] = _dot(xn_ref[...], w_ref[...]).astype(o_ref.dtype)


def _rms_matmul_rope_kernel(x_ref, g_ref, w_ref, cos_ref, sin_ref, o_ref, xn_ref, *,
                            n_rope_tiles, half):
    j = pl.program_id(1)

    @pl.when(j == 0)
    def _():
        xn_ref[...] = _rms(x_ref[...], g_ref[...]).astype(BF16)

    acc = _dot(xn_ref[...], w_ref[...])

    @pl.when(j < n_rope_tiles)
    def _():
        cos = cos_ref[...]
        sin = sin_ref[...]
        for c in range(acc.shape[1] // LANES):
            sl = slice(c * LANES, (c + 1) * LANES)
            o_ref[:, sl] = _rope_lanes(acc[:, sl], cos, sin, half).astype(o_ref.dtype)

    @pl.when(j >= n_rope_tiles)
    def _():
        o_ref[...] = acc.astype(o_ref.dtype)


def rms_matmul(x, g, w, *, tm, tn, rope=None):
    t, d = x.shape
    n = w.shape[1]
    grid = (t // tm, n // tn)
    x_spec = pl.BlockSpec((tm, d), lambda i, j: (i, 0))
    g_spec = pl.BlockSpec((1, d), lambda i, j: (0, 0))
    w_spec = pl.BlockSpec((d, tn), lambda i, j: (0, j))
    o_spec = pl.BlockSpec((tm, tn), lambda i, j: (i, j))
    scratch = [pltpu.VMEM((tm, d), BF16)]
    out_shape = jax.ShapeDtypeStruct((t, n), BF16)
    if rope is None:
        return pl.pallas_call(
            _rms_matmul_kernel, grid=grid, in_specs=[x_spec, g_spec, w_spec], out_specs=o_spec,
            out_shape=out_shape, scratch_shapes=scratch,
            compiler_params=_params("parallel", "arbitrary"), name="rms_matmul",
        )(x, g, w)
    tables, per_part, half = rope
    cos_spec = pl.BlockSpec((None, None, tm, LANES),
                            lambda i, j: (jnp.minimum(j // per_part, 1), 0, i, 0))
    sin_spec = pl.BlockSpec((None, None, tm, LANES),
                            lambda i, j: (jnp.minimum(j // per_part, 1), 1, i, 0))
    kern = functools.partial(_rms_matmul_rope_kernel, n_rope_tiles=2 * per_part, half=half)
    return pl.pallas_call(
        kern, grid=grid, in_specs=[x_spec, g_spec, w_spec, cos_spec, sin_spec], out_specs=o_spec,
        out_shape=out_shape, scratch_shapes=scratch,
        compiler_params=_params("parallel", "arbitrary"), name="rms_matmul_rope",
    )(x, g, w, tables, tables)


def _mm_res_kernel(*refs, n_in):
    res_ref = refs[2 * n_in]
    o_ref = refs[2 * n_in + 1]
    acc = res_ref[...]
    for k in range(n_in):
        acc = acc + _dot(refs[k][...], refs[n_in + k][...])
    o_ref[...] = acc


def matmul_residual(a_list, w_list, res, *, tm):
    t, n = res.shape
    n_in = len(a_list)
    in_specs = [pl.BlockSpec((tm, a.shape[1]), lambda i: (i, 0)) for a in a_list]
    in_specs += [pl.BlockSpec(w.shape, lambda i: (0, 0)) for w in w_list]
    in_specs += [pl.BlockSpec((tm, n), lambda i: (i, 0))]
    return pl.pallas_call(
        functools.partial(_mm_res_kernel, n_in=n_in), grid=(t // tm,), in_specs=in_specs,
        out_specs=pl.BlockSpec((tm, n), lambda i: (i, 0)),
        out_shape=jax.ShapeDtypeStruct((t, n), F32),
        compiler_params=_params("parallel"), name="matmul_residual",
    )(*a_list, *w_list, res)


def _ffn_kernel(te_ref, nv_ref, x_ref, g_ref, wg_ref, wu_ref, wd_ref, o_ref, xn_ref, acc_ref, *,
                add_res):
    del te_ref
    i = pl.program_id(0)
    f = pl.program_id(1)
    nf = pl.num_programs(1)
    valid = i < nv_ref[0]

    @pl.when(valid & (f == 0))
    def _():
        xn_ref[...] = _rms(x_ref[...], g_ref[...]).astype(BF16)
        acc_ref[...] = jnp.zeros_like(acc_ref)

    @pl.when(valid)
    def _():
        xn = xn_ref[...]
        gate = _dot(xn, wg_ref[...])
        up = _dot(xn, wu_ref[...])
        act = (gate * jax.nn.sigmoid(gate) * up).astype(BF16)
        acc_ref[...] += _dot(act, wd_ref[...])

    @pl.when(valid & (f == nf - 1))
    def _():
        if add_res:
            o_ref[...] = x_ref[...] + acc_ref[...]
        else:
            o_ref[...] = acc_ref[...]

    @pl.when(jnp.logical_not(valid) & (f == nf - 1))
    def _():
        o_ref[...] = jnp.zeros_like(o_ref)


def grouped_swiglu(x, g, w_gu, w_down, tile_expert, n_valid, *, tm, tf, add_res):
    p, d = x.shape
    ff = w_down.shape[1]
    nf = ff // tf
    grid = (p // tm, nf)

    def f_eff(i, f, nv):
        return jnp.where(i < nv[0], f, nf - 1)

    in_specs = [
        pl.BlockSpec((tm, d), lambda i, f, te, nv: (i, 0)),
        pl.BlockSpec((1, d), lambda i, f, te, nv: (0, 0)),
        pl.BlockSpec((None, d, tf), lambda i, f, te, nv: (te[i], 0, f_eff(i, f, nv))),
        pl.BlockSpec((None, d, tf), lambda i, f, te, nv: (te[i], 0, nf + f_eff(i, f, nv))),
        pl.BlockSpec((None, tf, d), lambda i, f, te, nv: (te[i], f_eff(i, f, nv), 0)),
    ]
    grid_spec = pltpu.PrefetchScalarGridSpec(
        num_scalar_prefetch=2, grid=grid, in_specs=in_specs,
        out_specs=pl.BlockSpec((tm, d), lambda i, f, te, nv: (i, 0)),
        scratch_shapes=[pltpu.VMEM((tm, d), BF16), pltpu.VMEM((tm, d), F32)],
    )
    return pl.pallas_call(
        functools.partial(_ffn_kernel, add_res=add_res), grid_spec=grid_spec,
        out_shape=jax.ShapeDtypeStruct((p, d), F32),
        compiler_params=_params("parallel", "arbitrary"), name="grouped_swiglu",
    )(tile_expert, n_valid, x, g, w_gu, w_gu, w_down)


def _flash_kernel(lam_ref, q_ref, k_ref, v_ref, g_ref, o_ref, m_ref, l_ref, acc_ref, vt_ref,
                  s_ref, *,
                  mode, tq, out_scale):
    i = pl.program_id(2)
    s_len = v_ref.shape[0]
    sub = lax.broadcasted_iota(I32, (LANES, tq), 0)
    top = sub < (LANES // 2)

    @pl.when(i == 0)
    def _():
        for c in range(s_len // tq):
            sl = slice(c * tq, (c + 1) * tq)
            vt_ref[:, sl] = v_ref[sl, :].T

    if mode == "mla":
        qts = (q_ref[:, :LANES].T, q_ref[:, LANES:].T)
    else:
        qt = q_ref[...].T
        zero = jnp.zeros_like(qt)
        qts = (jnp.where(top, qt, zero), jnp.where(top, zero, qt))

    m_ref[...] = jnp.full_like(m_ref, NEG)
    l_ref[...] = jnp.zeros_like(l_ref)
    acc_ref[...] = jnp.zeros_like(acc_ref)

    def k_tiles(start):
        if mode == "mla":
            return (k_ref[pl.ds(start, tq), :LANES], k_ref[pl.ds(start, tq), LANES:])
        kt = k_ref[pl.ds(start, tq), :]
        return (kt, kt)

    def update(idx, s, vt_tile):
        m_prev = m_ref[idx]
        m_new = jnp.maximum(m_prev, jnp.max(s, axis=0, keepdims=True))
        alpha = jnp.exp2(m_prev - m_new)
        p = jnp.exp2(s - m_new)
        l_ref[idx] = alpha * l_ref[idx] + jnp.sum(p, axis=0, keepdims=True)
        acc_ref[idx] = alpha * acc_ref[idx] + _dot(vt_tile, p.astype(BF16))
        m_ref[idx] = m_new

    def scores_into(slot, t):
        kts = k_tiles(pl.multiple_of(t * tq, tq))
        for idx in range(2):
            s_ref[slot, idx] = _dot(kts[idx], qts[idx])

    def consume(slot, t, masked):
        vt_tile = vt_ref[:, pl.ds(pl.multiple_of(t * tq, tq), tq)]
        for idx in range(2):
            s = s_ref[slot, idx]
            if masked:
                row = lax.broadcasted_iota(I32, (tq, tq), 0)
                col = lax.broadcasted_iota(I32, (tq, tq), 1)
                shift = CHUNK.bit_length() - 1
                s = jnp.where((row >> shift) <= (col >> shift), s, NEG)
            update(idx, s, vt_tile)

    scores_into(0, 0)

    def tile_pair(u, carry):
        t = 2 * u
        scores_into(1, t + 1)
        consume(0, t, False)
        scores_into(0, t + 2)
        consume(1, t + 1, False)
        return carry

    lax.fori_loop(0, i >> 1, tile_pair, 0)

    @pl.when((i & 1) == 1)
    def _():
        scores_into(1, i)
        consume(0, i - 1, False)
        consume(1, i, True)

    @pl.when((i & 1) == 0)
    def _():
        consume(0, i, True)

    o0 = acc_ref[0] * (1.0 / l_ref[0])
    o1 = acc_ref[1] * (1.0 / l_ref[1])
    if mode == "mla":
        o_ref[...] = jnp.where(top, o0, o1).T.astype(o_ref.dtype)
    else:
        o = (o0 - lam_ref[0] * o1).T
        o_ref[...] = (_rms(o, g_ref[...]) * out_scale).astype(o_ref.dtype)


def flash_attention(q_arr, k_arr, v_arr, *, mode, q_blk0, k_blk0, v_blk0, n_groups, tq,
                    lam=None, subln_g=None, out_scale=1.0):
    b, s, _ = q_arr.shape
    wqk = 2 * LANES if mode == "mla" else LANES
    if lam is None:
        lam = jnp.zeros((1,), F32)
    if subln_g is None:
        subln_g = jnp.ones((1, LANES), F32)
    grid = (b, n_groups, s // tq)
    in_specs = [
        pl.BlockSpec((None, tq, wqk), lambda bb, g, i, lam_r: (bb, i, q_blk0 + g)),
        pl.BlockSpec((None, s, wqk), lambda bb, g, i, lam_r: (bb, 0, k_blk0 + g)),
        pl.BlockSpec((None, s, LANES), lambda bb, g, i, lam_r: (bb, 0, v_blk0 + g)),
        pl.BlockSpec((1, LANES), lambda bb, g, i, lam_r: (0, 0)),
    ]
    grid_spec = pltpu.PrefetchScalarGridSpec(
        num_scalar_prefetch=1, grid=grid, in_specs=in_specs,
        out_specs=pl.BlockSpec((None, tq, LANES), lambda bb, g, i, lam_r: (bb, i, g)),
        scratch_shapes=[pltpu.VMEM((2, 1, tq), F32), pltpu.VMEM((2, 1, tq), F32),
                        pltpu.VMEM((2, LANES, tq), F32), pltpu.VMEM((LANES, s), BF16),
                        pltpu.VMEM((2, 2, tq, tq), F32)],
    )
    kern = functools.partial(_flash_kernel, mode=mode, tq=tq, out_scale=out_scale)
    return pl.pallas_call(
        kern, grid_spec=grid_spec,
        out_shape=jax.ShapeDtypeStruct((b, s, n_groups * LANES), BF16),
        compiler_params=_params("parallel", "parallel", "arbitrary"), name="flash_" + mode,
    )(lam, q_arr, k_arr, v_arr, subln_g)


BAND_TQ = 256
BAND_TILES = 3


def _band_kernel(q_ref, k0_ref, k1_ref, k2_ref, v0_ref, v1_ref, v2_ref, bias_ref, o_ref, *, scale):
    i = pl.program_id(2)
    lane = lax.broadcasted_iota(I32, (BAND_TQ, LANES), 1)
    low = lane < (LANES // 2)
    q = q_ref[...].astype(F32) * scale
    qs = (jnp.where(low, q, 0.0).astype(BF16), jnp.where(low, 0.0, q).astype(BF16))
    k_refs = (k0_ref, k1_ref, k2_ref)
    v_refs = (v0_ref, v1_ref, v2_ref)
    outs = []
    for h in range(2):
        qh = qs[h]
        scores = []
        for j in range(BAND_TILES):
            pen = jnp.where(i + j >= BAND_TILES - 1, 0.0, NEG).astype(F32)
            scores.append(_dot_nt(qh, k_refs[j][...]) + bias_ref[h, j] + pen)
        m = jnp.maximum(jnp.maximum(jnp.max(scores[0], axis=-1, keepdims=True),
                                    jnp.max(scores[1], axis=-1, keepdims=True)),
                        jnp.max(scores[2], axis=-1, keepdims=True))
        l = jnp.zeros_like(m)
        o = jnp.zeros((BAND_TQ, LANES), F32)
        for j in range(BAND_TILES):
            p = jnp.exp(scores[j] - m)
            l = l + jnp.sum(p, axis=-1, keepdims=True)
            o = o + _dot(p.astype(BF16), v_refs[j][...])
        outs.append(o * (1.0 / l))
    o_ref[...] = jnp.where(low, outs[0], outs[1]).astype(o_ref.dtype)


def band_bias_tiles(rel_bias):
    h = rel_bias.shape[0]
    n = BAND_TQ
    r = np.arange(n)[:, None]
    c = np.arange(n)[None, :]
    e_of = np.zeros(2 * n, np.int64)
    e_of[:n] = -np.arange(n)
    e_of[n + 1:] = n - 1 - np.arange(n - 1)
    tiles = []
    for j in range(BAND_TILES):
        rel = (BAND_TILES - 1 - j) * n + e_of
        gen = rel_bias[:, np.clip(rel, -A_MAX_REL, A_MAX_REL) + A_MAX_REL].astype(F32)
        skew = jnp.tile(gen, (1, n))[:, :n * (2 * n - 1)].reshape(h, n, 2 * n - 1)[:, :, :n]
        dist = (r // CHUNK) - (c // CHUNK) + (BAND_TILES - 1 - j) * (n // CHUNK)
        ok = (dist >= 0) & (dist <= A_LEFT_CHUNKS)
        tiles.append(jnp.where(ok[None], skew, NEG))
    t = jnp.stack(tiles, axis=1)
    return t.reshape(h // 2, 2, BAND_TILES, BAND_TQ, BAND_TQ)


def band_attention(proj, bias_tiles, *, q_blk0, k_blk0, v_blk0):
    b, s, _ = proj.shape
    n_pairs = bias_tiles.shape[0]
    grid = (n_pairs, b, s // BAND_TQ)

    def kv_spec(blk0, j):
        return pl.BlockSpec(
            (None, BAND_TQ, LANES),
            lambda p, bb, i: (bb, jnp.maximum(i + j - (BAND_TILES - 1), 0), blk0 + p))

    in_specs = [pl.BlockSpec((None, BAND_TQ, LANES), lambda p, bb, i: (bb, i, q_blk0 + p))]
    in_specs += [kv_spec(k_blk0, j) for j in range(BAND_TILES)]
    in_specs += [kv_spec(v_blk0, j) for j in range(BAND_TILES)]
    in_specs += [pl.BlockSpec((None, 2, BAND_TILES, BAND_TQ, BAND_TQ),
                              lambda p, bb, i: (p, 0, 0, 0, 0))]
    return pl.pallas_call(
        functools.partial(_band_kernel, scale=A_HEAD_DIM ** -0.5), grid=grid, in_specs=in_specs,
        out_specs=pl.BlockSpec((None, BAND_TQ, LANES), lambda p, bb, i: (bb, i, p)),
        out_shape=jax.ShapeDtypeStruct((b, s, n_pairs * LANES), BF16),
        compiler_params=_params("parallel", "parallel", "parallel"), name="band_attention",
    )(proj, proj, proj, proj, proj, proj, proj, bias_tiles)


def _mla_prep_kernel(cq_ref, ckv_ref, kr_ref, gq_ref, gkv_ref, wq_ref, wk_ref, wv_ref,
                     cq_cos_ref, cq_sin_ref, ck_cos_ref, ck_sin_ref, q_ref, k_ref, v_ref):
    half = B_ROPE_DIM // 2
    nq = _rms(cq_ref[...].astype(F32), gq_ref[...]).astype(BF16)
    nkv = _rms(ckv_ref[...].astype(F32), gkv_ref[...]).astype(BF16)
    q = _dot(nq, wq_ref[...])
    k = _dot(nkv, wk_ref[...])
    v_ref[...] = _dot(nkv, wv_ref[...]).astype(v_ref.dtype)
    kr = _rope_lanes(kr_ref[...].astype(F32), ck_cos_ref[...], ck_sin_ref[...], half)
    lane = lax.broadcasted_iota(I32, kr.shape, 1)
    kr = jnp.where(lane < B_ROPE_DIM, kr, 0.0)
    kr = pltpu.roll(kr, B_NOPE_DIM, 1)
    cos = cq_cos_ref[...]
    sin = cq_sin_ref[...]
    for c in range(q.shape[1] // LANES):
        sl = slice(c * LANES, (c + 1) * LANES)
        q_ref[:, sl] = _rope_lanes(q[:, sl], cos, sin, half).astype(q_ref.dtype)
        k_ref[:, sl] = (k[:, sl] + kr).astype(k_ref.dtype)


def mla_prep(proj, gq, gkv, wq, wk, wv, tables, *, tm, cq_blk, ckv_blk, kr_blk):
    t = proj.shape[0]
    nq = wq.shape[0]
    nkv = wk.shape[0]
    whole = lambda a: pl.BlockSpec(a.shape, lambda i: (0,) * a.ndim)
    tab = lambda k: pl.BlockSpec((None, tm, LANES), lambda i: (k, i, 0))
    in_specs = [
        pl.BlockSpec((tm, nq), lambda i: (i, cq_blk)),
        pl.BlockSpec((tm, nkv), lambda i: (i, ckv_blk)),
        pl.BlockSpec((tm, LANES), lambda i: (i, kr_blk)),
        whole(gq), whole(gkv), whole(wq), whole(wk), whole(wv),
        tab(0), tab(1), tab(2), tab(3),
    ]
    out_specs = [pl.BlockSpec((tm, wq.shape[1]), lambda i: (i, 0)),
                 pl.BlockSpec((tm, wk.shape[1]), lambda i: (i, 0)),
                 pl.BlockSpec((tm, wv.shape[1]), lambda i: (i, 0))]
    out_shape = [jax.ShapeDtypeStruct((t, wq.shape[1]), BF16),
                 jax.ShapeDtypeStruct((t, wk.shape[1]), BF16),
                 jax.ShapeDtypeStruct((t, wv.shape[1]), BF16)]
    return pl.pallas_call(
        _mla_prep_kernel, grid=(t // tm,), in_specs=in_specs, out_specs=out_specs,
        out_shape=out_shape, compiler_params=_params("parallel"), name="mla_prep",
    )(proj, proj, proj, gq, gkv, wq, wk, wv, tables, tables, tables, tables)


def _router_kernel(h_ref, g_ref, wr_ref, gate_ref, idx_ref, cnt_ref, carry_ref):
    tb = h_ref.shape[0]

    @pl.when(pl.program_id(0) == 0)
    def _():
        carry_ref[...] = jnp.zeros_like(carry_ref)

    n = _rms(h_ref[...], g_ref[...])
    logits = jnp.dot(n, wr_ref[...], preferred_element_type=F32, precision=lax.Precision.HIGHEST)
    lane = lax.broadcasted_iota(I32, (tb, LANES), 1).astype(F32)
    neg_inf = jnp.float32(-jnp.inf)
    lg = jnp.where(lane < N_EXPERTS, logits, neg_inf)
    m1 = jnp.max(lg, axis=-1, keepdims=True)
    e0 = jnp.min(jnp.where(lg == m1, lane, float(LANES)), axis=-1, keepdims=True)
    lg2 = jnp.where(lane == e0, neg_inf, lg)
    m2 = jnp.max(lg2, axis=-1, keepdims=True)
    e1 = jnp.min(jnp.where(lg2 == m2, lane, float(LANES)), axis=-1, keepdims=True)
    t = jnp.exp(m2 - m1)
    g0 = 1.0 / (1.0 + t)
    g1 = t / (1.0 + t)

    sel = ((lane == e0) | (lane == e1)).astype(F32)
    row = lax.broadcasted_iota(I32, (tb, tb), 0)
    col = lax.broadcasted_iota(I32, (tb, tb), 1)
    strict_lower = (col < row).astype(BF16)
    carry = carry_ref[...]
    excl = _dot(strict_lower, sel.astype(BF16)) + carry
    r0 = jnp.sum(jnp.where(lane == e0, excl, 0.0), axis=-1, keepdims=True)
    r1 = jnp.sum(jnp.where(lane == e1, excl, 0.0), axis=-1, keepdims=True)
    carry = carry + jnp.sum(sel, axis=0, keepdims=True)
    carry_ref[...] = carry

    gate_ref[...] = jnp.where(lane == 0, g0, jnp.where(lane == 1, g1, 0.0))
    idx_ref[...] = jnp.where(
        lane == 0, e0, jnp.where(lane == 1, e1, jnp.where(
            lane == 2, r0, jnp.where(lane == 3, r1, 0.0)))).astype(I32)
    cnt_ref[...] = jnp.broadcast_to(carry, cnt_ref.shape).astype(I32)


def router(h, g, wr_pad, *, tb):
    t, d = h.shape
    return pl.pallas_call(
        _router_kernel, grid=(t // tb,),
        in_specs=[pl.BlockSpec((tb, d), lambda i: (i, 0)),
                  pl.BlockSpec((1, d), lambda i: (0, 0)),
                  pl.BlockSpec((d, LANES), lambda i: (0, 0))],
        out_specs=[pl.BlockSpec((tb, LANES), lambda i: (i, 0)),
                   pl.BlockSpec((tb, LANES), lambda i: (i, 0)),
                   pl.BlockSpec((8, LANES), lambda i: (0, 0))],
        out_shape=[jax.ShapeDtypeStruct((t, LANES), F32),
                   jax.ShapeDtypeStruct((t, LANES), I32),
                   jax.ShapeDtypeStruct((8, LANES), I32)],
        scratch_shapes=[pltpu.VMEM((1, LANES), F32)],
        compiler_params=_params("arbitrary"), name="router",
    )(h, g, wr_pad)


def _row_copy(src_hbm, row, dst_ref, r, sem):
    return pltpu.make_async_copy(src_hbm.at[pl.ds(row, 1), :], dst_ref.at[pl.ds(r, 1), :], sem)


ROW_UNROLL = 8


def _for_rows(rows, body):
    def outer(o, carry):
        for u in range(ROW_UNROLL):
            body(o * ROW_UNROLL + u, u)
        return carry

    lax.fori_loop(0, rows // ROW_UNROLL, outer, 0)


def _gather_kernel(idx_ref, src_hbm, o_ref, sem):
    rows = o_ref.shape[0]
    _for_rows(rows, lambda r, u: _row_copy(src_hbm, idx_ref[0, 0, r], o_ref, r, sem).start(priority=u % 2))
    _for_rows(rows, lambda r, u: _row_copy(src_hbm, 0, o_ref, r, sem).wait())


def gather_rows(src, idx, *, rows):
    p = idx.shape[0]
    d = src.shape[1]
    idx3 = idx.reshape(p // rows, 1, rows)
    return pl.pallas_call(
        _gather_kernel, grid=(p // rows,),
        in_specs=[pl.BlockSpec((1, 1, rows), lambda i: (i, 0, 0), memory_space=pltpu.SMEM),
                  pl.BlockSpec(memory_space=pl.ANY)],
        out_specs=pl.BlockSpec((rows, d), lambda i: (i, 0)),
        out_shape=jax.ShapeDtypeStruct((p, d), src.dtype),
        scratch_shapes=[pltpu.SemaphoreType.DMA(())],
        compiler_params=_params("arbitrary"), name="gather_rows",
    )(idx3, src)


def _combine_kernel(p0_ref, p1_ref, h_ref, gate_ref, ys_hbm, o_ref, buf0, buf1, sem):
    rows = h_ref.shape[0]

    def start(r, u):
        _row_copy(ys_hbm, p0_ref[0, 0, r], buf0, r, sem).start(priority=0)
        _row_copy(ys_hbm, p1_ref[0, 0, r], buf1, r, sem).start(priority=1)

    def wait(r, u):
        _row_copy(ys_hbm, 0, buf0, r, sem).wait()
        _row_copy(ys_hbm, 0, buf1, r, sem).wait()

    _for_rows(rows, start)
    _for_rows(rows, wait)
    gates = gate_ref[...]
    o_ref[...] = h_ref[...] + gates[:, 0:1] * buf0[...] + gates[:, 1:2] * buf1[...]


def combine_rows(h, gates, ys, pos0, pos1, *, rows):
    t, d = h.shape
    smem = lambda: pl.BlockSpec((1, 1, rows), lambda i: (i, 0, 0), memory_space=pltpu.SMEM)
    return pl.pallas_call(
        _combine_kernel, grid=(t // rows,),
        in_specs=[smem(), smem(),
                  pl.BlockSpec((rows, d), lambda i: (i, 0)),
                  pl.BlockSpec((rows, LANES), lambda i: (i, 0)),
                  pl.BlockSpec(memory_space=pl.ANY)],
        out_specs=pl.BlockSpec((rows, d), lambda i: (i, 0)),
        out_shape=jax.ShapeDtypeStruct((t, d), F32),
        scratch_shapes=[pltpu.VMEM((rows, d), F32), pltpu.VMEM((rows, d), F32),
                        pltpu.SemaphoreType.DMA(())],
        compiler_params=_params("arbitrary"), name="combine_rows",
    )(pos0.reshape(t // rows, 1, rows), pos1.reshape(t // rows, 1, rows), h, gates, ys)


def _rmsnorm_kernel(x_ref, g_ref, o_ref):
    o_ref[...] = _rms(x_ref[...], g_ref[...])


def rmsnorm(x, g, *, tm):
    t, d = x.shape
    return pl.pallas_call(
        _rmsnorm_kernel, grid=(t // tm,),
        in_specs=[pl.BlockSpec((tm, d), lambda i: (i, 0)), pl.BlockSpec((1, d), lambda i: (0, 0))],
        out_specs=pl.BlockSpec((tm, d), lambda i: (i, 0)),
        out_shape=jax.ShapeDtypeStruct((t, d), F32),
        compiler_params=_params("parallel"), name="final_rmsnorm",
    )(x, g)


def _rope_cos_sin(positions, dim):
    inv_freq = 1.0 / jnp.power(ROPE_THETA, jnp.arange(0, dim, 2, dtype=F32) / dim)
    ang = positions.reshape(-1).astype(F32)[:, None] * inv_freq
    return jnp.cos(ang), jnp.sin(ang)


def _diff_tables(positions):
    cos, sin = _rope_cos_sin(positions, C_HEAD_DIM)
    cos_l = jnp.tile(cos, (1, 4))
    sin_l = jnp.tile(jnp.concatenate([-sin, sin], axis=1), (1, 2))
    k_tab = jnp.stack([cos_l, sin_l])
    return jnp.stack([k_tab * (C_HEAD_DIM ** -0.5 * LOG2E), k_tab])


def _mla_tables(positions):
    cos, sin = _rope_cos_sin(positions, B_ROPE_DIM)
    t = cos.shape[0]
    scale = (B_NOPE_DIM + B_ROPE_DIM) ** -0.5 * LOG2E
    ones = lambda n: jnp.ones((t, n), F32)
    zeros = lambda n: jnp.zeros((t, n), F32)
    q_cos = jnp.concatenate([ones(B_NOPE_DIM), cos, cos, ones(32)], axis=1) * scale
    q_sin = jnp.concatenate([zeros(B_NOPE_DIM), -sin, sin, zeros(32)], axis=1) * scale
    k_cos = jnp.concatenate([cos, cos, ones(96)], axis=1)
    k_sin = jnp.concatenate([-sin, sin, zeros(96)], axis=1)
    return jnp.stack([q_cos, q_sin, k_cos, k_sin])


def _routing_plan(idx_out, counts, *, tm, n_tiles):
    t = idx_out.shape[0]
    e0, e1, r0, r1 = idx_out[:, 0], idx_out[:, 1], idx_out[:, 2], idx_out[:, 3]
    padded = ((counts + tm - 1) // tm) * tm
    ends = jnp.cumsum(padded)
    offs = ends - padded
    experts = jnp.arange(N_EXPERTS, dtype=I32)[None, :]
    pos0 = jnp.sum(jnp.where(e0[:, None] == experts, offs[None, :], 0), axis=1) + r0
    pos1 = jnp.sum(jnp.where(e1[:, None] == experts, offs[None, :], 0), axis=1) + r1
    tok = jnp.arange(t, dtype=I32)
    src = jnp.zeros((n_tiles * tm,), I32).at[pos0].set(tok).at[pos1].set(tok)
    n_valid = (ends[-1] // tm).astype(I32)
    tile = jnp.minimum(jnp.arange(n_tiles, dtype=I32), n_valid - 1)
    tile_expert = jnp.minimum(jnp.sum((tile * tm)[:, None] >= ends[None, :], axis=1),
                              N_EXPERTS - 1)
    return pos0.astype(I32), pos1.astype(I32), src, tile_expert.astype(I32), n_valid.reshape(1)


def _pad_cols(w, n):
    return jnp.pad(w, ((0, 0), (0, n - w.shape[1])))


def kernel(x, positions, mix_norm_g, ffn_norm_g, w_in_even, rel_bias_a, q_norm_b, w_uq_b, kv_norm_b, w_ukv_b, w_out_even, w_in_odd, lambda_q1, lambda_k1, lambda_q2, lambda_k2, subln_g, w_out_odd, w_gu_dense, w_down_dense, w_router, w_gu_moe, w_down_moe, final_norm_g):
    b, s, d = x.shape
    t = b * s
    depth = mix_norm_g.shape[0]
    a_width = 4 * LANES
    tm = 512
    tm_moe = 512
    n_tiles_moe = (2 * t) // tm_moe + N_EXPERTS

    diff_tab = _diff_tables(positions)
    mla_tab = _mla_tables(positions)
    row = lambda v: v.reshape(1, -1).astype(F32)

    h = x.reshape(t, d).astype(F32)
    for layer in range(depth):
        i = layer // 2
        if layer % 2 == 0:
            w_in = _pad_cols(w_in_even[i], 2048).astype(BF16)
            proj = rms_matmul(h, row(mix_norm_g[layer]), w_in, tm=tm, tn=1024)
            proj3 = proj.reshape(b, s, -1)
            out_a = band_attention(proj3, band_bias_tiles(rel_bias_a[i]),
                                   q_blk0=0, k_blk0=4, v_blk0=8)
            n_heads = w_uq_b.shape[2] // (B_NOPE_DIM + B_ROPE_DIM)
            wq = jnp.pad(w_uq_b[i].reshape(-1, n_heads, B_NOPE_DIM + B_ROPE_DIM),
                         ((0, 0), (0, 0), (0, 32))).reshape(-1, n_heads * LANES).astype(BF16)
            wkv = w_ukv_b[i].reshape(-1, n_heads, B_NOPE_DIM + B_V_DIM)
            wk = jnp.pad(wkv[:, :, :B_NOPE_DIM], ((0, 0), (0, 0), (0, LANES - B_NOPE_DIM)))
            wk = wk.reshape(-1, n_heads * LANES).astype(BF16)
            wv = wkv[:, :, B_NOPE_DIM:].reshape(-1, n_heads * B_V_DIM).astype(BF16)
            q_b, k_b, v_b = mla_prep(proj, row(q_norm_b[i]), row(kv_norm_b[i]), wq, wk, wv, mla_tab,
                                     tm=tm, cq_blk=6, ckv_blk=14, kr_blk=15)
            out_b = flash_attention(q_b.reshape(b, s, -1), k_b.reshape(b, s, -1),
                                    v_b.reshape(b, s, -1), mode="mla", q_blk0=0, k_blk0=0,
                                    v_blk0=0, n_groups=n_heads // 2, tq=512)
            w_out = w_out_even[i].astype(BF16)
            h = matmul_residual([out_a.reshape(t, -1), out_b.reshape(t, -1)],
                                [w_out[:a_width], w_out[a_width:]], h, tm=tm)
            n_tiles = t // tm
            h = grouped_swiglu(h, row(ffn_norm_g[layer]), w_gu_dense[i][None].astype(BF16),
                               w_down_dense[i][None].astype(BF16), jnp.zeros((n_tiles,), I32),
                               jnp.full((1,), n_tiles, I32), tm=tm, tf=1408, add_res=True)
        else:
            c_width = w_out_odd.shape[1]
            n_heads = c_width // LANES
            qkv = rms_matmul(h, row(mix_norm_g[layer]), w_in_odd[i].astype(BF16), tm=tm, tn=512,
                             rope=(diff_tab, c_width // 512, C_HEAD_DIM // 2))
            lam_init = 0.8 - 0.6 * math.exp(-0.3 * layer)
            lam = (jnp.exp(jnp.sum(lambda_q1[i].astype(F32) * lambda_k1[i].astype(F32)))
                   - jnp.exp(jnp.sum(lambda_q2[i].astype(F32) * lambda_k2[i].astype(F32)))
                   + lam_init).reshape(1).astype(F32)
            qkv3 = qkv.reshape(b, s, -1)
            out_c = flash_attention(qkv3, qkv3, qkv3, mode="diff", q_blk0=0, k_blk0=n_heads,
                                    v_blk0=2 * n_heads, n_groups=n_heads, tq=512, lam=lam,
                                    subln_g=row(subln_g[i]), out_scale=1.0 - lam_init)
            h = matmul_residual([out_c.reshape(t, -1)], [w_out_odd[i].astype(BF16)], h, tm=tm)

            g_ffn = row(ffn_norm_g[layer])
            wr_pad = _pad_cols(w_router[i].astype(F32), LANES)
            gates, idx_out, cnt = router(h, g_ffn, wr_pad, tb=512)
            pos0, pos1, src, tile_expert, n_valid = _routing_plan(
                idx_out, cnt[0, :N_EXPERTS], tm=tm_moe, n_tiles=n_tiles_moe)
            xs = gather_rows(h, src, rows=256)
            ys = grouped_swiglu(xs, g_ffn, w_gu_moe[i].astype(BF16), w_down_moe[i].astype(BF16),
                                tile_expert, n_valid, tm=tm_moe, tf=512, add_res=False)
            h = combine_rows(h, gates, ys, pos0, pos1, rows=256)

    out = rmsnorm(h, row(final_norm_g), tm=tm)
    return out.reshape(b, s, d)
```

```python
import functools
import math

import jax
import jax.numpy as jnp
import numpy as np
from jax import lax
from jax.experimental import pallas as pl
from jax.experimental.pallas import tpu as pltpu

F32 = jnp.float32
BF16 = jnp.bfloat16
I32 = jnp.int32

NORM_EPS = 1e-6
ROPE_THETA = 10000.0
NEG = -1e30
LOG2E = math.log2(math.e)
LANES = 128
CHUNK = 64

A_LEFT_CHUNKS = 8
A_MAX_REL = 256
A_HEAD_DIM = 64
B_NOPE_DIM = 64
B_ROPE_DIM = 32
B_V_DIM = 64
C_HEAD_DIM = 64
N_EXPERTS = 8

VMEM_LIMIT = 56 * 1024 * 1024


def _params(*sem):
    return pltpu.CompilerParams(dimension_semantics=sem, vmem_limit_bytes=VMEM_LIMIT)


def _rms(x, g):
    ms = jnp.mean(x * x, axis=-1, keepdims=True)
    return x * lax.rsqrt(ms + NORM_EPS) * g


def _dot(a, b):
    return jnp.dot(a, b, preferred_element_type=F32)


def _dot_nt(a, b):
    return lax.dot_general(a, b, (((1,), (1,)), ((), ())), preferred_element_type=F32)


def _rope_lanes(x, cos, sin, half):
    if 2 * half == LANES:
        swapped = pltpu.roll(x, half, 1)
    else:
        lane = lax.broadcasted_iota(I32, x.shape, 1)
        fwd = pltpu.roll(x, LANES - half, 1)
        bwd = pltpu.roll(x, half, 1)
        swapped = jnp.where((lane & half) == 0, fwd, bwd)
    return x * cos + swapped * sin


def _rms_matmul_kernel(x_ref, g_ref, w_ref, o_ref, xn_ref):
    @pl.when(pl.program_id(1) == 0)
    def _():
        xn_ref[...] = _rms(x_ref[...], g_ref[...]).astype(BF16)

    o_ref[...] = _dot(xn_ref[...], w_ref[...]).astype(o_ref.dtype)


def _rms_matmul_rope_kernel(x_ref, g_ref, w_ref, cos_ref, sin_ref, o_ref, xn_ref, *,
                            n_rope_tiles, half):
    j = pl.program_id(1)

    @pl.when(j == 0)
    def _():
        xn_ref[...] = _rms(x_ref[...], g_ref[...]).astype(BF16)

    acc = _dot(xn_ref[...], w_ref[...])

    @pl.when(j < n_rope_tiles)
    def _():
        cos = cos_ref[...]
        sin = sin_ref[...]
        for c in range(acc.shape[1] // LANES):
            sl = slice(c * LANES, (c + 1) * LANES)
            o_ref[:, sl] = _rope_lanes(acc[:, sl], cos, sin, half).astype(o_ref.dtype)

    @pl.when(j >= n_rope_tiles)
    def _():
        o_ref[...] = acc.astype(o_ref.dtype)


def rms_matmul(x, g, w, *, tm, tn, rope=None):
    t, d = x.shape
    n = w.shape[1]
    grid = (t // tm, n // tn)
    x_spec = pl.BlockSpec((tm, d), lambda i, j: (i, 0))
    g_spec = pl.BlockSpec((1, d), lambda i, j: (0, 0))
    w_spec = pl.BlockSpec((d, tn), lambda i, j: (0, j))
    o_spec = pl.BlockSpec((tm, tn), lambda i, j: (i, j))
    scratch = [pltpu.VMEM((tm, d), BF16)]
    out_shape = jax.ShapeDtypeStruct((t, n), BF16)
    if rope is None:
        return pl.pallas_call(
            _rms_matmul_kernel, grid=grid, in_specs=[x_spec, g_spec, w_spec], out_specs=o_spec,
            out_shape=out_shape, scratch_shapes=scratch,
            compiler_params=_params("parallel", "arbitrary"), name="rms_matmul",
        )(x, g, w)
    tables, per_part, half = rope
    cos_spec = pl.BlockSpec((None, None, tm, LANES),
                            lambda i, j: (jnp.minimum(j // per_part, 1), 0, i, 0))
    sin_spec = pl.BlockSpec((None, None, tm, LANES),
                            lambda i, j: (jnp.minimum(j // per_part, 1), 1, i, 0))
    kern = functools.partial(_rms_matmul_rope_kernel, n_rope_tiles=2 * per_part, half=half)
    return pl.pallas_call(
        kern, grid=grid, in_specs=[x_spec, g_spec, w_spec, cos_spec, sin_spec], out_specs=o_spec,
        out_shape=out_shape, scratch_shapes=scratch,
        compiler_params=_params("parallel", "arbitrary"), name="rms_matmul_rope",
    )(x, g, w, tables, tables)


def _mm_res_kernel(*refs, n_in):
    res_ref = refs[2 * n_in]
    o_ref = refs[2 * n_in + 1]
    acc = res_ref[...]
    for k in range(n_in):
        acc = acc + _dot(refs[k][...], refs[n_in + k][...])
    o_ref[...] = acc


def matmul_residual(a_list, w_list, res, *, tm):
    t, n = res.shape
    n_in = len(a_list)
    in_specs = [pl.BlockSpec((tm, a.shape[1]), lambda i: (i, 0)) for a in a_list]
    in_specs += [pl.BlockSpec(w.shape, lambda i: (0, 0)) for w in w_list]
    in_specs += [pl.BlockSpec((tm, n), lambda i: (i, 0))]
    return pl.pallas_call(
        functools.partial(_mm_res_kernel, n_in=n_in), grid=(t // tm,), in_specs=in_specs,
        out_specs=pl.BlockSpec((tm, n), lambda i: (i, 0)),
        out_shape=jax.ShapeDtypeStruct((t, n), F32),
        compiler_params=_params("parallel"), name="matmul_residual",
    )(*a_list, *w_list, res)


def _ffn_kernel(te_ref, nv_ref, x_ref, g_ref, wg_ref, wu_ref, wd_ref, o_ref, xn_ref, acc_ref, *,
                add_res):
    del te_ref
    i = pl.program_id(0)
    f = pl.program_id(1)
    nf = pl.num_programs(1)
    valid = i < nv_ref[0]

    @pl.when(valid & (f == 0))
    def _():
        xn_ref[...] = _rms(x_ref[...], g_ref[...]).astype(BF16)
        acc_ref[...] = jnp.zeros_like(acc_ref)

    @pl.when(valid)
    def _():
        xn = xn_ref[...]
        gate = _dot(xn, wg_ref[...])
        up = _dot(xn, wu_ref[...])
        act = (gate * jax.nn.sigmoid(gate) * up).astype(BF16)
        acc_ref[...] += _dot(act, wd_ref[...])

    @pl.when(valid & (f == nf - 1))
    def _():
        if add_res:
            o_ref[...] = x_ref[...] + acc_ref[...]
        else:
            o_ref[...] = acc_ref[...]

    @pl.when(jnp.logical_not(valid) & (f == nf - 1))
    def _():
        o_ref[...] = jnp.zeros_like(o_ref)


def grouped_swiglu(x, g, w_gu, w_down, tile_expert, n_valid, *, tm, tf, add_res):
    p, d = x.shape
    ff = w_down.shape[1]
    nf = ff // tf
    grid = (p // tm, nf)

    def f_eff(i, f, nv):
        return jnp.where(i < nv[0], f, nf - 1)

    in_specs = [
        pl.BlockSpec((tm, d), lambda i, f, te, nv: (i, 0)),
        pl.BlockSpec((1, d), lambda i, f, te, nv: (0, 0)),
        pl.BlockSpec((None, d, tf), lambda i, f, te, nv: (te[i], 0, f_eff(i, f, nv))),
        pl.BlockSpec((None, d, tf), lambda i, f, te, nv: (te[i], 0, nf + f_eff(i, f, nv))),
        pl.BlockSpec((None, tf, d), lambda i, f, te, nv: (te[i], f_eff(i, f, nv), 0)),
    ]
    grid_spec = pltpu.PrefetchScalarGridSpec(
        num_scalar_prefetch=2, grid=grid, in_specs=in_specs,
        out_specs=pl.BlockSpec((tm, d), lambda i, f, te, nv: (i, 0)),
        scratch_shapes=[pltpu.VMEM((tm, d), BF16), pltpu.VMEM((tm, d), F32)],
    )
    return pl.pallas_call(
        functools.partial(_ffn_kernel, add_res=add_res), grid_spec=grid_spec,
        out_shape=jax.ShapeDtypeStruct((p, d), F32),
        compiler_params=_params("parallel", "arbitrary"), name="grouped_swiglu",
    )(tile_expert, n_valid, x, g, w_gu, w_gu, w_down)


def _flash_kernel(lam_ref, q_ref, k_ref, v_ref, g_ref, o_ref, m_ref, l_ref, acc_ref, vt_ref,
                  s_ref, *,
                  mode, tq, out_scale):
    i = pl.program_id(2)
    s_len = v_ref.shape[0]
    sub = lax.broadcasted_iota(I32, (LANES, tq), 0)
    top = sub < (LANES // 2)

    @pl.when(i == 0)
    def _():
        for c in range(s_len // tq):
            sl = slice(c * tq, (c + 1) * tq)
            vt_ref[:, sl] = v_ref[sl, :].T

    if mode == "mla":
        qts = (q_ref[:, :LANES].T, q_ref[:, LANES:].T)
    else:
        qt = q_ref[...].T
        zero = jnp.zeros_like(qt)
        map1 = (sub & (C_HEAD_DIM // 2)) == 0
        qts = (jnp.where(map1, qt, zero), jnp.where(map1, zero, qt))

    m_ref[...] = jnp.full_like(m_ref, NEG)
    l_ref[...] = jnp.zeros_like(l_ref)
    acc_ref[...] = jnp.zeros_like(acc_ref)

    def k_tiles(start):
        if mode == "mla":
            return (k_ref[pl.ds(start, tq), :LANES], k_ref[pl.ds(start, tq), LANES:])
        kt = k_ref[pl.ds(start, tq), :]
        return (kt, kt)

    def update(idx, s, vt_tile):
        m_prev = m_ref[idx]
        m_new = jnp.maximum(m_prev, jnp.max(s, axis=0, keepdims=True))
        alpha = jnp.exp2(m_prev - m_new)
        p = jnp.exp2(s - m_new)
        l_ref[idx] = alpha * l_ref[idx] + jnp.sum(p, axis=0, keepdims=True)
        acc_ref[idx] = alpha * acc_ref[idx] + _dot(vt_tile, p.astype(BF16))
        m_ref[idx] = m_new

    def scores_into(slot, t):
        kts = k_tiles(pl.multiple_of(t * tq, tq))
        for idx in range(2):
            s_ref[slot, idx] = _dot(kts[idx], qts[idx])

    def consume(slot, t, masked):
        vt_tile = vt_ref[:, pl.ds(pl.multiple_of(t * tq, tq), tq)]
        for idx in range(2):
            s = s_ref[slot, idx]
            if masked:
                row = lax.broadcasted_iota(I32, (tq, tq), 0)
                col = lax.broadcasted_iota(I32, (tq, tq), 1)
                shift = CHUNK.bit_length() - 1
                s = jnp.where((row >> shift) <= (col >> shift), s, NEG)
            update(idx, s, vt_tile)

    scores_into(0, 0)

    def tile_pair(u, carry):
        t = 2 * u
        scores_into(1, t + 1)
        consume(0, t, False)
        scores_into(0, t + 2)
        consume(1, t + 1, False)
        return carry

    lax.fori_loop(0, i >> 1, tile_pair, 0)

    @pl.when((i & 1) == 1)
    def _():
        scores_into(1, i)
        consume(0, i - 1, False)
        consume(1, i, True)

    @pl.when((i & 1) == 0)
    def _():
        consume(0, i, True)

    o0 = acc_ref[0] * (1.0 / l_ref[0])
    o1 = acc_ref[1] * (1.0 / l_ref[1])
    if mode == "mla":
        o_ref[...] = jnp.where(top, o0, o1).T.astype(o_ref.dtype)
    else:
        o = (o0 - lam_ref[0] * o1).T
        o_ref[...] = (_rms(o, g_ref[...]) * out_scale).astype(o_ref.dtype)


def flash_attention(q_arr, k_arr, v_arr, *, mode, q_blk0, k_blk0, v_blk0, n_groups, tq,
                    lam=None, subln_g=None, out_scale=1.0):
    b, s, _ = q_arr.shape
    wqk = 2 * LANES if mode == "mla" else LANES
    if lam is None:
        lam = jnp.zeros((1,), F32)
    if subln_g is None:
        subln_g = jnp.ones((1, LANES), F32)
    grid = (b, n_groups, s // tq)
    in_specs = [
        pl.BlockSpec((None, tq, wqk), lambda bb, g, i, lam_r: (bb, i, q_blk0 + g)),
        pl.BlockSpec((None, s, wqk), lambda bb, g, i, lam_r: (bb, 0, k_blk0 + g)),
        pl.BlockSpec((None, s, LANES), lambda bb, g, i, lam_r: (bb, 0, v_blk0 + g)),
        pl.BlockSpec((1, LANES), lambda bb, g, i, lam_r: (0, 0)),
    ]
    grid_spec = pltpu.PrefetchScalarGridSpec(
        num_scalar_prefetch=1, grid=grid, in_specs=in_specs,
        out_specs=pl.BlockSpec((None, tq, LANES), lambda bb, g, i, lam_r: (bb, i, g)),
        scratch_shapes=[pltpu.VMEM((2, 1, tq), F32), pltpu.VMEM((2, 1, tq), F32),
                        pltpu.VMEM((2, LANES, tq), F32), pltpu.VMEM((LANES, s), BF16),
                        pltpu.VMEM((2, 2, tq, tq), F32)],
    )
    kern = functools.partial(_flash_kernel, mode=mode, tq=tq, out_scale=out_scale)
    return pl.pallas_call(
        kern, grid_spec=grid_spec,
        out_shape=jax.ShapeDtypeStruct((b, s, n_groups * LANES), BF16),
        compiler_params=_params("parallel", "parallel", "arbitrary"), name="flash_" + mode,
    )(lam, q_arr, k_arr, v_arr, subln_g)


BAND_TQ = 256
BAND_TILES = 3


def _band_kernel(q_ref, k0_ref, k1_ref, k2_ref, v0_ref, v1_ref, v2_ref, bias_ref, o_ref, *, scale):
    i = pl.program_id(2)
    lane = lax.broadcasted_iota(I32, (BAND_TQ, LANES), 1)
    low = lane < (LANES // 2)
    q = q_ref[...].astype(F32) * scale
    qs = (jnp.where(low, q, 0.0).astype(BF16), jnp.where(low, 0.0, q).astype(BF16))
    k_refs = (k0_ref, k1_ref, k2_ref)
    v_refs = (v0_ref, v1_ref, v2_ref)
    outs = []
    for h in range(2):
        qh = qs[h]
        scores = []
        for j in range(BAND_TILES):
            pen = jnp.where(i + j >= BAND_TILES - 1, 0.0, NEG).astype(F32)
            scores.append(_dot_nt(qh, k_refs[j][...]) + bias_ref[h, j] + pen)
        m = jnp.maximum(jnp.maximum(jnp.max(scores[0], axis=-1, keepdims=True),
                                    jnp.max(scores[1], axis=-1, keepdims=True)),
                        jnp.max(scores[2], axis=-1, keepdims=True))
        l = jnp.zeros_like(m)
        o = jnp.zeros((BAND_TQ, LANES), F32)
        for j in range(BAND_TILES):
            p = jnp.exp(scores[j] - m)
            l = l + jnp.sum(p, axis=-1, keepdims=True)
            o = o + _dot(p.astype(BF16), v_refs[j][...])
        outs.append(o * (1.0 / l))
    o_ref[...] = jnp.where(low, outs[0], outs[1]).astype(o_ref.dtype)


def band_bias_tiles(rel_bias):
    h = rel_bias.shape[0]
    n = BAND_TQ
    r = np.arange(n)[:, None]
    c = np.arange(n)[None, :]
    e_of = np.zeros(2 * n, np.int64)
    e_of[:n] = -np.arange(n)
    e_of[n + 1:] = n - 1 - np.arange(n - 1)
    tiles = []
    for j in range(BAND_TILES):
        rel = (BAND_TILES - 1 - j) * n + e_of
        gen = rel_bias[:, np.clip(rel, -A_MAX_REL, A_MAX_REL) + A_MAX_REL].astype(F32)
        skew = jnp.tile(gen, (1, n))[:, :n * (2 * n - 1)].reshape(h, n, 2 * n - 1)[:, :, :n]
        dist = (r // CHUNK) - (c // CHUNK) + (BAND_TILES - 1 - j) * (n // CHUNK)
        ok = (dist >= 0) & (dist <= A_LEFT_CHUNKS)
        tiles.append(jnp.where(ok[None], skew, NEG))
    t = jnp.stack(tiles, axis=1)
    return t.reshape(h // 2, 2, BAND_TILES, BAND_TQ, BAND_TQ)


def band_attention(proj, bias_tiles, *, q_blk0, k_blk0, v_blk0):
    b, s, _ = proj.shape
    n_pairs = bias_tiles.shape[0]
    grid = (n_pairs, b, s // BAND_TQ)

    def kv_spec(blk0, j):
        return pl.BlockSpec(
            (None, BAND_TQ, LANES),
            lambda p, bb, i: (bb, jnp.maximum(i + j - (BAND_TILES - 1), 0), blk0 + p))

    in_specs = [pl.BlockSpec((None, BAND_TQ, LANES), lambda p, bb, i: (bb, i, q_blk0 + p))]
    in_specs += [kv_spec(k_blk0, j) for j in range(BAND_TILES)]
    in_specs += [kv_spec(v_blk0, j) for j in range(BAND_TILES)]
    in_specs += [pl.BlockSpec((None, 2, BAND_TILES, BAND_TQ, BAND_TQ),
                              lambda p, bb, i: (p, 0, 0, 0, 0))]
    return pl.pallas_call(
        functools.partial(_band_kernel, scale=A_HEAD_DIM ** -0.5), grid=grid, in_specs=in_specs,
        out_specs=pl.BlockSpec((None, BAND_TQ, LANES), lambda p, bb, i: (bb, i, p)),
        out_shape=jax.ShapeDtypeStruct((b, s, n_pairs * LANES), BF16),
        compiler_params=_params("parallel", "parallel", "parallel"), name="band_attention",
    )(proj, proj, proj, proj, proj, proj, proj, bias_tiles)


def _mla_prep_kernel(cq_ref, ckv_ref, kr_ref, gq_ref, gkv_ref, wq_ref, wk_ref, wv_ref,
                     cq_cos_ref, cq_sin_ref, ck_cos_ref, ck_sin_ref, q_ref, k_ref, v_ref):
    half = B_ROPE_DIM // 2
    nq = _rms(cq_ref[...].astype(F32), gq_ref[...]).astype(BF16)
    nkv = _rms(ckv_ref[...].astype(F32), gkv_ref[...]).astype(BF16)
    q = _dot(nq, wq_ref[...])
    k = _dot(nkv, wk_ref[...])
    v_ref[...] = _dot(nkv, wv_ref[...]).astype(v_ref.dtype)
    kr = _rope_lanes(kr_ref[...].astype(F32), ck_cos_ref[...], ck_sin_ref[...], half)
    lane = lax.broadcasted_iota(I32, kr.shape, 1)
    kr = jnp.where(lane < B_ROPE_DIM, kr, 0.0)
    kr = pltpu.roll(kr, B_NOPE_DIM, 1)
    cos = cq_cos_ref[...]
    sin = cq_sin_ref[...]
    for c in range(q.shape[1] // LANES):
        sl = slice(c * LANES, (c + 1) * LANES)
        q_ref[:, sl] = _rope_lanes(q[:, sl], cos, sin, half).astype(q_ref.dtype)
        k_ref[:, sl] = (k[:, sl] + kr).astype(k_ref.dtype)


def mla_prep(proj, gq, gkv, wq, wk, wv, tables, *, tm, cq_blk, ckv_blk, kr_blk):
    t = proj.shape[0]
    nq = wq.shape[0]
    nkv = wk.shape[0]
    whole = lambda a: pl.BlockSpec(a.shape, lambda i: (0,) * a.ndim)
    tab = lambda k: pl.BlockSpec((None, tm, LANES), lambda i: (k, i, 0))
    in_specs = [
        pl.BlockSpec((tm, nq), lambda i: (i, cq_blk)),
        pl.BlockSpec((tm, nkv), lambda i: (i, ckv_blk)),
        pl.BlockSpec((tm, LANES), lambda i: (i, kr_blk)),
        whole(gq), whole(gkv), whole(wq), whole(wk), whole(wv),
        tab(0), tab(1), tab(2), tab(3),
    ]
    out_specs = [pl.BlockSpec((tm, wq.shape[1]), lambda i: (i, 0)),
                 pl.BlockSpec((tm, wk.shape[1]), lambda i: (i, 0)),
                 pl.BlockSpec((tm, wv.shape[1]), lambda i: (i, 0))]
    out_shape = [jax.ShapeDtypeStruct((t, wq.shape[1]), BF16),
                 jax.ShapeDtypeStruct((t, wk.shape[1]), BF16),
                 jax.ShapeDtypeStruct((t, wv.shape[1]), BF16)]
    return pl.pallas_call(
        _mla_prep_kernel, grid=(t // tm,), in_specs=in_specs, out_specs=out_specs,
        out_shape=out_shape, compiler_params=_params("parallel"), name="mla_prep",
    )(proj, proj, proj, gq, gkv, wq, wk, wv, tables, tables, tables, tables)


def _router_kernel(h_ref, g_ref, wr_ref, gate_ref, idx_ref, cnt_ref, carry_ref):
    tb = h_ref.shape[0]

    @pl.when(pl.program_id(0) == 0)
    def _():
        carry_ref[...] = jnp.zeros_like(carry_ref)

    n = _rms(h_ref[...], g_ref[...])
    logits = jnp.dot(n, wr_ref[...], preferred_element_type=F32, precision=lax.Precision.HIGHEST)
    lane = lax.broadcasted_iota(I32, (tb, LANES), 1).astype(F32)
    neg_inf = jnp.float32(-jnp.inf)
    lg = jnp.where(lane < N_EXPERTS, logits, neg_inf)
    m1 = jnp.max(lg, axis=-1, keepdims=True)
    e0 = jnp.min(jnp.where(lg == m1, lane, float(LANES)), axis=-1, keepdims=True)
    lg2 = jnp.where(lane == e0, neg_inf, lg)
    m2 = jnp.max(lg2, axis=-1, keepdims=True)
    e1 = jnp.min(jnp.where(lg2 == m2, lane, float(LANES)), axis=-1, keepdims=True)
    t = jnp.exp(m2 - m1)
    g0 = 1.0 / (1.0 + t)
    g1 = t / (1.0 + t)

    sel = ((lane == e0) | (lane == e1)).astype(F32)
    row = lax.broadcasted_iota(I32, (tb, tb), 0)
    col = lax.broadcasted_iota(I32, (tb, tb), 1)
    strict_lower = (col < row).astype(BF16)
    carry = carry_ref[...]
    excl = _dot(strict_lower, sel.astype(BF16)) + carry
    r0 = jnp.sum(jnp.where(lane == e0, excl, 0.0), axis=-1, keepdims=True)
    r1 = jnp.sum(jnp.where(lane == e1, excl, 0.0), axis=-1, keepdims=True)
    carry = carry + jnp.sum(sel, axis=0, keepdims=True)
    carry_ref[...] = carry

    gate_ref[...] = jnp.where(lane == 0, g0, jnp.where(lane == 1, g1, 0.0))
    idx_ref[...] = jnp.where(
        lane == 0, e0, jnp.where(lane == 1, e1, jnp.where(
            lane == 2, r0, jnp.where(lane == 3, r1, 0.0)))).astype(I32)
    cnt_ref[...] = jnp.broadcast_to(carry, cnt_ref.shape).astype(I32)


def router(h, g, wr_pad, *, tb):
    t, d = h.shape
    return pl.pallas_call(
        _router_kernel, grid=(t // tb,),
        in_specs=[pl.BlockSpec((tb, d), lambda i: (i, 0)),
                  pl.BlockSpec((1, d), lambda i: (0, 0)),
                  pl.BlockSpec((d, LANES), lambda i: (0, 0))],
        out_specs=[pl.BlockSpec((tb, LANES), lambda i: (i, 0)),
                   pl.BlockSpec((tb, LANES), lambda i: (i, 0)),
                   pl.BlockSpec((8, LANES), lambda i: (0, 0))],
        out_shape=[jax.ShapeDtypeStruct((t, LANES), F32),
                   jax.ShapeDtypeStruct((t, LANES), I32),
                   jax.ShapeDtypeStruct((8, LANES), I32)],
        scratch_shapes=[pltpu.VMEM((1, LANES), F32)],
        compiler_params=_params("arbitrary"), name="router",
    )(h, g, wr_pad)


def _row_copy(src_hbm, row, dst_ref, r, sem):
    return pltpu.make_async_copy(src_hbm.at[pl.ds(row, 1), :], dst_ref.at[pl.ds(r, 1), :], sem)


ROW_UNROLL = 8


def _for_rows(rows, body):
    def outer(o, carry):
        for u in range(ROW_UNROLL):
            body(o * ROW_UNROLL + u, u)
        return carry

    lax.fori_loop(0, rows // ROW_UNROLL, outer, 0)


def _gather_kernel(idx_ref, src_hbm, o_hbm, sem):
    i = pl.program_id(0)
    rows = idx_ref.shape[-1]
    base = i * rows
    slot = i & 1

    def start(r, u):
        _row_copy(src_hbm, idx_ref[0, 0, r], o_hbm, base + r, sem.at[slot]).start(priority=u % 2)

    def wait_on(s):
        _for_rows(rows, lambda r, u: _row_copy(src_hbm, 0, o_hbm, r, sem.at[s]).wait())

    _for_rows(rows, start)

    @pl.when(i > 0)
    def _():
        wait_on(1 - slot)

    @pl.when(i == pl.num_programs(0) - 1)
    def _():
        wait_on(slot)


def gather_rows(src, idx, *, rows):
    p = idx.shape[0]
    d = src.shape[1]
    idx3 = idx.reshape(p // rows, 1, rows)
    return pl.pallas_call(
        _gather_kernel, grid=(p // rows,),
        in_specs=[pl.BlockSpec((1, 1, rows), lambda i: (i, 0, 0), memory_space=pltpu.SMEM),
                  pl.BlockSpec(memory_space=pl.ANY)],
        out_specs=pl.BlockSpec(memory_space=pl.ANY),
        out_shape=jax.ShapeDtypeStruct((p, d), src.dtype),
        scratch_shapes=[pltpu.SemaphoreType.DMA((2,))],
        compiler_params=_params("arbitrary"), name="gather_rows",
    )(idx3, src)


def _combine_kernel(p0_first, p1_first, p0_next, p1_next, h_ref, gate_ref, ys_hbm, o_ref,
                    buf_ref, sem):
    i = pl.program_id(0)
    n = pl.num_programs(0)
    rows = h_ref.shape[0]
    slot = i & 1

    def start_into(s, p0_ref, p1_ref):
        def start(r, u):
            _row_copy(ys_hbm, p0_ref[0, 0, r], buf_ref.at[s, 0], r, sem.at[s]).start(priority=0)
            _row_copy(ys_hbm, p1_ref[0, 0, r], buf_ref.at[s, 1], r, sem.at[s]).start(priority=1)
        _for_rows(rows, start)

    @pl.when(i == 0)
    def _():
        start_into(0, p0_first, p1_first)

    @pl.when(i + 1 < n)
    def _():
        start_into(1 - slot, p0_next, p1_next)

    def wait(r, u):
        _row_copy(ys_hbm, 0, buf_ref.at[slot, 0], r, sem.at[slot]).wait()
        _row_copy(ys_hbm, 0, buf_ref.at[slot, 1], r, sem.at[slot]).wait()

    _for_rows(rows, wait)
    gates = gate_ref[...]
    o_ref[...] = (h_ref[...] + gates[:, 0:1] * buf_ref[slot, 0]
                  + gates[:, 1:2] * buf_ref[slot, 1])


def combine_rows(h, gates, ys, pos0, pos1, *, rows):
    t, d = h.shape
    n = t // rows
    first = lambda: pl.BlockSpec((1, 1, rows), lambda i: (0, 0, 0), memory_space=pltpu.SMEM)
    nxt = lambda: pl.BlockSpec((1, 1, rows), lambda i: (jnp.minimum(i + 1, n - 1), 0, 0),
                               memory_space=pltpu.SMEM)
    pos0 = pos0.reshape(n, 1, rows)
    pos1 = pos1.reshape(n, 1, rows)
    return pl.pallas_call(
        _combine_kernel, grid=(n,),
        in_specs=[first(), first(), nxt(), nxt(),
                  pl.BlockSpec((rows, d), lambda i: (i, 0)),
                  pl.BlockSpec((rows, LANES), lambda i: (i, 0)),
                  pl.BlockSpec(memory_space=pl.ANY)],
        out_specs=pl.BlockSpec((rows, d), lambda i: (i, 0)),
        out_shape=jax.ShapeDtypeStruct((t, d), F32),
        scratch_shapes=[pltpu.VMEM((2, 2, rows, d), F32), pltpu.SemaphoreType.DMA((2,))],
        compiler_params=_params("arbitrary"), name="combine_rows",
    )(pos0, pos1, pos0, pos1, h, gates, ys)


def _rmsnorm_kernel(x_ref, g_ref, o_ref):
    o_ref[...] = _rms(x_ref[...], g_ref[...])


def rmsnorm(x, g, *, tm):
    t, d = x.shape
    return pl.pallas_call(
        _rmsnorm_kernel, grid=(t // tm,),
        in_specs=[pl.BlockSpec((tm, d), lambda i: (i, 0)), pl.BlockSpec((1, d), lambda i: (0, 0))],
        out_specs=pl.BlockSpec((tm, d), lambda i: (i, 0)),
        out_shape=jax.ShapeDtypeStruct((t, d), F32),
        compiler_params=_params("parallel"), name="final_rmsnorm",
    )(x, g)


def _rope_cos_sin(positions, dim):
    inv_freq = 1.0 / jnp.power(ROPE_THETA, jnp.arange(0, dim, 2, dtype=F32) / dim)
    ang = positions.reshape(-1).astype(F32)[:, None] * inv_freq
    return jnp.cos(ang), jnp.sin(ang)


def _diff_tables(positions):
    cos, sin = _rope_cos_sin(positions, C_HEAD_DIM)
    cos_l = jnp.tile(cos, (1, 4))
    sin_l = jnp.concatenate([-sin, -sin, sin, sin], axis=1)
    k_tab = jnp.stack([cos_l, sin_l])
    return jnp.stack([k_tab * (C_HEAD_DIM ** -0.5 * LOG2E), k_tab])


def _diff_head_order(w, n_cols):
    d = w.shape[0]
    half = C_HEAD_DIM // 2
    head = w[:, :n_cols].reshape(d, n_cols // LANES, 2, 2, half)
    head = head.transpose(0, 1, 3, 2, 4).reshape(d, n_cols)
    return jnp.concatenate([head, w[:, n_cols:]], axis=1)


def _mla_tables(positions):
    cos, sin = _rope_cos_sin(positions, B_ROPE_DIM)
    t = cos.shape[0]
    scale = (B_NOPE_DIM + B_ROPE_DIM) ** -0.5 * LOG2E
    ones = lambda n: jnp.ones((t, n), F32)
    zeros = lambda n: jnp.zeros((t, n), F32)
    q_cos = jnp.concatenate([ones(B_NOPE_DIM), cos, cos, ones(32)], axis=1) * scale
    q_sin = jnp.concatenate([zeros(B_NOPE_DIM), -sin, sin, zeros(32)], axis=1) * scale
    k_cos = jnp.concatenate([cos, cos, ones(96)], axis=1)
    k_sin = jnp.concatenate([-sin, sin, zeros(96)], axis=1)
    return jnp.stack([q_cos, q_sin, k_cos, k_sin])


def _routing_plan(idx_out, counts, *, tm, n_tiles):
    t = idx_out.shape[0]
    e0, e1, r0, r1 = idx_out[:, 0], idx_out[:, 1], idx_out[:, 2], idx_out[:, 3]
    padded = ((counts + tm - 1) // tm) * tm
    ends = jnp.cumsum(padded)
    offs = ends - padded
    experts = jnp.arange(N_EXPERTS, dtype=I32)[None, :]
    pos0 = jnp.sum(jnp.where(e0[:, None] == experts, offs[None, :], 0), axis=1) + r0
    pos1 = jnp.sum(jnp.where(e1[:, None] == experts, offs[None, :], 0), axis=1) + r1
    tok = jnp.arange(t, dtype=I32)
    src = jnp.zeros((n_tiles * tm,), I32).at[pos0].set(tok).at[pos1].set(tok)
    n_valid = (ends[-1] // tm).astype(I32)
    tile = jnp.minimum(jnp.arange(n_tiles, dtype=I32), n_valid - 1)
    tile_expert = jnp.minimum(jnp.sum((tile * tm)[:, None] >= ends[None, :], axis=1),
                              N_EXPERTS - 1)
    return pos0.astype(I32), pos1.astype(I32), src, tile_expert.astype(I32), n_valid.reshape(1)


def _pad_cols(w, n):
    return jnp.pad(w, ((0, 0), (0, n - w.shape[1])))


def kernel(x, positions, mix_norm_g, ffn_norm_g, w_in_even, rel_bias_a, q_norm_b, w_uq_b, kv_norm_b, w_ukv_b, w_out_even, w_in_odd, lambda_q1, lambda_k1, lambda_q2, lambda_k2, subln_g, w_out_odd, w_gu_dense, w_down_dense, w_router, w_gu_moe, w_down_moe, final_norm_g):
    b, s, d = x.shape
    t = b * s
    depth = mix_norm_g.shape[0]
    a_width = 4 * LANES
    tm = 512
    tm_moe = 512
    n_tiles_moe = (2 * t) // tm_moe + N_EXPERTS

    diff_tab = _diff_tables(positions)
    mla_tab = _mla_tables(positions)
    row = lambda v: v.reshape(1, -1).astype(F32)

    h = x.reshape(t, d).astype(F32)
    for layer in range(depth):
        i = layer // 2
        if layer % 2 == 0:
            w_in = _pad_cols(w_in_even[i], 2048).astype(BF16)
            proj = rms_matmul(h, row(mix_norm_g[layer]), w_in, tm=tm, tn=1024)
            proj3 = proj.reshape(b, s, -1)
            out_a = band_attention(proj3, band_bias_tiles(rel_bias_a[i]),
                                   q_blk0=0, k_blk0=4, v_blk0=8)
            n_heads = w_uq_b.shape[2] // (B_NOPE_DIM + B_ROPE_DIM)
            wq = jnp.pad(w_uq_b[i].reshape(-1, n_heads, B_NOPE_DIM + B_ROPE_DIM),
                         ((0, 0), (0, 0), (0, 32))).reshape(-1, n_heads * LANES).astype(BF16)
            wkv = w_ukv_b[i].reshape(-1, n_heads, B_NOPE_DIM + B_V_DIM)
            wk = jnp.pad(wkv[:, :, :B_NOPE_DIM], ((0, 0), (0, 0), (0, LANES - B_NOPE_DIM)))
            wk = wk.reshape(-1, n_heads * LANES).astype(BF16)
            wv = wkv[:, :, B_NOPE_DIM:].reshape(-1, n_heads * B_V_DIM).astype(BF16)
            q_b, k_b, v_b = mla_prep(proj, row(q_norm_b[i]), row(kv_norm_b[i]), wq, wk, wv, mla_tab,
                                     tm=tm, cq_blk=6, ckv_blk=14, kr_blk=15)
            out_b = flash_attention(q_b.reshape(b, s, -1), k_b.reshape(b, s, -1),
                                    v_b.reshape(b, s, -1), mode="mla", q_blk0=0, k_blk0=0,
                                    v_blk0=0, n_groups=n_heads // 2, tq=512)
            w_out = w_out_even[i].astype(BF16)
            h = matmul_residual([out_a.reshape(t, -1), out_b.reshape(t, -1)],
                                [w_out[:a_width], w_out[a_width:]], h, tm=tm)
            n_tiles = t // tm
            h = grouped_swiglu(h, row(ffn_norm_g[layer]), w_gu_dense[i][None].astype(BF16),
                               w_down_dense[i][None].astype(BF16), jnp.zeros((n_tiles,), I32),
                               jnp.full((1,), n_tiles, I32), tm=tm, tf=1408, add_res=True)
        else:
            c_width = w_out_odd.shape[1]
            n_heads = c_width // LANES
            w_in = _diff_head_order(w_in_odd[i], 2 * c_width).astype(BF16)
            qkv = rms_matmul(h, row(mix_norm_g[layer]), w_in, tm=tm, tn=512,
                             rope=(diff_tab, c_width // 512, LANES // 2))
            lam_init = 0.8 - 0.6 * math.exp(-0.3 * layer)
            lam = (jnp.exp(jnp.sum(lambda_q1[i].astype(F32) * lambda_k1[i].astype(F32)))
                   - jnp.exp(jnp.sum(lambda_q2[i].astype(F32) * lambda_k2[i].astype(F32)))
                   + lam_init).reshape(1).astype(F32)
            qkv3 = qkv.reshape(b, s, -1)
            out_c = flash_attention(qkv3, qkv3, qkv3, mode="diff", q_blk0=0, k_blk0=n_heads,
                                    v_blk0=2 * n_heads, n_groups=n_heads, tq=512, lam=lam,
                                    subln_g=row(subln_g[i]), out_scale=1.0 - lam_init)
            h = matmul_residual([out_c.reshape(t, -1)], [w_out_odd[i].astype(BF16)], h, tm=tm)

            g_ffn = row(ffn_norm_g[layer])
            wr_pad = _pad_cols(w_router[i].astype(F32), LANES)
            gates, idx_out, cnt = router(h, g_ffn, wr_pad, tb=512)
            pos0, pos1, src, tile_expert, n_valid = _routing_plan(
                idx_out, cnt[0, :N_EXPERTS], tm=tm_moe, n_tiles=n_tiles_moe)
            xs = gather_rows(h, src, rows=256)
            ys = grouped_swiglu(xs, g_ffn, w_gu_moe[i].astype(BF16), w_down_moe[i].astype(BF16),
                                tile_expert, n_valid, tm=tm_moe, tf=1792, add_res=False)
            h = combine_rows(h, gates, ys, pos0, pos1, rows=256)

    out = rmsnorm(h, row(final_norm_g), tm=tm)
    return out.reshape(b, s, d)
```

```python
import functools
import math

import jax
import jax.numpy as jnp
import numpy as np
from jax import lax
from jax.experimental import pallas as pl
from jax.experimental.pallas import tpu as pltpu

F32 = jnp.float32
BF16 = jnp.bfloat16
I32 = jnp.int32

NORM_EPS = 1e-6
ROPE_THETA = 10000.0
NEG = -1e30
LOG2E = math.log2(math.e)
LANES = 128
CHUNK = 64

A_LEFT_CHUNKS = 8
A_MAX_REL = 256
A_HEAD_DIM = 64
B_NOPE_DIM = 64
B_ROPE_DIM = 32
B_V_DIM = 64
C_HEAD_DIM = 64
N_EXPERTS = 8

VMEM_LIMIT = 56 * 1024 * 1024


def _params(*sem):
    return pltpu.CompilerParams(dimension_semantics=sem, vmem_limit_bytes=VMEM_LIMIT)


def _rms(x, g):
    ms = jnp.mean(x * x, axis=-1, keepdims=True)
    return x * lax.rsqrt(ms + NORM_EPS) * g


def _dot(a, b):
    return jnp.dot(a, b, preferred_element_type=F32)


def _dot_nt(a, b):
    return lax.dot_general(a, b, (((1,), (1,)), ((), ())), preferred_element_type=F32)


def _rope_lanes(x, cos, sin, half):
    if 2 * half == LANES:
        swapped = pltpu.roll(x, half, 1)
    else:
        lane = lax.broadcasted_iota(I32, x.shape, 1)
        fwd = pltpu.roll(x, LANES - half, 1)
        bwd = pltpu.roll(x, half, 1)
        swapped = jnp.where((lane & half) == 0, fwd, bwd)
    return x * cos + swapped * sin


def _rms_matmul_kernel(x_ref, g_ref, w_ref, o_ref, xn_ref):
    @pl.when(pl.program_id(1) == 0)
    def _():
        xn_ref[...] = _rms(x_ref[...], g_ref[...]).astype(BF16)

    o_ref[...] = _dot(xn_ref[...], w_ref[...]).astype(o_ref.dtype)


def _rms_matmul_rope_kernel(x_ref, g_ref, w_ref, cos_ref, sin_ref, o_ref, xn_ref, *,
                            n_rope_tiles, half):
    j = pl.program_id(1)

    @pl.when(j == 0)
    def _():
        xn_ref[...] = _rms(x_ref[...], g_ref[...]).astype(BF16)

    acc = _dot(xn_ref[...], w_ref[...])

    @pl.when(j < n_rope_tiles)
    def _():
        cos = cos_ref[...]
        sin = sin_ref[...]
        for c in range(acc.shape[1] // LANES):
            sl = slice(c * LANES, (c + 1) * LANES)
            o_ref[:, sl] = _rope_lanes(acc[:, sl], cos, sin, half).astype(o_ref.dtype)

    @pl.when(j >= n_rope_tiles)
    def _():
        o_ref[...] = acc.astype(o_ref.dtype)


def rms_matmul(x, g, w, *, tm, tn, rope=None):
    t, d = x.shape
    n = w.shape[1]
    grid = (t // tm, n // tn)
    x_spec = pl.BlockSpec((tm, d), lambda i, j: (i, 0))
    g_spec = pl.BlockSpec((1, d), lambda i, j: (0, 0))
    w_spec = pl.BlockSpec((d, tn), lambda i, j: (0, j))
    o_spec = pl.BlockSpec((tm, tn), lambda i, j: (i, j))
    scratch = [pltpu.VMEM((tm, d), BF16)]
    out_shape = jax.ShapeDtypeStruct((t, n), BF16)
    if rope is None:
        return pl.pallas_call(
            _rms_matmul_kernel, grid=grid, in_specs=[x_spec, g_spec, w_spec], out_specs=o_spec,
            out_shape=out_shape, scratch_shapes=scratch,
            compiler_params=_params("parallel", "arbitrary"), name="rms_matmul",
        )(x, g, w)
    tables, per_part, half = rope
    cos_spec = pl.BlockSpec((None, None, tm, LANES),
                            lambda i, j: (jnp.minimum(j // per_part, 1), 0, i, 0))
    sin_spec = pl.BlockSpec((None, None, tm, LANES),
                            lambda i, j: (jnp.minimum(j // per_part, 1), 1, i, 0))
    kern = functools.partial(_rms_matmul_rope_kernel, n_rope_tiles=2 * per_part, half=half)
    return pl.pallas_call(
        kern, grid=grid, in_specs=[x_spec, g_spec, w_spec, cos_spec, sin_spec], out_specs=o_spec,
        out_shape=out_shape, scratch_shapes=scratch,
        compiler_params=_params("parallel", "arbitrary"), name="rms_matmul_rope",
    )(x, g, w, tables, tables)


def _mm_res_kernel(*refs, n_in):
    res_ref = refs[2 * n_in]
    o_ref = refs[2 * n_in + 1]
    acc = res_ref[...]
    for k in range(n_in):
        acc = acc + _dot(refs[k][...], refs[n_in + k][...])
    o_ref[...] = acc


def matmul_residual(a_list, w_list, res, *, tm):
    t, n = res.shape
    n_in = len(a_list)
    in_specs = [pl.BlockSpec((tm, a.shape[1]), lambda i: (i, 0)) for a in a_list]
    in_specs += [pl.BlockSpec(w.shape, lambda i: (0, 0)) for w in w_list]
    in_specs += [pl.BlockSpec((tm, n), lambda i: (i, 0))]
    return pl.pallas_call(
        functools.partial(_mm_res_kernel, n_in=n_in), grid=(t // tm,), in_specs=in_specs,
        out_specs=pl.BlockSpec((tm, n), lambda i: (i, 0)),
        out_shape=jax.ShapeDtypeStruct((t, n), F32),
        compiler_params=_params("parallel"), name="matmul_residual",
    )(*a_list, *w_list, res)


def _ffn_kernel(te_ref, nv_ref, x_ref, g_ref, wg_ref, wu_ref, wd_ref, o_ref, xn_ref, acc_ref, *,
                add_res):
    del te_ref
    i = pl.program_id(0)
    f = pl.program_id(1)
    nf = pl.num_programs(1)
    valid = i < nv_ref[0]

    @pl.when(valid & (f == 0))
    def _():
        xn_ref[...] = _rms(x_ref[...], g_ref[...]).astype(BF16)
        acc_ref[...] = jnp.zeros_like(acc_ref)

    @pl.when(valid)
    def _():
        xn = xn_ref[...]
        gate = _dot(xn, wg_ref[...])
        up = _dot(xn, wu_ref[...])
        act = (gate * jax.nn.sigmoid(gate) * up).astype(BF16)
        acc_ref[...] += _dot(act, wd_ref[...])

    @pl.when(valid & (f == nf - 1))
    def _():
        if add_res:
            o_ref[...] = x_ref[...] + acc_ref[...]
        else:
            o_ref[...] = acc_ref[...]

    @pl.when(jnp.logical_not(valid) & (f == nf - 1))
    def _():
        o_ref[...] = jnp.zeros_like(o_ref)


def grouped_swiglu(x, g, w_gu, w_down, tile_expert, n_valid, *, tm, tf, add_res):
    p, d = x.shape
    ff = w_down.shape[1]
    nf = ff // tf
    grid = (p // tm, nf)

    def f_eff(i, f, nv):
        return jnp.where(i < nv[0], f, nf - 1)

    in_specs = [
        pl.BlockSpec((tm, d), lambda i, f, te, nv: (i, 0)),
        pl.BlockSpec((1, d), lambda i, f, te, nv: (0, 0)),
        pl.BlockSpec((None, d, tf), lambda i, f, te, nv: (te[i], 0, f_eff(i, f, nv))),
        pl.BlockSpec((None, d, tf), lambda i, f, te, nv: (te[i], 0, nf + f_eff(i, f, nv))),
        pl.BlockSpec((None, tf, d), lambda i, f, te, nv: (te[i], f_eff(i, f, nv), 0)),
    ]
    grid_spec = pltpu.PrefetchScalarGridSpec(
        num_scalar_prefetch=2, grid=grid, in_specs=in_specs,
        out_specs=pl.BlockSpec((tm, d), lambda i, f, te, nv: (i, 0)),
        scratch_shapes=[pltpu.VMEM((tm, d), BF16), pltpu.VMEM((tm, d), F32)],
    )
    return pl.pallas_call(
        functools.partial(_ffn_kernel, add_res=add_res), grid_spec=grid_spec,
        out_shape=jax.ShapeDtypeStruct((p, d), F32),
        compiler_params=_params("parallel", "arbitrary"), name="grouped_swiglu",
    )(tile_expert, n_valid, x, g, w_gu, w_gu, w_down)


def _flash_kernel(lam_ref, q_ref, k_ref, v_ref, g_ref, o_ref, m_ref, l_ref, acc_ref, vt_ref,
                  s_ref, *,
                  mode, tq, out_scale):
    i = pl.program_id(2)
    s_len = v_ref.shape[0]
    sub = lax.broadcasted_iota(I32, (LANES, tq), 0)
    top = sub < (LANES // 2)

    @pl.when(i == 0)
    def _():
        for c in range(s_len // tq):
            sl = slice(c * tq, (c + 1) * tq)
            vt_ref[:, sl] = v_ref[sl, :].T

    if mode == "mla":
        qts = (q_ref[:, :LANES].T, q_ref[:, LANES:].T)
    else:
        qt = q_ref[...].T
        zero = jnp.zeros_like(qt)
        map1 = (sub & (C_HEAD_DIM // 2)) == 0
        qts = (jnp.where(map1, qt, zero), jnp.where(map1, zero, qt))

    m_ref[...] = jnp.full_like(m_ref, NEG)
    l_ref[...] = jnp.zeros_like(l_ref)
    acc_ref[...] = jnp.zeros_like(acc_ref)

    def k_tiles(start):
        if mode == "mla":
            return (k_ref[pl.ds(start, tq), :LANES], k_ref[pl.ds(start, tq), LANES:])
        kt = k_ref[pl.ds(start, tq), :]
        return (kt, kt)

    def update(idx, s, vt_tile):
        m_prev = m_ref[idx]
        m_new = jnp.maximum(m_prev, jnp.max(s, axis=0, keepdims=True))
        alpha = jnp.exp2(m_prev - m_new)
        p = jnp.exp2(s - m_new)
        l_ref[idx] = alpha * l_ref[idx] + jnp.sum(p, axis=0, keepdims=True)
        acc_ref[idx] = alpha * acc_ref[idx] + _dot(vt_tile, p.astype(BF16))
        m_ref[idx] = m_new

    def scores_into(slot, t):
        kts = k_tiles(pl.multiple_of(t * tq, tq))
        for idx in range(2):
            s_ref[slot, idx] = _dot(kts[idx], qts[idx])

    def consume(slot, t, masked):
        vt_tile = vt_ref[:, pl.ds(pl.multiple_of(t * tq, tq), tq)]
        for idx in range(2):
            s = s_ref[slot, idx]
            if masked:
                row = lax.broadcasted_iota(I32, (tq, tq), 0)
                col = lax.broadcasted_iota(I32, (tq, tq), 1)
                shift = CHUNK.bit_length() - 1
                s = jnp.where((row >> shift) <= (col >> shift), s, NEG)
            update(idx, s, vt_tile)

    scores_into(0, 0)

    def tile_pair(u, carry):
        t = 2 * u
        scores_into(1, t + 1)
        consume(0, t, False)
        scores_into(0, t + 2)
        consume(1, t + 1, False)
        return carry

    lax.fori_loop(0, i >> 1, tile_pair, 0)

    @pl.when((i & 1) == 1)
    def _():
        scores_into(1, i)
        consume(0, i - 1, False)
        consume(1, i, True)

    @pl.when((i & 1) == 0)
    def _():
        consume(0, i, True)

    o0 = acc_ref[0] * (1.0 / l_ref[0])
    o1 = acc_ref[1] * (1.0 / l_ref[1])
    if mode == "mla":
        o_ref[...] = jnp.where(top, o0, o1).T.astype(o_ref.dtype)
    else:
        o = (o0 - lam_ref[0] * o1).T
        o_ref[...] = (_rms(o, g_ref[...]) * out_scale).astype(o_ref.dtype)


def flash_attention(q_arr, k_arr, v_arr, *, mode, q_blk0, k_blk0, v_blk0, n_groups, tq,
                    lam=None, subln_g=None, out_scale=1.0):
    b, s, _ = q_arr.shape
    wqk = 2 * LANES if mode == "mla" else LANES
    if lam is None:
        lam = jnp.zeros((1,), F32)
    if subln_g is None:
        subln_g = jnp.ones((1, LANES), F32)
    grid = (b, n_groups, s // tq)
    in_specs = [
        pl.BlockSpec((None, tq, wqk), lambda bb, g, i, lam_r: (bb, i, q_blk0 + g)),
        pl.BlockSpec((None, s, wqk), lambda bb, g, i, lam_r: (bb, 0, k_blk0 + g)),
        pl.BlockSpec((None, s, LANES), lambda bb, g, i, lam_r: (bb, 0, v_blk0 + g)),
        pl.BlockSpec((1, LANES), lambda bb, g, i, lam_r: (0, 0)),
    ]
    grid_spec = pltpu.PrefetchScalarGridSpec(
        num_scalar_prefetch=1, grid=grid, in_specs=in_specs,
        out_specs=pl.BlockSpec((None, tq, LANES), lambda bb, g, i, lam_r: (bb, i, g)),
        scratch_shapes=[pltpu.VMEM((2, 1, tq), F32), pltpu.VMEM((2, 1, tq), F32),
                        pltpu.VMEM((2, LANES, tq), F32), pltpu.VMEM((LANES, s), BF16),
                        pltpu.VMEM((2, 2, tq, tq), F32)],
    )
    kern = functools.partial(_flash_kernel, mode=mode, tq=tq, out_scale=out_scale)
    return pl.pallas_call(
        kern, grid_spec=grid_spec,
        out_shape=jax.ShapeDtypeStruct((b, s, n_groups * LANES), BF16),
        compiler_params=_params("parallel", "parallel", "arbitrary"), name="flash_" + mode,
    )(lam, q_arr, k_arr, v_arr, subln_g)


BAND_TQ = 256
BAND_TILES = 3


def _band_kernel(q_ref, k0_ref, k1_ref, k2_ref, v0_ref, v1_ref, v2_ref, bias_ref, o_ref, *, scale):
    i = pl.program_id(2)
    lane = lax.broadcasted_iota(I32, (BAND_TQ, LANES), 1)
    low = lane < (LANES // 2)
    q = q_ref[...].astype(F32) * scale
    qs = (jnp.where(low, q, 0.0).astype(BF16), jnp.where(low, 0.0, q).astype(BF16))
    k_refs = (k0_ref, k1_ref, k2_ref)
    v_refs = (v0_ref, v1_ref, v2_ref)
    outs = []
    for h in range(2):
        qh = qs[h]
        scores = []
        for j in range(BAND_TILES):
            pen = jnp.where(i + j >= BAND_TILES - 1, 0.0, NEG).astype(F32)
            scores.append(_dot_nt(qh, k_refs[j][...]) + bias_ref[h, j] + pen)
        m = jnp.maximum(jnp.maximum(jnp.max(scores[0], axis=-1, keepdims=True),
                                    jnp.max(scores[1], axis=-1, keepdims=True)),
                        jnp.max(scores[2], axis=-1, keepdims=True))
        l = jnp.zeros_like(m)
        o = jnp.zeros((BAND_TQ, LANES), F32)
        for j in range(BAND_TILES):
            p = jnp.exp(scores[j] - m)
            l = l + jnp.sum(p, axis=-1, keepdims=True)
            o = o + _dot(p.astype(BF16), v_refs[j][...])
        outs.append(o * (1.0 / l))
    o_ref[...] = jnp.where(low, outs[0], outs[1]).astype(o_ref.dtype)


def band_bias_tiles(rel_bias):
    h = rel_bias.shape[0]
    n = BAND_TQ
    r = np.arange(n)[:, None]
    c = np.arange(n)[None, :]
    e_of = np.zeros(2 * n, np.int64)
    e_of[:n] = -np.arange(n)
    e_of[n + 1:] = n - 1 - np.arange(n - 1)
    tiles = []
    for j in range(BAND_TILES):
        rel = (BAND_TILES - 1 - j) * n + e_of
        gen = rel_bias[:, np.clip(rel, -A_MAX_REL, A_MAX_REL) + A_MAX_REL].astype(F32)
        skew = jnp.tile(gen, (1, n))[:, :n * (2 * n - 1)].reshape(h, n, 2 * n - 1)[:, :, :n]
        dist = (r // CHUNK) - (c // CHUNK) + (BAND_TILES - 1 - j) * (n // CHUNK)
        ok = (dist >= 0) & (dist <= A_LEFT_CHUNKS)
        tiles.append(jnp.where(ok[None], skew, NEG))
    t = jnp.stack(tiles, axis=1)
    return t.reshape(h // 2, 2, BAND_TILES, BAND_TQ, BAND_TQ)


def band_attention(proj, bias_tiles, *, q_blk0, k_blk0, v_blk0):
    b, s, _ = proj.shape
    n_pairs = bias_tiles.shape[0]
    grid = (n_pairs, b, s // BAND_TQ)

    def kv_spec(blk0, j):
        return pl.BlockSpec(
            (None, BAND_TQ, LANES),
            lambda p, bb, i: (bb, jnp.maximum(i + j - (BAND_TILES - 1), 0), blk0 + p))

    in_specs = [pl.BlockSpec((None, BAND_TQ, LANES), lambda p, bb, i: (bb, i, q_blk0 + p))]
    in_specs += [kv_spec(k_blk0, j) for j in range(BAND_TILES)]
    in_specs += [kv_spec(v_blk0, j) for j in range(BAND_TILES)]
    in_specs += [pl.BlockSpec((None, 2, BAND_TILES, BAND_TQ, BAND_TQ),
                              lambda p, bb, i: (p, 0, 0, 0, 0))]
    return pl.pallas_call(
        functools.partial(_band_kernel, scale=A_HEAD_DIM ** -0.5), grid=grid, in_specs=in_specs,
        out_specs=pl.BlockSpec((None, BAND_TQ, LANES), lambda p, bb, i: (bb, i, p)),
        out_shape=jax.ShapeDtypeStruct((b, s, n_pairs * LANES), BF16),
        compiler_params=_params("parallel", "parallel", "parallel"), name="band_attention",
    )(proj, proj, proj, proj, proj, proj, proj, bias_tiles)


def _mla_prep_kernel(cq_ref, ckv_ref, kr_ref, gq_ref, gkv_ref, wq_ref, wk_ref, wv_ref,
                     cq_cos_ref, cq_sin_ref, ck_cos_ref, ck_sin_ref, q_ref, k_ref, v_ref):
    half = B_ROPE_DIM // 2
    nq = _rms(cq_ref[...].astype(F32), gq_ref[...]).astype(BF16)
    nkv = _rms(ckv_ref[...].astype(F32), gkv_ref[...]).astype(BF16)
    q = _dot(nq, wq_ref[...])
    k = _dot(nkv, wk_ref[...])
    v_ref[...] = _dot(nkv, wv_ref[...]).astype(v_ref.dtype)
    kr = _rope_lanes(kr_ref[...].astype(F32), ck_cos_ref[...], ck_sin_ref[...], half)
    lane = lax.broadcasted_iota(I32, kr.shape, 1)
    kr = jnp.where(lane < B_ROPE_DIM, kr, 0.0)
    kr = pltpu.roll(kr, B_NOPE_DIM, 1)
    cos = cq_cos_ref[...]
    sin = cq_sin_ref[...]
    for c in range(q.shape[1] // LANES):
        sl = slice(c * LANES, (c + 1) * LANES)
        q_ref[:, sl] = _rope_lanes(q[:, sl], cos, sin, half).astype(q_ref.dtype)
        k_ref[:, sl] = (k[:, sl] + kr).astype(k_ref.dtype)


def mla_prep(proj, gq, gkv, wq, wk, wv, tables, *, tm, cq_blk, ckv_blk, kr_blk):
    t = proj.shape[0]
    nq = wq.shape[0]
    nkv = wk.shape[0]
    whole = lambda a: pl.BlockSpec(a.shape, lambda i: (0,) * a.ndim)
    tab = lambda k: pl.BlockSpec((None, tm, LANES), lambda i: (k, i, 0))
    in_specs = [
        pl.BlockSpec((tm, nq), lambda i: (i, cq_blk)),
        pl.BlockSpec((tm, nkv), lambda i: (i, ckv_blk)),
        pl.BlockSpec((tm, LANES), lambda i: (i, kr_blk)),
        whole(gq), whole(gkv), whole(wq), whole(wk), whole(wv),
        tab(0), tab(1), tab(2), tab(3),
    ]
    out_specs = [pl.BlockSpec((tm, wq.shape[1]), lambda i: (i, 0)),
                 pl.BlockSpec((tm, wk.shape[1]), lambda i: (i, 0)),
                 pl.BlockSpec((tm, wv.shape[1]), lambda i: (i, 0))]
    out_shape = [jax.ShapeDtypeStruct((t, wq.shape[1]), BF16),
                 jax.ShapeDtypeStruct((t, wk.shape[1]), BF16),
                 jax.ShapeDtypeStruct((t, wv.shape[1]), BF16)]
    return pl.pallas_call(
        _mla_prep_kernel, grid=(t // tm,), in_specs=in_specs, out_specs=out_specs,
        out_shape=out_shape, compiler_params=_params("parallel"), name="mla_prep",
    )(proj, proj, proj, gq, gkv, wq, wk, wv, tables, tables, tables, tables)


def _router_kernel(h_ref, g_ref, wr_ref, gate_ref, idx_ref, cnt_ref, carry_ref):
    tb = h_ref.shape[0]

    @pl.when(pl.program_id(0) == 0)
    def _():
        carry_ref[...] = jnp.zeros_like(carry_ref)

    n = _rms(h_ref[...], g_ref[...])
    logits = jnp.dot(n, wr_ref[...], preferred_element_type=F32, precision=lax.Precision.HIGHEST)
    lane = lax.broadcasted_iota(I32, (tb, LANES), 1).astype(F32)
    neg_inf = jnp.float32(-jnp.inf)
    lg = jnp.where(lane < N_EXPERTS, logits, neg_inf)
    m1 = jnp.max(lg, axis=-1, keepdims=True)
    e0 = jnp.min(jnp.where(lg == m1, lane, float(LANES)), axis=-1, keepdims=True)
    lg2 = jnp.where(lane == e0, neg_inf, lg)
    m2 = jnp.max(lg2, axis=-1, keepdims=True)
    e1 = jnp.min(jnp.where(lg2 == m2, lane, float(LANES)), axis=-1, keepdims=True)
    t = jnp.exp(m2 - m1)
    g0 = 1.0 / (1.0 + t)
    g1 = t / (1.0 + t)

    sel = ((lane == e0) | (lane == e1)).astype(F32)
    row = lax.broadcasted_iota(I32, (tb, tb), 0)
    col = lax.broadcasted_iota(I32, (tb, tb), 1)
    strict_lower = (col < row).astype(BF16)
    carry = carry_ref[...]
    excl = _dot(strict_lower, sel.astype(BF16)) + carry
    r0 = jnp.sum(jnp.where(lane == e0, excl, 0.0), axis=-1, keepdims=True)
    r1 = jnp.sum(jnp.where(lane == e1, excl, 0.0), axis=-1, keepdims=True)
    carry = carry + jnp.sum(sel, axis=0, keepdims=True)
    carry_ref[...] = carry

    gate_ref[...] = jnp.where(lane == 0, g0, jnp.where(lane == 1, g1, 0.0))
    idx_ref[...] = jnp.where(
        lane == 0, e0, jnp.where(lane == 1, e1, jnp.where(
            lane == 2, r0, jnp.where(lane == 3, r1, 0.0)))).astype(I32)
    cnt_ref[...] = jnp.broadcast_to(carry, cnt_ref.shape).astype(I32)


def router(h, g, wr_pad, *, tb):
    t, d = h.shape
    return pl.pallas_call(
        _router_kernel, grid=(t // tb,),
        in_specs=[pl.BlockSpec((tb, d), lambda i: (i, 0)),
                  pl.BlockSpec((1, d), lambda i: (0, 0)),
                  pl.BlockSpec((d, LANES), lambda i: (0, 0))],
        out_specs=[pl.BlockSpec((tb, LANES), lambda i: (i, 0)),
                   pl.BlockSpec((tb, LANES), lambda i: (i, 0)),
                   pl.BlockSpec((8, LANES), lambda i: (0, 0))],
        out_shape=[jax.ShapeDtypeStruct((t, LANES), F32),
                   jax.ShapeDtypeStruct((t, LANES), I32),
                   jax.ShapeDtypeStruct((8, LANES), I32)],
        scratch_shapes=[pltpu.VMEM((1, LANES), F32)],
        compiler_params=_params("arbitrary"), name="router",
    )(h, g, wr_pad)


def _row_copy(src_hbm, row, dst_ref, r, sem):
    return pltpu.make_async_copy(src_hbm.at[pl.ds(row, 1), :], dst_ref.at[pl.ds(r, 1), :], sem)


ROW_UNROLL = 8


def _for_rows(rows, body):
    def outer(o, carry):
        for u in range(ROW_UNROLL):
            body(o * ROW_UNROLL + u, u)
        return carry

    lax.fori_loop(0, rows // ROW_UNROLL, outer, 0)


GATHER_SLOTS = 3


def _gather_kernel(idx_first, idx_next, src_hbm, o_hbm, buf_ref, row_sem, blk_sem):
    i = pl.program_id(0)
    n = pl.num_programs(0)
    rows = idx_first.shape[-1]
    slot = lax.rem(i, GATHER_SLOTS)
    nxt = lax.rem(i + 1, GATHER_SLOTS)

    def start_rows(s, idx_ref):
        def start(r, u):
            _row_copy(src_hbm, idx_ref[0, 0, r], buf_ref.at[s], r, row_sem.at[s]).start(priority=u % 2)
        _for_rows(rows, start)

    def block_write(s, step):
        return pltpu.make_async_copy(buf_ref.at[s], o_hbm.at[pl.ds(step * rows, rows), :],
                                     blk_sem.at[s])

    @pl.when(i == 0)
    def _():
        start_rows(0, idx_first)

    @pl.when(i >= 2)
    def _():
        block_write(nxt, i - 2).wait()

    @pl.when(i + 1 < n)
    def _():
        start_rows(nxt, idx_next)

    _for_rows(rows, lambda r, u: _row_copy(src_hbm, 0, buf_ref.at[slot], r, row_sem.at[slot]).wait())
    block_write(slot, i).start()

    @pl.when(i == n - 1)
    def _():
        @pl.when(i >= 1)
        def _():
            block_write(lax.rem(i + 2, GATHER_SLOTS), i - 1).wait()
        block_write(slot, i).wait()


def gather_rows(src, idx, *, rows):
    p = idx.shape[0]
    d = src.shape[1]
    n = p // rows
    idx3 = idx.reshape(n, 1, rows)
    return pl.pallas_call(
        _gather_kernel, grid=(n,),
        in_specs=[pl.BlockSpec((1, 1, rows), lambda i: (0, 0, 0), memory_space=pltpu.SMEM),
                  pl.BlockSpec((1, 1, rows), lambda i: (jnp.minimum(i + 1, n - 1), 0, 0),
                               memory_space=pltpu.SMEM),
                  pl.BlockSpec(memory_space=pl.ANY)],
        out_specs=pl.BlockSpec(memory_space=pl.ANY),
        out_shape=jax.ShapeDtypeStruct((p, d), src.dtype),
        scratch_shapes=[pltpu.VMEM((GATHER_SLOTS, rows, d), src.dtype),
                        pltpu.SemaphoreType.DMA((GATHER_SLOTS,)),
                        pltpu.SemaphoreType.DMA((GATHER_SLOTS,))],
        compiler_params=_params("arbitrary"), name="gather_rows",
    )(idx3, idx3, src)


def _combine_kernel(p0_first, p1_first, p0_next, p1_next, h_ref, gate_ref, ys_hbm, o_ref,
                    buf_ref, sem):
    i = pl.program_id(0)
    n = pl.num_programs(0)
    rows = h_ref.shape[0]
    slot = i & 1

    def start_into(s, p0_ref, p1_ref):
        def start(r, u):
            _row_copy(ys_hbm, p0_ref[0, 0, r], buf_ref.at[s, 0], r, sem.at[s]).start(priority=0)
            _row_copy(ys_hbm, p1_ref[0, 0, r], buf_ref.at[s, 1], r, sem.at[s]).start(priority=1)
        _for_rows(rows, start)

    @pl.when(i == 0)
    def _():
        start_into(0, p0_first, p1_first)

    @pl.when(i + 1 < n)
    def _():
        start_into(1 - slot, p0_next, p1_next)

    def wait(r, u):
        _row_copy(ys_hbm, 0, buf_ref.at[slot, 0], r, sem.at[slot]).wait()
        _row_copy(ys_hbm, 0, buf_ref.at[slot, 1], r, sem.at[slot]).wait()

    _for_rows(rows, wait)
    gates = gate_ref[...]
    o_ref[...] = (h_ref[...] + gates[:, 0:1] * buf_ref[slot, 0]
                  + gates[:, 1:2] * buf_ref[slot, 1])


def combine_rows(h, gates, ys, pos0, pos1, *, rows):
    t, d = h.shape
    n = t // rows
    first = lambda: pl.BlockSpec((1, 1, rows), lambda i: (0, 0, 0), memory_space=pltpu.SMEM)
    nxt = lambda: pl.BlockSpec((1, 1, rows), lambda i: (jnp.minimum(i + 1, n - 1), 0, 0),
                               memory_space=pltpu.SMEM)
    pos0 = pos0.reshape(n, 1, rows)
    pos1 = pos1.reshape(n, 1, rows)
    return pl.pallas_call(
        _combine_kernel, grid=(n,),
        in_specs=[first(), first(), nxt(), nxt(),
                  pl.BlockSpec((rows, d), lambda i: (i, 0)),
                  pl.BlockSpec((rows, LANES), lambda i: (i, 0)),
                  pl.BlockSpec(memory_space=pl.ANY)],
        out_specs=pl.BlockSpec((rows, d), lambda i: (i, 0)),
        out_shape=jax.ShapeDtypeStruct((t, d), F32),
        scratch_shapes=[pltpu.VMEM((2, 2, rows, d), F32), pltpu.SemaphoreType.DMA((2,))],
        compiler_params=_params("arbitrary"), name="combine_rows",
    )(pos0, pos1, pos0, pos1, h, gates, ys)


def _rmsnorm_kernel(x_ref, g_ref, o_ref):
    o_ref[...] = _rms(x_ref[...], g_ref[...])


def rmsnorm(x, g, *, tm):
    t, d = x.shape
    return pl.pallas_call(
        _rmsnorm_kernel, grid=(t // tm,),
        in_specs=[pl.BlockSpec((tm, d), lambda i: (i, 0)), pl.BlockSpec((1, d), lambda i: (0, 0))],
        out_specs=pl.BlockSpec((tm, d), lambda i: (i, 0)),
        out_shape=jax.ShapeDtypeStruct((t, d), F32),
        compiler_params=_params("parallel"), name="final_rmsnorm",
    )(x, g)


def _rope_cos_sin(positions, dim):
    inv_freq = 1.0 / jnp.power(ROPE_THETA, jnp.arange(0, dim, 2, dtype=F32) / dim)
    ang = positions.reshape(-1).astype(F32)[:, None] * inv_freq
    return jnp.cos(ang), jnp.sin(ang)


def _diff_tables(positions):
    cos, sin = _rope_cos_sin(positions, C_HEAD_DIM)
    cos_l = jnp.tile(cos, (1, 4))
    sin_l = jnp.concatenate([-sin, -sin, sin, sin], axis=1)
    k_tab = jnp.stack([cos_l, sin_l])
    return jnp.stack([k_tab * (C_HEAD_DIM ** -0.5 * LOG2E), k_tab])


def _diff_head_order(w, n_cols):
    d = w.shape[0]
    half = C_HEAD_DIM // 2
    head = w[:, :n_cols].reshape(d, n_cols // LANES, 2, 2, half)
    head = head.transpose(0, 1, 3, 2, 4).reshape(d, n_cols)
    return jnp.concatenate([head, w[:, n_cols:]], axis=1)


def _mla_tables(positions):
    cos, sin = _rope_cos_sin(positions, B_ROPE_DIM)
    t = cos.shape[0]
    scale = (B_NOPE_DIM + B_ROPE_DIM) ** -0.5 * LOG2E
    ones = lambda n: jnp.ones((t, n), F32)
    zeros = lambda n: jnp.zeros((t, n), F32)
    q_cos = jnp.concatenate([ones(B_NOPE_DIM), cos, cos, ones(32)], axis=1) * scale
    q_sin = jnp.concatenate([zeros(B_NOPE_DIM), -sin, sin, zeros(32)], axis=1) * scale
    k_cos = jnp.concatenate([cos, cos, ones(96)], axis=1)
    k_sin = jnp.concatenate([-sin, sin, zeros(96)], axis=1)
    return jnp.stack([q_cos, q_sin, k_cos, k_sin])


def _routing_plan(idx_out, counts, *, tm, n_tiles):
    t = idx_out.shape[0]
    e0, e1, r0, r1 = idx_out[:, 0], idx_out[:, 1], idx_out[:, 2], idx_out[:, 3]
    padded = ((counts + tm - 1) // tm) * tm
    ends = jnp.cumsum(padded)
    offs = ends - padded
    experts = jnp.arange(N_EXPERTS, dtype=I32)[None, :]
    pos0 = jnp.sum(jnp.where(e0[:, None] == experts, offs[None, :], 0), axis=1) + r0
    pos1 = jnp.sum(jnp.where(e1[:, None] == experts, offs[None, :], 0), axis=1) + r1
    tok = jnp.arange(t, dtype=I32)
    src = jnp.zeros((n_tiles * tm,), I32).at[pos0].set(tok).at[pos1].set(tok)
    n_valid = (ends[-1] // tm).astype(I32)
    tile = jnp.minimum(jnp.arange(n_tiles, dtype=I32), n_valid - 1)
    tile_expert = jnp.minimum(jnp.sum((tile * tm)[:, None] >= ends[None, :], axis=1),
                              N_EXPERTS - 1)
    return pos0.astype(I32), pos1.astype(I32), src, tile_expert.astype(I32), n_valid.reshape(1)


def _pad_cols(w, n):
    return jnp.pad(w, ((0, 0), (0, n - w.shape[1])))


def kernel(x, positions, mix_norm_g, ffn_norm_g, w_in_even, rel_bias_a, q_norm_b, w_uq_b, kv_norm_b, w_ukv_b, w_out_even, w_in_odd, lambda_q1, lambda_k1, lambda_q2, lambda_k2, subln_g, w_out_odd, w_gu_dense, w_down_dense, w_router, w_gu_moe, w_down_moe, final_norm_g):
    b, s, d = x.shape
    t = b * s
    depth = mix_norm_g.shape[0]
    a_width = 4 * LANES
    tm = 512
    tm_moe = 512
    n_tiles_moe = (2 * t) // tm_moe + N_EXPERTS

    diff_tab = _diff_tables(positions)
    mla_tab = _mla_tables(positions)
    row = lambda v: v.reshape(1, -1).astype(F32)

    h = x.reshape(t, d).astype(F32)
    for layer in range(depth):
        i = layer // 2
        if layer % 2 == 0:
            w_in = _pad_cols(w_in_even[i], 2048).astype(BF16)
            proj = rms_matmul(h, row(mix_norm_g[layer]), w_in, tm=tm, tn=1024)
            proj3 = proj.reshape(b, s, -1)
            out_a = band_attention(proj3, band_bias_tiles(rel_bias_a[i]),
                                   q_blk0=0, k_blk0=4, v_blk0=8)
            n_heads = w_uq_b.shape[2] // (B_NOPE_DIM + B_ROPE_DIM)
            wq = jnp.pad(w_uq_b[i].reshape(-1, n_heads, B_NOPE_DIM + B_ROPE_DIM),
                         ((0, 0), (0, 0), (0, 32))).reshape(-1, n_heads * LANES).astype(BF16)
            wkv = w_ukv_b[i].reshape(-1, n_heads, B_NOPE_DIM + B_V_DIM)
            wk = jnp.pad(wkv[:, :, :B_NOPE_DIM], ((0, 0), (0, 0), (0, LANES - B_NOPE_DIM)))
            wk = wk.reshape(-1, n_heads * LANES).astype(BF16)
            wv = wkv[:, :, B_NOPE_DIM:].reshape(-1, n_heads * B_V_DIM).astype(BF16)
            q_b, k_b, v_b = mla_prep(proj, row(q_norm_b[i]), row(kv_norm_b[i]), wq, wk, wv, mla_tab,
                                     tm=tm, cq_blk=6, ckv_blk=14, kr_blk=15)
            out_b = flash_attention(q_b.reshape(b, s, -1), k_b.reshape(b, s, -1),
                                    v_b.reshape(b, s, -1), mode="mla", q_blk0=0, k_blk0=0,
                                    v_blk0=0, n_groups=n_heads // 2, tq=512)
            w_out = w_out_even[i].astype(BF16)
            h = matmul_residual([out_a.reshape(t, -1), out_b.reshape(t, -1)],
                                [w_out[:a_width], w_out[a_width:]], h, tm=tm)
            n_tiles = t // tm
            h = grouped_swiglu(h, row(ffn_norm_g[layer]), w_gu_dense[i][None].astype(BF16),
                               w_down_dense[i][None].astype(BF16), jnp.zeros((n_tiles,), I32),
                               jnp.full((1,), n_tiles, I32), tm=tm, tf=1408, add_res=True)
        else:
            c_width = w_out_odd.shape[1]
            n_heads = c_width // LANES
            w_in = _diff_head_order(w_in_odd[i], 2 * c_width).astype(BF16)
            qkv = rms_matmul(h, row(mix_norm_g[layer]), w_in, tm=tm, tn=512,
                             rope=(diff_tab, c_width // 512, LANES // 2))
            lam_init = 0.8 - 0.6 * math.exp(-0.3 * layer)
            lam = (jnp.exp(jnp.sum(lambda_q1[i].astype(F32) * lambda_k1[i].astype(F32)))
                   - jnp.exp(jnp.sum(lambda_q2[i].astype(F32) * lambda_k2[i].astype(F32)))
                   + lam_init).reshape(1).astype(F32)
            qkv3 = qkv.reshape(b, s, -1)
            out_c = flash_attention(qkv3, qkv3, qkv3, mode="diff", q_blk0=0, k_blk0=n_heads,
                                    v_blk0=2 * n_heads, n_groups=n_heads, tq=512, lam=lam,
                                    subln_g=row(subln_g[i]), out_scale=1.0 - lam_init)
            h = matmul_residual([out_c.reshape(t, -1)], [w_out_odd[i].astype(BF16)], h, tm=tm)

            g_ffn = row(ffn_norm_g[layer])
            wr_pad = _pad_cols(w_router[i].astype(F32), LANES)
            gates, idx_out, cnt = router(h, g_ffn, wr_pad, tb=512)
            pos0, pos1, src, tile_expert, n_valid = _routing_plan(
                idx_out, cnt[0, :N_EXPERTS], tm=tm_moe, n_tiles=n_tiles_moe)
            xs = gather_rows(h, src, rows=256)
            ys = grouped_swiglu(xs, g_ffn, w_gu_moe[i].astype(BF16), w_down_moe[i].astype(BF16),
                                tile_expert, n_valid, tm=tm_moe, tf=1792, add_res=False)
            h = combine_rows(h, gates, ys, pos0, pos1, rows=256)

    out = rmsnorm(h, row(final_norm_g), tm=tm)
    return out.reshape(b, s, d)
```

```python
import functools
import math

import jax
import jax.numpy as jnp
import numpy as np
from jax import lax
from jax.experimental import pallas as pl
from jax.experimental.pallas import tpu as pltpu

F32 = jnp.float32
BF16 = jnp.bfloat16
I32 = jnp.int32

NORM_EPS = 1e-6
ROPE_THETA = 10000.0
NEG = -1e30
LOG2E = math.log2(math.e)
LANES = 128
CHUNK = 64

A_LEFT_CHUNKS = 8
A_MAX_REL = 256
A_HEAD_DIM = 64
B_NOPE_DIM = 64
B_ROPE_DIM = 32
B_V_DIM = 64
C_HEAD_DIM = 64
N_EXPERTS = 8

VMEM_LIMIT = 56 * 1024 * 1024


def _params(*sem):
    return pltpu.CompilerParams(dimension_semantics=sem, vmem_limit_bytes=VMEM_LIMIT)


def _rms(x, g):
    ms = jnp.mean(x * x, axis=-1, keepdims=True)
    return x * lax.rsqrt(ms + NORM_EPS) * g


def _dot(a, b):
    return jnp.dot(a, b, preferred_element_type=F32)


def _dot_nt(a, b):
    return lax.dot_general(a, b, (((1,), (1,)), ((), ())), preferred_element_type=F32)


def _rope_lanes(x, cos, sin, half):
    if 2 * half == LANES:
        swapped = pltpu.roll(x, half, 1)
    else:
        lane = lax.broadcasted_iota(I32, x.shape, 1)
        fwd = pltpu.roll(x, LANES - half, 1)
        bwd = pltpu.roll(x, half, 1)
        swapped = jnp.where((lane & half) == 0, fwd, bwd)
    return x * cos + swapped * sin


def _rms_matmul_kernel(x_ref, g_ref, w_ref, o_ref, xn_ref):
    @pl.when(pl.program_id(1) == 0)
    def _():
        xn_ref[...] = _rms(x_ref[...], g_ref[...]).astype(BF16)

    o_ref[...] = _dot(xn_ref[...], w_ref[...]).astype(o_ref.dtype)


def _rms_matmul_rope_kernel(x_ref, g_ref, w_ref, cos_ref, sin_ref, o_ref, xn_ref, *,
                            n_rope_tiles, half):
    j = pl.program_id(1)

    @pl.when(j == 0)
    def _():
        xn_ref[...] = _rms(x_ref[...], g_ref[...]).astype(BF16)

    acc = _dot(xn_ref[...], w_ref[...])

    @pl.when(j < n_rope_tiles)
    def _():
        cos = cos_ref[...]
        sin = sin_ref[...]
        for c in range(acc.shape[1] // LANES):
            sl = slice(c * LANES, (c + 1) * LANES)
            o_ref[:, sl] = _rope_lanes(acc[:, sl], cos, sin, half).astype(o_ref.dtype)

    @pl.when(j >= n_rope_tiles)
    def _():
        o_ref[...] = acc.astype(o_ref.dtype)


def rms_matmul(x, g, w, *, tm, tn, rope=None):
    t, d = x.shape
    n = w.shape[1]
    grid = (t // tm, n // tn)
    x_spec = pl.BlockSpec((tm, d), lambda i, j: (i, 0))
    g_spec = pl.BlockSpec((1, d), lambda i, j: (0, 0))
    w_spec = pl.BlockSpec((d, tn), lambda i, j: (0, j))
    o_spec = pl.BlockSpec((tm, tn), lambda i, j: (i, j))
    scratch = [pltpu.VMEM((tm, d), BF16)]
    out_shape = jax.ShapeDtypeStruct((t, n), BF16)
    if rope is None:
        return pl.pallas_call(
            _rms_matmul_kernel, grid=grid, in_specs=[x_spec, g_spec, w_spec], out_specs=o_spec,
            out_shape=out_shape, scratch_shapes=scratch,
            compiler_params=_params("parallel", "arbitrary"), name="rms_matmul",
        )(x, g, w)
    tables, per_part, half = rope
    cos_spec = pl.BlockSpec((None, None, tm, LANES),
                            lambda i, j: (jnp.minimum(j // per_part, 1), 0, i, 0))
    sin_spec = pl.BlockSpec((None, None, tm, LANES),
                            lambda i, j: (jnp.minimum(j // per_part, 1), 1, i, 0))
    kern = functools.partial(_rms_matmul_rope_kernel, n_rope_tiles=2 * per_part, half=half)
    return pl.pallas_call(
        kern, grid=grid, in_specs=[x_spec, g_spec, w_spec, cos_spec, sin_spec], out_specs=o_spec,
        out_shape=out_shape, scratch_shapes=scratch,
        compiler_params=_params("parallel", "arbitrary"), name="rms_matmul_rope",
    )(x, g, w, tables, tables)


def _mm_res_kernel(*refs, n_in):
    res_ref = refs[2 * n_in]
    o_ref = refs[2 * n_in + 1]
    acc = res_ref[...]
    for k in range(n_in):
        acc = acc + _dot(refs[k][...], refs[n_in + k][...])
    o_ref[...] = acc


def matmul_residual(a_list, w_list, res, *, tm):
    t, n = res.shape
    n_in = len(a_list)
    in_specs = [pl.BlockSpec((tm, a.shape[1]), lambda i: (i, 0)) for a in a_list]
    in_specs += [pl.BlockSpec(w.shape, lambda i: (0, 0)) for w in w_list]
    in_specs += [pl.BlockSpec((tm, n), lambda i: (i, 0))]
    return pl.pallas_call(
        functools.partial(_mm_res_kernel, n_in=n_in), grid=(t // tm,), in_specs=in_specs,
        out_specs=pl.BlockSpec((tm, n), lambda i: (i, 0)),
        out_shape=jax.ShapeDtypeStruct((t, n), F32),
        compiler_params=_params("parallel"), name="matmul_residual",
    )(*a_list, *w_list, res)


def _ffn_kernel(te_ref, nv_ref, x_ref, g_ref, wg_ref, wu_ref, wd_ref, o_ref, xn_ref, acc_ref, *,
                add_res):
    del te_ref
    i = pl.program_id(0)
    f = pl.program_id(1)
    nf = pl.num_programs(1)
    valid = i < nv_ref[0]

    @pl.when(valid & (f == 0))
    def _():
        xn_ref[...] = _rms(x_ref[...], g_ref[...]).astype(BF16)
        acc_ref[...] = jnp.zeros_like(acc_ref)

    @pl.when(valid)
    def _():
        xn = xn_ref[...]
        gate = _dot(xn, wg_ref[...])
        up = _dot(xn, wu_ref[...])
        act = (gate * jax.nn.sigmoid(gate) * up).astype(BF16)
        acc_ref[...] += _dot(act, wd_ref[...])

    @pl.when(valid & (f == nf - 1))
    def _():
        if add_res:
            o_ref[...] = x_ref[...] + acc_ref[...]
        else:
            o_ref[...] = acc_ref[...]

    @pl.when(jnp.logical_not(valid) & (f == nf - 1))
    def _():
        o_ref[...] = jnp.zeros_like(o_ref)


def grouped_swiglu(x, g, w_gu, w_down, tile_expert, n_valid, *, tm, tf, add_res):
    p, d = x.shape
    ff = w_down.shape[1]
    nf = ff // tf
    grid = (p // tm, nf)

    def f_eff(i, f, nv):
        return jnp.where(i < nv[0], f, nf - 1)

    in_specs = [
        pl.BlockSpec((tm, d), lambda i, f, te, nv: (i, 0)),
        pl.BlockSpec((1, d), lambda i, f, te, nv: (0, 0)),
        pl.BlockSpec((None, d, tf), lambda i, f, te, nv: (te[i], 0, f_eff(i, f, nv))),
        pl.BlockSpec((None, d, tf), lambda i, f, te, nv: (te[i], 0, nf + f_eff(i, f, nv))),
        pl.BlockSpec((None, tf, d), lambda i, f, te, nv: (te[i], f_eff(i, f, nv), 0)),
    ]
    grid_spec = pltpu.PrefetchScalarGridSpec(
        num_scalar_prefetch=2, grid=grid, in_specs=in_specs,
        out_specs=pl.BlockSpec((tm, d), lambda i, f, te, nv: (i, 0)),
        scratch_shapes=[pltpu.VMEM((tm, d), BF16), pltpu.VMEM((tm, d), F32)],
    )
    return pl.pallas_call(
        functools.partial(_ffn_kernel, add_res=add_res), grid_spec=grid_spec,
        out_shape=jax.ShapeDtypeStruct((p, d), F32),
        compiler_params=_params("parallel", "arbitrary"), name="grouped_swiglu",
    )(tile_expert, n_valid, x, g, w_gu, w_gu, w_down)


def _flash_kernel(lam_ref, q_ref, k_ref, v_ref, g_ref, o_ref, m_ref, l_ref, acc_ref, vt_ref,
                  s_ref, *, mode, tq, groups, out_scale):
    i = pl.program_id(2)
    s_len = v_ref.shape[0]
    wqk = q_ref.shape[1] // groups
    n_maps = 2 * groups
    sub = lax.broadcasted_iota(I32, (LANES, tq), 0)
    top = sub < (LANES // 2)

    @pl.when(i == 0)
    def _():
        for c in range(s_len // tq):
            sl = slice(c * tq, (c + 1) * tq)
            vt_ref[:, sl] = v_ref[sl, :].T

    qts = []
    for g in range(groups):
        if mode == "mla":
            qts += [q_ref[:, g * wqk:g * wqk + LANES].T, q_ref[:, g * wqk + LANES:(g + 1) * wqk].T]
        else:
            qt = q_ref[:, g * wqk:(g + 1) * wqk].T
            zero = jnp.zeros_like(qt)
            map1 = (sub & (C_HEAD_DIM // 2)) == 0
            qts += [jnp.where(map1, qt, zero), jnp.where(map1, zero, qt)]

    m_ref[...] = jnp.full_like(m_ref, NEG)
    l_ref[...] = jnp.zeros_like(l_ref)
    acc_ref[...] = jnp.zeros_like(acc_ref)

    def k_tile(mi, start):
        g, idx = divmod(mi, 2)
        if mode == "mla":
            return k_ref[pl.ds(start, tq), g * wqk + idx * LANES:g * wqk + (idx + 1) * LANES]
        return k_ref[pl.ds(start, tq), g * wqk:(g + 1) * wqk]

    def update(mi, s, vt_tile):
        m_prev = m_ref[mi]
        m_new = jnp.maximum(m_prev, jnp.max(s, axis=0, keepdims=True))
        alpha = jnp.exp2(m_prev - m_new)
        p = jnp.exp2(s - m_new)
        l_ref[mi] = alpha * l_ref[mi] + jnp.sum(p, axis=0, keepdims=True)
        acc_ref[mi] = alpha * acc_ref[mi] + _dot(vt_tile, p.astype(BF16))
        m_ref[mi] = m_new

    def scores_into(slot, t):
        start = pl.multiple_of(t * tq, tq)
        for mi in range(n_maps):
            s_ref[slot, mi] = _dot(k_tile(mi, start), qts[mi])

    def consume(slot, t, masked):
        start = pl.multiple_of(t * tq, tq)
        for mi in range(n_maps):
            g = mi // 2
            s = s_ref[slot, mi]
            if masked:
                row = lax.broadcasted_iota(I32, (tq, tq), 0)
                col = lax.broadcasted_iota(I32, (tq, tq), 1)
                shift = CHUNK.bit_length() - 1
                s = jnp.where((row >> shift) <= (col >> shift), s, NEG)
            update(mi, s, vt_ref[g * LANES:(g + 1) * LANES, pl.ds(start, tq)])

    scores_into(0, 0)

    def tile_pair(u, carry):
        t = 2 * u
        scores_into(1, t + 1)
        consume(0, t, False)
        scores_into(0, t + 2)
        consume(1, t + 1, False)
        return carry

    lax.fori_loop(0, i >> 1, tile_pair, 0)

    @pl.when((i & 1) == 1)
    def _():
        scores_into(1, i)
        consume(0, i - 1, False)
        consume(1, i, True)

    @pl.when((i & 1) == 0)
    def _():
        consume(0, i, True)

    for g in range(groups):
        o0 = acc_ref[2 * g] * (1.0 / l_ref[2 * g])
        o1 = acc_ref[2 * g + 1] * (1.0 / l_ref[2 * g + 1])
        cols = slice(g * LANES, (g + 1) * LANES)
        if mode == "mla":
            o_ref[:, cols] = jnp.where(top, o0, o1).T.astype(o_ref.dtype)
        else:
            o = (o0 - lam_ref[0] * o1).T
            o_ref[:, cols] = (_rms(o, g_ref[...]) * out_scale).astype(o_ref.dtype)


def flash_attention(q_arr, k_arr, v_arr, *, mode, q_blk0, k_blk0, v_blk0, n_groups, tq,
                    groups=1, lam=None, subln_g=None, out_scale=1.0):
    b, s, _ = q_arr.shape
    wqk = (2 * LANES if mode == "mla" else LANES) * groups
    wv = LANES * groups
    assert n_groups % groups == 0 and q_blk0 % groups == 0 and k_blk0 % groups == 0
    assert v_blk0 % groups == 0
    if lam is None:
        lam = jnp.zeros((1,), F32)
    if subln_g is None:
        subln_g = jnp.ones((1, LANES), F32)
    grid = (b, n_groups // groups, s // tq)
    qb, kb, vb = q_blk0 // groups, k_blk0 // groups, v_blk0 // groups
    in_specs = [
        pl.BlockSpec((None, tq, wqk), lambda bb, g, i, lam_r: (bb, i, qb + g)),
        pl.BlockSpec((None, s, wqk), lambda bb, g, i, lam_r: (bb, 0, kb + g)),
        pl.BlockSpec((None, s, wv), lambda bb, g, i, lam_r: (bb, 0, vb + g)),
        pl.BlockSpec((1, LANES), lambda bb, g, i, lam_r: (0, 0)),
    ]
    n_maps = 2 * groups
    grid_spec = pltpu.PrefetchScalarGridSpec(
        num_scalar_prefetch=1, grid=grid, in_specs=in_specs,
        out_specs=pl.BlockSpec((None, tq, wv), lambda bb, g, i, lam_r: (bb, i, g)),
        scratch_shapes=[pltpu.VMEM((n_maps, 1, tq), F32), pltpu.VMEM((n_maps, 1, tq), F32),
                        pltpu.VMEM((n_maps, LANES, tq), F32), pltpu.VMEM((wv, s), BF16),
                        pltpu.VMEM((2, n_maps, tq, tq), F32)],
    )
    kern = functools.partial(_flash_kernel, mode=mode, tq=tq, groups=groups, out_scale=out_scale)
    return pl.pallas_call(
        kern, grid_spec=grid_spec,
        out_shape=jax.ShapeDtypeStruct((b, s, n_groups * LANES), BF16),
        compiler_params=_params("parallel", "parallel", "arbitrary"), name="flash_" + mode,
    )(lam, q_arr, k_arr, v_arr, subln_g)


BAND_TQ = 256
BAND_TILES = 3
BAND_PAIRS = 2


def _band_kernel(q_ref, k0_ref, k1_ref, k2_ref, v0_ref, v1_ref, v2_ref, bias_ref, o_ref, *, scale):
    i = pl.program_id(2)
    lane = lax.broadcasted_iota(I32, (BAND_TQ, LANES), 1)
    low = lane < (LANES // 2)
    k_refs = (k0_ref, k1_ref, k2_ref)
    v_refs = (v0_ref, v1_ref, v2_ref)
    for pp in range(BAND_PAIRS):
        cols = slice(pp * LANES, (pp + 1) * LANES)
        q = q_ref[:, cols].astype(F32) * scale
        qs = (jnp.where(low, q, 0.0).astype(BF16), jnp.where(low, 0.0, q).astype(BF16))
        outs = []
        for h in range(2):
            scores = []
            for j in range(BAND_TILES):
                pen = jnp.where(i + j >= BAND_TILES - 1, 0.0, NEG).astype(F32)
                scores.append(_dot_nt(qs[h], k_refs[j][:, cols]) + bias_ref[pp, h, j] + pen)
            m = jnp.maximum(jnp.maximum(jnp.max(scores[0], axis=-1, keepdims=True),
                                        jnp.max(scores[1], axis=-1, keepdims=True)),
                            jnp.max(scores[2], axis=-1, keepdims=True))
            l = jnp.zeros_like(m)
            o = jnp.zeros((BAND_TQ, LANES), F32)
            for j in range(BAND_TILES):
                p = jnp.exp(scores[j] - m)
                l = l + jnp.sum(p, axis=-1, keepdims=True)
                o = o + _dot(p.astype(BF16), v_refs[j][:, cols])
            outs.append(o * (1.0 / l))
        o_ref[:, cols] = jnp.where(low, outs[0], outs[1]).astype(o_ref.dtype)


def band_bias_tiles(rel_bias):
    h = rel_bias.shape[0]
    n = BAND_TQ
    r = np.arange(n)[:, None]
    c = np.arange(n)[None, :]
    e_of = np.zeros(2 * n, np.int64)
    e_of[:n] = -np.arange(n)
    e_of[n + 1:] = n - 1 - np.arange(n - 1)
    tiles = []
    for j in range(BAND_TILES):
        rel = (BAND_TILES - 1 - j) * n + e_of
        gen = rel_bias[:, np.clip(rel, -A_MAX_REL, A_MAX_REL) + A_MAX_REL].astype(F32)
        skew = jnp.tile(gen, (1, n))[:, :n * (2 * n - 1)].reshape(h, n, 2 * n - 1)[:, :, :n]
        dist = (r // CHUNK) - (c // CHUNK) + (BAND_TILES - 1 - j) * (n // CHUNK)
        ok = (dist >= 0) & (dist <= A_LEFT_CHUNKS)
        tiles.append(jnp.where(ok[None], skew, NEG))
    t = jnp.stack(tiles, axis=1)
    return t.reshape(h // 2, 2, BAND_TILES, BAND_TQ, BAND_TQ)


def band_attention(proj, bias_tiles, *, q_blk0, k_blk0, v_blk0):
    b, s, _ = proj.shape
    n_pairs = bias_tiles.shape[0]
    width = BAND_PAIRS * LANES
    assert n_pairs % BAND_PAIRS == 0 and q_blk0 % BAND_PAIRS == 0
    assert k_blk0 % BAND_PAIRS == 0 and v_blk0 % BAND_PAIRS == 0
    grid = (n_pairs // BAND_PAIRS, b, s // BAND_TQ)

    def kv_spec(blk0, j):
        return pl.BlockSpec(
            (None, BAND_TQ, width),
            lambda p, bb, i: (bb, jnp.maximum(i + j - (BAND_TILES - 1), 0),
                              blk0 // BAND_PAIRS + p))

    in_specs = [pl.BlockSpec((None, BAND_TQ, width),
                             lambda p, bb, i: (bb, i, q_blk0 // BAND_PAIRS + p))]
    in_specs += [kv_spec(k_blk0, j) for j in range(BAND_TILES)]
    in_specs += [kv_spec(v_blk0, j) for j in range(BAND_TILES)]
    in_specs += [pl.BlockSpec((BAND_PAIRS, 2, BAND_TILES, BAND_TQ, BAND_TQ),
                              lambda p, bb, i: (p, 0, 0, 0, 0))]
    return pl.pallas_call(
        functools.partial(_band_kernel, scale=A_HEAD_DIM ** -0.5), grid=grid, in_specs=in_specs,
        out_specs=pl.BlockSpec((None, BAND_TQ, width), lambda p, bb, i: (bb, i, p)),
        out_shape=jax.ShapeDtypeStruct((b, s, n_pairs * LANES), BF16),
        compiler_params=_params("parallel", "parallel", "parallel"), name="band_attention",
    )(proj, proj, proj, proj, proj, proj, proj, bias_tiles)


def _mla_prep_kernel(cq_ref, ckv_ref, kr_ref, gq_ref, gkv_ref, wq_ref, wk_ref, wv_ref,
                     cq_cos_ref, cq_sin_ref, ck_cos_ref, ck_sin_ref, q_ref, k_ref, v_ref):
    half = B_ROPE_DIM // 2
    nq = _rms(cq_ref[...].astype(F32), gq_ref[...]).astype(BF16)
    nkv = _rms(ckv_ref[...].astype(F32), gkv_ref[...]).astype(BF16)
    q = _dot(nq, wq_ref[...])
    k = _dot(nkv, wk_ref[...])
    v_ref[...] = _dot(nkv, wv_ref[...]).astype(v_ref.dtype)
    kr = _rope_lanes(kr_ref[...].astype(F32), ck_cos_ref[...], ck_sin_ref[...], half)
    lane = lax.broadcasted_iota(I32, kr.shape, 1)
    kr = jnp.where(lane < B_ROPE_DIM, kr, 0.0)
    kr = pltpu.roll(kr, B_NOPE_DIM, 1)
    cos = cq_cos_ref[...]
    sin = cq_sin_ref[...]
    for c in range(q.shape[1] // LANES):
        sl = slice(c * LANES, (c + 1) * LANES)
        q_ref[:, sl] = _rope_lanes(q[:, sl], cos, sin, half).astype(q_ref.dtype)
        k_ref[:, sl] = (k[:, sl] + kr).astype(k_ref.dtype)


def mla_prep(proj, gq, gkv, wq, wk, wv, tables, *, tm, cq_blk, ckv_blk, kr_blk):
    t = proj.shape[0]
    nq = wq.shape[0]
    nkv = wk.shape[0]
    whole = lambda a: pl.BlockSpec(a.shape, lambda i: (0,) * a.ndim)
    tab = lambda k: pl.BlockSpec((None, tm, LANES), lambda i: (k, i, 0))
    in_specs = [
        pl.BlockSpec((tm, nq), lambda i: (i, cq_blk)),
        pl.BlockSpec((tm, nkv), lambda i: (i, ckv_blk)),
        pl.BlockSpec((tm, LANES), lambda i: (i, kr_blk)),
        whole(gq), whole(gkv), whole(wq), whole(wk), whole(wv),
        tab(0), tab(1), tab(2), tab(3),
    ]
    out_specs = [pl.BlockSpec((tm, wq.shape[1]), lambda i: (i, 0)),
                 pl.BlockSpec((tm, wk.shape[1]), lambda i: (i, 0)),
                 pl.BlockSpec((tm, wv.shape[1]), lambda i: (i, 0))]
    out_shape = [jax.ShapeDtypeStruct((t, wq.shape[1]), BF16),
                 jax.ShapeDtypeStruct((t, wk.shape[1]), BF16),
                 jax.ShapeDtypeStruct((t, wv.shape[1]), BF16)]
    return pl.pallas_call(
        _mla_prep_kernel, grid=(t // tm,), in_specs=in_specs, out_specs=out_specs,
        out_shape=out_shape, compiler_params=_params("parallel"), name="mla_prep",
    )(proj, proj, proj, gq, gkv, wq, wk, wv, tables, tables, tables, tables)


def _router_kernel(h_ref, g_ref, wr_ref, gate_ref, idx_ref, cnt_ref, carry_ref):
    tb = h_ref.shape[0]

    @pl.when(pl.program_id(0) == 0)
    def _():
        carry_ref[...] = jnp.zeros_like(carry_ref)

    n = _rms(h_ref[...], g_ref[...])
    logits = jnp.dot(n, wr_ref[...], preferred_element_type=F32, precision=lax.Precision.HIGHEST)
    lane = lax.broadcasted_iota(I32, (tb, LANES), 1).astype(F32)
    neg_inf = jnp.float32(-jnp.inf)
    lg = jnp.where(lane < N_EXPERTS, logits, neg_inf)
    m1 = jnp.max(lg, axis=-1, keepdims=True)
    e0 = jnp.min(jnp.where(lg == m1, lane, float(LANES)), axis=-1, keepdims=True)
    lg2 = jnp.where(lane == e0, neg_inf, lg)
    m2 = jnp.max(lg2, axis=-1, keepdims=True)
    e1 = jnp.min(jnp.where(lg2 == m2, lane, float(LANES)), axis=-1, keepdims=True)
    t = jnp.exp(m2 - m1)
    g0 = 1.0 / (1.0 + t)
    g1 = t / (1.0 + t)

    sel = ((lane == e0) | (lane == e1)).astype(F32)
    row = lax.broadcasted_iota(I32, (tb, tb), 0)
    col = lax.broadcasted_iota(I32, (tb, tb), 1)
    strict_lower = (col < row).astype(BF16)
    carry = carry_ref[...]
    excl = _dot(strict_lower, sel.astype(BF16)) + carry
    r0 = jnp.sum(jnp.where(lane == e0, excl, 0.0), axis=-1, keepdims=True)
    r1 = jnp.sum(jnp.where(lane == e1, excl, 0.0), axis=-1, keepdims=True)
    carry = carry + jnp.sum(sel, axis=0, keepdims=True)
    carry_ref[...] = carry

    gate_ref[...] = jnp.where(lane == 0, g0, jnp.where(lane == 1, g1, 0.0))
    idx_ref[...] = jnp.where(
        lane == 0, e0, jnp.where(lane == 1, e1, jnp.where(
            lane == 2, r0, jnp.where(lane == 3, r1, 0.0)))).astype(I32)
    cnt_ref[...] = jnp.broadcast_to(carry, cnt_ref.shape).astype(I32)


def router(h, g, wr_pad, *, tb):
    t, d = h.shape
    return pl.pallas_call(
        _router_kernel, grid=(t // tb,),
        in_specs=[pl.BlockSpec((tb, d), lambda i: (i, 0)),
                  pl.BlockSpec((1, d), lambda i: (0, 0)),
                  pl.BlockSpec((d, LANES), lambda i: (0, 0))],
        out_specs=[pl.BlockSpec((tb, LANES), lambda i: (i, 0)),
                   pl.BlockSpec((tb, LANES), lambda i: (i, 0)),
                   pl.BlockSpec((8, LANES), lambda i: (0, 0))],
        out_shape=[jax.ShapeDtypeStruct((t, LANES), F32),
                   jax.ShapeDtypeStruct((t, LANES), I32),
                   jax.ShapeDtypeStruct((8, LANES), I32)],
        scratch_shapes=[pltpu.VMEM((1, LANES), F32)],
        compiler_params=_params("arbitrary"), name="router",
    )(h, g, wr_pad)


def _row_copy(src_hbm, row, dst_ref, r, sem):
    return pltpu.make_async_copy(src_hbm.at[pl.ds(row, 1), :], dst_ref.at[pl.ds(r, 1), :], sem)


ROW_UNROLL = 8


def _for_rows(rows, body):
    def outer(o, carry):
        for u in range(ROW_UNROLL):
            body(o * ROW_UNROLL + u, u)
        return carry

    lax.fori_loop(0, rows // ROW_UNROLL, outer, 0)


GATHER_SLOTS = 3


def _gather_kernel(idx_first, idx_next, src_hbm, o_hbm, buf_ref, row_sem, blk_sem):
    i = pl.program_id(0)
    n = pl.num_programs(0)
    rows = idx_first.shape[-1]
    slot = lax.rem(i, GATHER_SLOTS)
    nxt = lax.rem(i + 1, GATHER_SLOTS)

    def start_rows(s, idx_ref):
        def start(r, u):
            _row_copy(src_hbm, idx_ref[0, 0, r], buf_ref.at[s], r, row_sem.at[s]).start(priority=u % 2)
        _for_rows(rows, start)

    def block_write(s, step):
        return pltpu.make_async_copy(buf_ref.at[s], o_hbm.at[pl.ds(step * rows, rows), :],
                                     blk_sem.at[s])

    @pl.when(i == 0)
    def _():
        start_rows(0, idx_first)

    @pl.when(i >= 2)
    def _():
        block_write(nxt, i - 2).wait()

    @pl.when(i + 1 < n)
    def _():
        start_rows(nxt, idx_next)

    _for_rows(rows, lambda r, u: _row_copy(src_hbm, 0, buf_ref.at[slot], r, row_sem.at[slot]).wait())
    block_write(slot, i).start()

    @pl.when(i == n - 1)
    def _():
        @pl.when(i >= 1)
        def _():
            block_write(lax.rem(i + 2, GATHER_SLOTS), i - 1).wait()
        block_write(slot, i).wait()


def gather_rows(src, idx, *, rows):
    p = idx.shape[0]
    d = src.shape[1]
    n = p // rows
    idx3 = idx.reshape(n, 1, rows)
    return pl.pallas_call(
        _gather_kernel, grid=(n,),
        in_specs=[pl.BlockSpec((1, 1, rows), lambda i: (0, 0, 0), memory_space=pltpu.SMEM),
                  pl.BlockSpec((1, 1, rows), lambda i: (jnp.minimum(i + 1, n - 1), 0, 0),
                               memory_space=pltpu.SMEM),
                  pl.BlockSpec(memory_space=pl.ANY)],
        out_specs=pl.BlockSpec(memory_space=pl.ANY),
        out_shape=jax.ShapeDtypeStruct((p, d), src.dtype),
        scratch_shapes=[pltpu.VMEM((GATHER_SLOTS, rows, d), src.dtype),
                        pltpu.SemaphoreType.DMA((GATHER_SLOTS,)),
                        pltpu.SemaphoreType.DMA((GATHER_SLOTS,))],
        compiler_params=_params("arbitrary"), name="gather_rows",
    )(idx3, idx3, src)


def _combine_kernel(p0_first, p1_first, p0_next, p1_next, h_ref, gate_ref, ys_hbm, o_ref,
                    buf_ref, sem):
    i = pl.program_id(0)
    n = pl.num_programs(0)
    rows = h_ref.shape[0]
    slot = i & 1

    def start_into(s, p0_ref, p1_ref):
        def start(r, u):
            _row_copy(ys_hbm, p0_ref[0, 0, r], buf_ref.at[s, 0], r, sem.at[s]).start(priority=0)
            _row_copy(ys_hbm, p1_ref[0, 0, r], buf_ref.at[s, 1], r, sem.at[s]).start(priority=1)
        _for_rows(rows, start)

    @pl.when(i == 0)
    def _():
        start_into(0, p0_first, p1_first)

    @pl.when(i + 1 < n)
    def _():
        start_into(1 - slot, p0_next, p1_next)

    def wait(r, u):
        _row_copy(ys_hbm, 0, buf_ref.at[slot, 0], r, sem.at[slot]).wait()
        _row_copy(ys_hbm, 0, buf_ref.at[slot, 1], r, sem.at[slot]).wait()

    _for_rows(rows, wait)
    gates = gate_ref[...]
    o_ref[...] = (h_ref[...] + gates[:, 0:1] * buf_ref[slot, 0]
                  + gates[:, 1:2] * buf_ref[slot, 1])


def combine_rows(h, gates, ys, pos0, pos1, *, rows):
    t, d = h.shape
    n = t // rows
    first = lambda: pl.BlockSpec((1, 1, rows), lambda i: (0, 0, 0), memory_space=pltpu.SMEM)
    nxt = lambda: pl.BlockSpec((1, 1, rows), lambda i: (jnp.minimum(i + 1, n - 1), 0, 0),
                               memory_space=pltpu.SMEM)
    pos0 = pos0.reshape(n, 1, rows)
    pos1 = pos1.reshape(n, 1, rows)
    return pl.pallas_call(
        _combine_kernel, grid=(n,),
        in_specs=[first(), first(), nxt(), nxt(),
                  pl.BlockSpec((rows, d), lambda i: (i, 0)),
                  pl.BlockSpec((rows, LANES), lambda i: (i, 0)),
                  pl.BlockSpec(memory_space=pl.ANY)],
        out_specs=pl.BlockSpec((rows, d), lambda i: (i, 0)),
        out_shape=jax.ShapeDtypeStruct((t, d), F32),
        scratch_shapes=[pltpu.VMEM((2, 2, rows, d), F32), pltpu.SemaphoreType.DMA((2,))],
        compiler_params=_params("arbitrary"), name="combine_rows",
    )(pos0, pos1, pos0, pos1, h, gates, ys)


def _rmsnorm_kernel(x_ref, g_ref, o_ref):
    o_ref[...] = _rms(x_ref[...], g_ref[...])


def rmsnorm(x, g, *, tm):
    t, d = x.shape
    return pl.pallas_call(
        _rmsnorm_kernel, grid=(t // tm,),
        in_specs=[pl.BlockSpec((tm, d), lambda i: (i, 0)), pl.BlockSpec((1, d), lambda i: (0, 0))],
        out_specs=pl.BlockSpec((tm, d), lambda i: (i, 0)),
        out_shape=jax.ShapeDtypeStruct((t, d), F32),
        compiler_params=_params("parallel"), name="final_rmsnorm",
    )(x, g)


def _rope_cos_sin(positions, dim):
    inv_freq = 1.0 / jnp.power(ROPE_THETA, jnp.arange(0, dim, 2, dtype=F32) / dim)
    ang = positions.reshape(-1).astype(F32)[:, None] * inv_freq
    return jnp.cos(ang), jnp.sin(ang)


def _diff_tables(positions):
    cos, sin = _rope_cos_sin(positions, C_HEAD_DIM)
    cos_l = jnp.tile(cos, (1, 4))
    sin_l = jnp.concatenate([-sin, -sin, sin, sin], axis=1)
    k_tab = jnp.stack([cos_l, sin_l])
    return jnp.stack([k_tab * (C_HEAD_DIM ** -0.5 * LOG2E), k_tab])


def _diff_head_order(w, n_cols):
    d = w.shape[0]
    half = C_HEAD_DIM // 2
    head = w[:, :n_cols].reshape(d, n_cols // LANES, 2, 2, half)
    head = head.transpose(0, 1, 3, 2, 4).reshape(d, n_cols)
    return jnp.concatenate([head, w[:, n_cols:]], axis=1)


def _mla_tables(positions):
    cos, sin = _rope_cos_sin(positions, B_ROPE_DIM)
    t = cos.shape[0]
    scale = (B_NOPE_DIM + B_ROPE_DIM) ** -0.5 * LOG2E
    ones = lambda n: jnp.ones((t, n), F32)
    zeros = lambda n: jnp.zeros((t, n), F32)
    q_cos = jnp.concatenate([ones(B_NOPE_DIM), cos, cos, ones(32)], axis=1) * scale
    q_sin = jnp.concatenate([zeros(B_NOPE_DIM), -sin, sin, zeros(32)], axis=1) * scale
    k_cos = jnp.concatenate([cos, cos, ones(96)], axis=1)
    k_sin = jnp.concatenate([-sin, sin, zeros(96)], axis=1)
    return jnp.stack([q_cos, q_sin, k_cos, k_sin])


def _routing_plan(idx_out, counts, *, tm, n_tiles):
    t = idx_out.shape[0]
    e0, e1, r0, r1 = idx_out[:, 0], idx_out[:, 1], idx_out[:, 2], idx_out[:, 3]
    padded = ((counts + tm - 1) // tm) * tm
    ends = jnp.cumsum(padded)
    offs = ends - padded
    experts = jnp.arange(N_EXPERTS, dtype=I32)[None, :]
    pos0 = jnp.sum(jnp.where(e0[:, None] == experts, offs[None, :], 0), axis=1) + r0
    pos1 = jnp.sum(jnp.where(e1[:, None] == experts, offs[None, :], 0), axis=1) + r1
    tok = jnp.arange(t, dtype=I32)
    src = jnp.zeros((n_tiles * tm,), I32).at[pos0].set(tok).at[pos1].set(tok)
    n_valid = (ends[-1] // tm).astype(I32)
    tile = jnp.minimum(jnp.arange(n_tiles, dtype=I32), n_valid - 1)
    tile_expert = jnp.minimum(jnp.sum((tile * tm)[:, None] >= ends[None, :], axis=1),
                              N_EXPERTS - 1)
    return pos0.astype(I32), pos1.astype(I32), src, tile_expert.astype(I32), n_valid.reshape(1)


def _pad_cols(w, n):
    return jnp.pad(w, ((0, 0), (0, n - w.shape[1])))


def kernel(x, positions, mix_norm_g, ffn_norm_g, w_in_even, rel_bias_a, q_norm_b, w_uq_b, kv_norm_b, w_ukv_b, w_out_even, w_in_odd, lambda_q1, lambda_k1, lambda_q2, lambda_k2, subln_g, w_out_odd, w_gu_dense, w_down_dense, w_router, w_gu_moe, w_down_moe, final_norm_g):
    b, s, d = x.shape
    t = b * s
    depth = mix_norm_g.shape[0]
    a_width = 4 * LANES
    tm = 512
    tm_moe = 512
    n_tiles_moe = (2 * t) // tm_moe + N_EXPERTS

    diff_tab = _diff_tables(positions)
    mla_tab = _mla_tables(positions)
    row = lambda v: v.reshape(1, -1).astype(F32)

    h = x.reshape(t, d).astype(F32)
    for layer in range(depth):
        i = layer // 2
        if layer % 2 == 0:
            w_in = _pad_cols(w_in_even[i], 2048).astype(BF16)
            proj = rms_matmul(h, row(mix_norm_g[layer]), w_in, tm=tm, tn=1024)
            proj3 = proj.reshape(b, s, -1)
            out_a = band_attention(proj3, band_bias_tiles(rel_bias_a[i]),
                                   q_blk0=0, k_blk0=4, v_blk0=8)
            n_heads = w_uq_b.shape[2] // (B_NOPE_DIM + B_ROPE_DIM)
            wq = jnp.pad(w_uq_b[i].reshape(-1, n_heads, B_NOPE_DIM + B_ROPE_DIM),
                         ((0, 0), (0, 0), (0, 32))).reshape(-1, n_heads * LANES).astype(BF16)
            wkv = w_ukv_b[i].reshape(-1, n_heads, B_NOPE_DIM + B_V_DIM)
            wk = jnp.pad(wkv[:, :, :B_NOPE_DIM], ((0, 0), (0, 0), (0, LANES - B_NOPE_DIM)))
            wk = wk.reshape(-1, n_heads * LANES).astype(BF16)
            wv = wkv[:, :, B_NOPE_DIM:].reshape(-1, n_heads * B_V_DIM).astype(BF16)
            q_b, k_b, v_b = mla_prep(proj, row(q_norm_b[i]), row(kv_norm_b[i]), wq, wk, wv, mla_tab,
                                     tm=tm, cq_blk=6, ckv_blk=14, kr_blk=15)
            out_b = flash_attention(q_b.reshape(b, s, -1), k_b.reshape(b, s, -1),
                                    v_b.reshape(b, s, -1), mode="mla", q_blk0=0, k_blk0=0,
                                    v_blk0=0, n_groups=n_heads // 2, tq=512, groups=2)
            w_out = w_out_even[i].astype(BF16)
            h = matmul_residual([out_a.reshape(t, -1), out_b.reshape(t, -1)],
                                [w_out[:a_width], w_out[a_width:]], h, tm=tm)
            n_tiles = t // tm
            h = grouped_swiglu(h, row(ffn_norm_g[layer]), w_gu_dense[i][None].astype(BF16),
                               w_down_dense[i][None].astype(BF16), jnp.zeros((n_tiles,), I32),
                               jnp.full((1,), n_tiles, I32), tm=tm, tf=1408, add_res=True)
        else:
            c_width = w_out_odd.shape[1]
            n_heads = c_width // LANES
            w_in = _diff_head_order(w_in_odd[i], 2 * c_width).astype(BF16)
            qkv = rms_matmul(h, row(mix_norm_g[layer]), w_in, tm=tm, tn=1024,
                             rope=(diff_tab, c_width // 1024, LANES // 2))
            lam_init = 0.8 - 0.6 * math.exp(-0.3 * layer)
            lam = (jnp.exp(jnp.sum(lambda_q1[i].astype(F32) * lambda_k1[i].astype(F32)))
                   - jnp.exp(jnp.sum(lambda_q2[i].astype(F32) * lambda_k2[i].astype(F32)))
                   + lam_init).reshape(1).astype(F32)
            qkv3 = qkv.reshape(b, s, -1)
            out_c = flash_attention(qkv3, qkv3, qkv3, mode="diff", q_blk0=0, k_blk0=n_heads,
                                    v_blk0=2 * n_heads, n_groups=n_heads, tq=512, groups=2, lam=lam,
                                    subln_g=row(subln_g[i]), out_scale=1.0 - lam_init)
            h = matmul_residual([out_c.reshape(t, -1)], [w_out_odd[i].astype(BF16)], h, tm=tm)

            g_ffn = row(ffn_norm_g[layer])
            wr_pad = _pad_cols(w_router[i].astype(F32), LANES)
            gates, idx_out, cnt = router(h, g_ffn, wr_pad, tb=512)
            pos0, pos1, src, tile_expert, n_valid = _routing_plan(
                idx_out, cnt[0, :N_EXPERTS], tm=tm_moe, n_tiles=n_tiles_moe)
            xs = gather_rows(h, src, rows=256)
            ys = grouped_swiglu(xs, g_ffn, w_gu_moe[i].astype(BF16), w_down_moe[i].astype(BF16),
                                tile_expert, n_valid, tm=tm_moe, tf=1792, add_res=False)
            h = combine_rows(h, gates, ys, pos0, pos1, rows=256)

    out = rmsnorm(h, row(final_norm_g), tm=tm)
    return out.reshape(b, s, d)
```

```python
import functools
import math

import jax
import jax.numpy as jnp
import numpy as np
from jax import lax
from jax.experimental import pallas as pl
from jax.experimental.pallas import tpu as pltpu

F32 = jnp.float32
BF16 = jnp.bfloat16
I32 = jnp.int32

NORM_EPS = 1e-6
ROPE_THETA = 10000.0
NEG = -1e30
LOG2E = math.log2(math.e)
LANES = 128
CHUNK = 64

A_LEFT_CHUNKS = 8
A_MAX_REL = 256
A_HEAD_DIM = 64
B_NOPE_DIM = 64
B_ROPE_DIM = 32
B_V_DIM = 64
C_HEAD_DIM = 64
N_EXPERTS = 8

VMEM_LIMIT = 56 * 1024 * 1024


def _params(*sem):
    return pltpu.CompilerParams(dimension_semantics=sem, vmem_limit_bytes=VMEM_LIMIT)


def _rms(x, g):
    ms = jnp.mean(x * x, axis=-1, keepdims=True)
    return x * lax.rsqrt(ms + NORM_EPS) * g


def _dot(a, b):
    return jnp.dot(a, b, preferred_element_type=F32)


def _dot_nt(a, b):
    return lax.dot_general(a, b, (((1,), (1,)), ((), ())), preferred_element_type=F32)


def _rope_lanes(x, cos, sin, half):
    if 2 * half == LANES:
        swapped = pltpu.roll(x, half, 1)
    else:
        lane = lax.broadcasted_iota(I32, x.shape, 1)
        fwd = pltpu.roll(x, LANES - half, 1)
        bwd = pltpu.roll(x, half, 1)
        swapped = jnp.where((lane & half) == 0, fwd, bwd)
    return x * cos + swapped * sin


def _rms_matmul_kernel(x_ref, g_ref, w_ref, o_ref, xn_ref):
    @pl.when(pl.program_id(1) == 0)
    def _():
        xn_ref[...] = _rms(x_ref[...], g_ref[...]).astype(BF16)

    o_ref[...] = _dot(xn_ref[...], w_ref[...]).astype(o_ref.dtype)


def _rms_matmul_rope_kernel(x_ref, g_ref, w_ref, cos_ref, sin_ref, o_ref, xn_ref, *,
                            n_rope_tiles, half):
    j = pl.program_id(1)

    @pl.when(j == 0)
    def _():
        xn_ref[...] = _rms(x_ref[...], g_ref[...]).astype(BF16)

    acc = _dot(xn_ref[...], w_ref[...])

    @pl.when(j < n_rope_tiles)
    def _():
        cos = cos_ref[...]
        sin = sin_ref[...]
        for c in range(acc.shape[1] // LANES):
            sl = slice(c * LANES, (c + 1) * LANES)
            o_ref[:, sl] = _rope_lanes(acc[:, sl], cos, sin, half).astype(o_ref.dtype)

    @pl.when(j >= n_rope_tiles)
    def _():
        o_ref[...] = acc.astype(o_ref.dtype)


def rms_matmul(x, g, w, *, tm, tn, rope=None):
    t, d = x.shape
    n = w.shape[1]
    grid = (t // tm, n // tn)
    x_spec = pl.BlockSpec((tm, d), lambda i, j: (i, 0))
    g_spec = pl.BlockSpec((1, d), lambda i, j: (0, 0))
    w_spec = pl.BlockSpec((d, tn), lambda i, j: (0, j))
    o_spec = pl.BlockSpec((tm, tn), lambda i, j: (i, j))
    scratch = [pltpu.VMEM((tm, d), BF16)]
    out_shape = jax.ShapeDtypeStruct((t, n), BF16)
    if rope is None:
        return pl.pallas_call(
            _rms_matmul_kernel, grid=grid, in_specs=[x_spec, g_spec, w_spec], out_specs=o_spec,
            out_shape=out_shape, scratch_shapes=scratch,
            compiler_params=_params("parallel", "arbitrary"), name="rms_matmul",
        )(x, g, w)
    tables, per_part, half = rope
    cos_spec = pl.BlockSpec((None, None, tm, LANES),
                            lambda i, j: (jnp.minimum(j // per_part, 1), 0, i, 0))
    sin_spec = pl.BlockSpec((None, None, tm, LANES),
                            lambda i, j: (jnp.minimum(j // per_part, 1), 1, i, 0))
    kern = functools.partial(_rms_matmul_rope_kernel, n_rope_tiles=2 * per_part, half=half)
    return pl.pallas_call(
        kern, grid=grid, in_specs=[x_spec, g_spec, w_spec, cos_spec, sin_spec], out_specs=o_spec,
        out_shape=out_shape, scratch_shapes=scratch,
        compiler_params=_params("parallel", "arbitrary"), name="rms_matmul_rope",
    )(x, g, w, tables, tables)


def _mm_res_kernel(*refs, n_in):
    res_ref = refs[2 * n_in]
    o_ref = refs[2 * n_in + 1]
    acc = res_ref[...]
    for k in range(n_in):
        acc = acc + _dot(refs[k][...], refs[n_in + k][...])
    o_ref[...] = acc


def matmul_residual(a_list, w_list, res, *, tm):
    t, n = res.shape
    n_in = len(a_list)
    in_specs = [pl.BlockSpec((tm, a.shape[1]), lambda i: (i, 0)) for a in a_list]
    in_specs += [pl.BlockSpec(w.shape, lambda i: (0, 0)) for w in w_list]
    in_specs += [pl.BlockSpec((tm, n), lambda i: (i, 0))]
    return pl.pallas_call(
        functools.partial(_mm_res_kernel, n_in=n_in), grid=(t // tm,), in_specs=in_specs,
        out_specs=pl.BlockSpec((tm, n), lambda i: (i, 0)),
        out_shape=jax.ShapeDtypeStruct((t, n), F32),
        compiler_params=_params("parallel"), name="matmul_residual",
    )(*a_list, *w_list, res)


U32 = jnp.uint32
HI16 = 0xFFFF0000


def _pack_bf16_pairs(x):
    n = x.shape[1] // 2
    bits = lax.bitcast_convert_type(x.astype(BF16).astype(F32), U32)
    return (bits[:, :n] & U32(HI16)) | (bits[:, n:] >> 16)


def _unpack_bf16_pairs(w):
    first = lax.bitcast_convert_type(w & U32(HI16), F32)
    second = lax.bitcast_convert_type(w << 16, F32)
    return first, second


def _ffn_kernel(te_ref, nv_ref, x_ref, g_ref, wg_ref, wu_ref, wd_ref, o_ref, xn_ref, acc_ref, *,
                add_res):
    del te_ref
    i = pl.program_id(0)
    f = pl.program_id(1)
    nf = pl.num_programs(1)
    valid = i < nv_ref[0]

    @pl.when(valid & (f == 0))
    def _():
        xn_ref[...] = _rms(x_ref[...], g_ref[...]).astype(BF16)
        acc_ref[...] = jnp.zeros_like(acc_ref)

    _ffn_body(valid, wg_ref, wu_ref, wd_ref, xn_ref, acc_ref)

    @pl.when(valid & (f == nf - 1))
    def _():
        if add_res:
            o_ref[...] = x_ref[...] + acc_ref[...]
        else:
            o_ref[...] = acc_ref[...]

    @pl.when(jnp.logical_not(valid) & (f == nf - 1))
    def _():
        o_ref[...] = jnp.zeros_like(o_ref)


def _ffn_body(valid, wg_ref, wu_ref, wd_ref, xn_ref, acc_ref):
    @pl.when(valid)
    def _():
        xn = xn_ref[...]
        gate = _dot(xn, wg_ref[...])
        up = _dot(xn, wu_ref[...])
        act = (gate * jax.nn.sigmoid(gate) * up).astype(BF16)
        acc_ref[...] += _dot(act, wd_ref[...])


def _ffn_packed_kernel(te_ref, nv_ref, x_ref, wg_ref, wu_ref, wd_ref, o_ref, xn_ref, acc_ref):
    del te_ref
    i = pl.program_id(0)
    f = pl.program_id(1)
    nf = pl.num_programs(1)
    valid = i < nv_ref[0]
    half = x_ref.shape[1]

    @pl.when(valid & (f == 0))
    def _():
        first, second = _unpack_bf16_pairs(x_ref[...])
        xn_ref[:, :half] = first.astype(BF16)
        xn_ref[:, half:] = second.astype(BF16)
        acc_ref[...] = jnp.zeros_like(acc_ref)

    _ffn_body(valid, wg_ref, wu_ref, wd_ref, xn_ref, acc_ref)

    @pl.when(valid & (f == nf - 1))
    def _():
        o_ref[...] = _pack_bf16_pairs(acc_ref[...])

    @pl.when(jnp.logical_not(valid) & (f == nf - 1))
    def _():
        o_ref[...] = jnp.zeros_like(o_ref)


def grouped_swiglu(x, g, w_gu, w_down, tile_expert, n_valid, *, tm, tf, add_res=False):
    packed = g is None
    assert not (packed and add_res)
    p = x.shape[0]
    d = w_gu.shape[1]
    ff = w_down.shape[1]
    nf = ff // tf
    grid = (p // tm, nf)

    def f_eff(i, f, nv):
        return jnp.where(i < nv[0], f, nf - 1)

    x_spec = pl.BlockSpec((tm, x.shape[1]), lambda i, f, te, nv: (i, 0))
    w_specs = [
        pl.BlockSpec((None, d, tf), lambda i, f, te, nv: (te[i], 0, f_eff(i, f, nv))),
        pl.BlockSpec((None, d, tf), lambda i, f, te, nv: (te[i], 0, nf + f_eff(i, f, nv))),
        pl.BlockSpec((None, tf, d), lambda i, f, te, nv: (te[i], f_eff(i, f, nv), 0)),
    ]
    if packed:
        kern, in_specs, operands = _ffn_packed_kernel, [x_spec] + w_specs, (x,)
    else:
        kern = functools.partial(_ffn_kernel, add_res=add_res)
        in_specs = [x_spec, pl.BlockSpec((1, d), lambda i, f, te, nv: (0, 0))] + w_specs
        operands = (x, g)
    grid_spec = pltpu.PrefetchScalarGridSpec(
        num_scalar_prefetch=2, grid=grid, in_specs=in_specs,
        out_specs=pl.BlockSpec((tm, x.shape[1]), lambda i, f, te, nv: (i, 0)),
        scratch_shapes=[pltpu.VMEM((tm, d), BF16), pltpu.VMEM((tm, d), F32)],
    )
    return pl.pallas_call(
        kern, grid_spec=grid_spec, out_shape=jax.ShapeDtypeStruct(x.shape, x.dtype),
        compiler_params=_params("parallel", "arbitrary"), name="grouped_swiglu",
    )(tile_expert, n_valid, *operands, w_gu, w_gu, w_down)


def _flash_kernel(lam_ref, q_ref, k_ref, v_ref, g_ref, o_ref, m_ref, l_ref, acc_ref, vt_ref,
                  s_ref, *, mode, tq, groups, out_scale):
    i = pl.program_id(2)
    s_len = v_ref.shape[0]
    wqk = q_ref.shape[1] // groups
    n_maps = 2 * groups
    sub = lax.broadcasted_iota(I32, (LANES, tq), 0)
    top = sub < (LANES // 2)

    @pl.when(i == 0)
    def _():
        for c in range(s_len // tq):
            sl = slice(c * tq, (c + 1) * tq)
            vt_ref[:, sl] = v_ref[sl, :].T

    qts = []
    for g in range(groups):
        if mode == "mla":
            qts += [q_ref[:, g * wqk:g * wqk + LANES].T, q_ref[:, g * wqk + LANES:(g + 1) * wqk].T]
        else:
            qt = q_ref[:, g * wqk:(g + 1) * wqk].T
            zero = jnp.zeros_like(qt)
            map1 = (sub & (C_HEAD_DIM // 2)) == 0
            qts += [jnp.where(map1, qt, zero), jnp.where(map1, zero, qt)]

    m_ref[...] = jnp.full_like(m_ref, NEG)
    l_ref[...] = jnp.zeros_like(l_ref)
    acc_ref[...] = jnp.zeros_like(acc_ref)

    def k_tile(mi, start):
        g, idx = divmod(mi, 2)
        if mode == "mla":
            return k_ref[pl.ds(start, tq), g * wqk + idx * LANES:g * wqk + (idx + 1) * LANES]
        return k_ref[pl.ds(start, tq), g * wqk:(g + 1) * wqk]

    def update(mi, s, vt_tile):
        m_prev = m_ref[mi]
        m_new = jnp.maximum(m_prev, jnp.max(s, axis=0, keepdims=True))
        alpha = jnp.exp2(m_prev - m_new)
        p = jnp.exp2(s - m_new)
        l_ref[mi] = alpha * l_ref[mi] + jnp.sum(p, axis=0, keepdims=True)
        acc_ref[mi] = alpha * acc_ref[mi] + _dot(vt_tile, p.astype(BF16))
        m_ref[mi] = m_new

    def scores_into(slot, t):
        start = pl.multiple_of(t * tq, tq)
        for mi in range(n_maps):
            s_ref[slot, mi] = _dot(k_tile(mi, start), qts[mi])

    def consume(slot, t, masked):
        start = pl.multiple_of(t * tq, tq)
        for mi in range(n_maps):
            g = mi // 2
            s = s_ref[slot, mi]
            if masked:
                row = lax.broadcasted_iota(I32, (tq, tq), 0)
                col = lax.broadcasted_iota(I32, (tq, tq), 1)
                shift = CHUNK.bit_length() - 1
                s = jnp.where((row >> shift) <= (col >> shift), s, NEG)
            update(mi, s, vt_ref[g * LANES:(g + 1) * LANES, pl.ds(start, tq)])

    scores_into(0, 0)

    def tile_pair(u, carry):
        t = 2 * u
        scores_into(1, t + 1)
        consume(0, t, False)
        scores_into(0, t + 2)
        consume(1, t + 1, False)
        return carry

    lax.fori_loop(0, i >> 1, tile_pair, 0)

    @pl.when((i & 1) == 1)
    def _():
        scores_into(1, i)
        consume(0, i - 1, False)
        consume(1, i, True)

    @pl.when((i & 1) == 0)
    def _():
        consume(0, i, True)

    for g in range(groups):
        o0 = acc_ref[2 * g] * (1.0 / l_ref[2 * g])
        o1 = acc_ref[2 * g + 1] * (1.0 / l_ref[2 * g + 1])
        cols = slice(g * LANES, (g + 1) * LANES)
        if mode == "mla":
            o_ref[:, cols] = jnp.where(top, o0, o1).T.astype(o_ref.dtype)
        else:
            o = (o0 - lam_ref[0] * o1).T
            o_ref[:, cols] = (_rms(o, g_ref[...]) * out_scale).astype(o_ref.dtype)


def flash_attention(q_arr, k_arr, v_arr, *, mode, q_blk0, k_blk0, v_blk0, n_groups, tq,
                    groups=1, lam=None, subln_g=None, out_scale=1.0):
    b, s, _ = q_arr.shape
    wqk = (2 * LANES if mode == "mla" else LANES) * groups
    wv = LANES * groups
    assert n_groups % groups == 0 and q_blk0 % groups == 0 and k_blk0 % groups == 0
    assert v_blk0 % groups == 0
    if lam is None:
        lam = jnp.zeros((1,), F32)
    if subln_g is None:
        subln_g = jnp.ones((1, LANES), F32)
    grid = (b, n_groups // groups, s // tq)
    qb, kb, vb = q_blk0 // groups, k_blk0 // groups, v_blk0 // groups
    in_specs = [
        pl.BlockSpec((None, tq, wqk), lambda bb, g, i, lam_r: (bb, i, qb + g)),
        pl.BlockSpec((None, s, wqk), lambda bb, g, i, lam_r: (bb, 0, kb + g)),
        pl.BlockSpec((None, s, wv), lambda bb, g, i, lam_r: (bb, 0, vb + g)),
        pl.BlockSpec((1, LANES), lambda bb, g, i, lam_r: (0, 0)),
    ]
    n_maps = 2 * groups
    grid_spec = pltpu.PrefetchScalarGridSpec(
        num_scalar_prefetch=1, grid=grid, in_specs=in_specs,
        out_specs=pl.BlockSpec((None, tq, wv), lambda bb, g, i, lam_r: (bb, i, g)),
        scratch_shapes=[pltpu.VMEM((n_maps, 1, tq), F32), pltpu.VMEM((n_maps, 1, tq), F32),
                        pltpu.VMEM((n_maps, LANES, tq), F32), pltpu.VMEM((wv, s), BF16),
                        pltpu.VMEM((2, n_maps, tq, tq), F32)],
    )
    kern = functools.partial(_flash_kernel, mode=mode, tq=tq, groups=groups, out_scale=out_scale)
    return pl.pallas_call(
        kern, grid_spec=grid_spec,
        out_shape=jax.ShapeDtypeStruct((b, s, n_groups * LANES), BF16),
        compiler_params=_params("parallel", "parallel", "arbitrary"), name="flash_" + mode,
    )(lam, q_arr, k_arr, v_arr, subln_g)


BAND_TQ = 256
BAND_TILES = 3
BAND_PAIRS = 2


def _band_kernel(q_ref, k0_ref, k1_ref, k2_ref, v0_ref, v1_ref, v2_ref, bias_ref, o_ref, *, scale):
    i = pl.program_id(2)
    lane = lax.broadcasted_iota(I32, (BAND_TQ, LANES), 1)
    low = lane < (LANES // 2)
    k_refs = (k0_ref, k1_ref, k2_ref)
    v_refs = (v0_ref, v1_ref, v2_ref)
    for pp in range(BAND_PAIRS):
        cols = slice(pp * LANES, (pp + 1) * LANES)
        q = q_ref[:, cols].astype(F32) * scale
        qs = (jnp.where(low, q, 0.0).astype(BF16), jnp.where(low, 0.0, q).astype(BF16))
        outs = []
        for h in range(2):
            scores = []
            for j in range(BAND_TILES):
                pen = jnp.where(i + j >= BAND_TILES - 1, 0.0, NEG).astype(F32)
                scores.append(_dot_nt(qs[h], k_refs[j][:, cols]) + bias_ref[pp, h, j] + pen)
            m = jnp.maximum(jnp.maximum(jnp.max(scores[0], axis=-1, keepdims=True),
                                        jnp.max(scores[1], axis=-1, keepdims=True)),
                            jnp.max(scores[2], axis=-1, keepdims=True))
            l = jnp.zeros_like(m)
            o = jnp.zeros((BAND_TQ, LANES), F32)
            for j in range(BAND_TILES):
                p = jnp.exp(scores[j] - m)
                l = l + jnp.sum(p, axis=-1, keepdims=True)
                o = o + _dot(p.astype(BF16), v_refs[j][:, cols])
            outs.append(o * (1.0 / l))
        o_ref[:, cols] = jnp.where(low, outs[0], outs[1]).astype(o_ref.dtype)


def band_bias_tiles(rel_bias):
    h = rel_bias.shape[0]
    n = BAND_TQ
    r = np.arange(n)[:, None]
    c = np.arange(n)[None, :]
    e_of = np.zeros(2 * n, np.int64)
    e_of[:n] = -np.arange(n)
    e_of[n + 1:] = n - 1 - np.arange(n - 1)
    tiles = []
    for j in range(BAND_TILES):
        rel = (BAND_TILES - 1 - j) * n + e_of
        gen = rel_bias[:, np.clip(rel, -A_MAX_REL, A_MAX_REL) + A_MAX_REL].astype(F32)
        skew = jnp.tile(gen, (1, n))[:, :n * (2 * n - 1)].reshape(h, n, 2 * n - 1)[:, :, :n]
        dist = (r // CHUNK) - (c // CHUNK) + (BAND_TILES - 1 - j) * (n // CHUNK)
        ok = (dist >= 0) & (dist <= A_LEFT_CHUNKS)
        tiles.append(jnp.where(ok[None], skew, NEG))
    t = jnp.stack(tiles, axis=1)
    return t.reshape(h // 2, 2, BAND_TILES, BAND_TQ, BAND_TQ)


def band_attention(proj, bias_tiles, *, q_blk0, k_blk0, v_blk0):
    b, s, _ = proj.shape
    n_pairs = bias_tiles.shape[0]
    width = BAND_PAIRS * LANES
    assert n_pairs % BAND_PAIRS == 0 and q_blk0 % BAND_PAIRS == 0
    assert k_blk0 % BAND_PAIRS == 0 and v_blk0 % BAND_PAIRS == 0
    grid = (n_pairs // BAND_PAIRS, b, s // BAND_TQ)

    def kv_spec(blk0, j):
        return pl.BlockSpec(
            (None, BAND_TQ, width),
            lambda p, bb, i: (bb, jnp.maximum(i + j - (BAND_TILES - 1), 0),
                              blk0 // BAND_PAIRS + p))

    in_specs = [pl.BlockSpec((None, BAND_TQ, width),
                             lambda p, bb, i: (bb, i, q_blk0 // BAND_PAIRS + p))]
    in_specs += [kv_spec(k_blk0, j) for j in range(BAND_TILES)]
    in_specs += [kv_spec(v_blk0, j) for j in range(BAND_TILES)]
    in_specs += [pl.BlockSpec((BAND_PAIRS, 2, BAND_TILES, BAND_TQ, BAND_TQ),
                              lambda p, bb, i: (p, 0, 0, 0, 0))]
    return pl.pallas_call(
        functools.partial(_band_kernel, scale=A_HEAD_DIM ** -0.5), grid=grid, in_specs=in_specs,
        out_specs=pl.BlockSpec((None, BAND_TQ, width), lambda p, bb, i: (bb, i, p)),
        out_shape=jax.ShapeDtypeStruct((b, s, n_pairs * LANES), BF16),
        compiler_params=_params("parallel", "parallel", "parallel"), name="band_attention",
    )(proj, proj, proj, proj, proj, proj, proj, bias_tiles)


def _mla_prep_kernel(cq_ref, ckv_ref, kr_ref, gq_ref, gkv_ref, wq_ref, wk_ref, wv_ref,
                     cq_cos_ref, cq_sin_ref, ck_cos_ref, ck_sin_ref, q_ref, k_ref, v_ref):
    half = B_ROPE_DIM // 2
    nq = _rms(cq_ref[...].astype(F32), gq_ref[...]).astype(BF16)
    nkv = _rms(ckv_ref[...].astype(F32), gkv_ref[...]).astype(BF16)
    q = _dot(nq, wq_ref[...])
    k = _dot(nkv, wk_ref[...])
    v_ref[...] = _dot(nkv, wv_ref[...]).astype(v_ref.dtype)
    kr = _rope_lanes(kr_ref[...].astype(F32), ck_cos_ref[...], ck_sin_ref[...], half)
    lane = lax.broadcasted_iota(I32, kr.shape, 1)
    kr = jnp.where(lane < B_ROPE_DIM, kr, 0.0)
    kr = pltpu.roll(kr, B_NOPE_DIM, 1)
    cos = cq_cos_ref[...]
    sin = cq_sin_ref[...]
    for c in range(q.shape[1] // LANES):
        sl = slice(c * LANES, (c + 1) * LANES)
        q_ref[:, sl] = _rope_lanes(q[:, sl], cos, sin, half).astype(q_ref.dtype)
        k_ref[:, sl] = (k[:, sl] + kr).astype(k_ref.dtype)


def mla_prep(proj, gq, gkv, wq, wk, wv, tables, *, tm, cq_blk, ckv_blk, kr_blk):
    t = proj.shape[0]
    nq = wq.shape[0]
    nkv = wk.shape[0]
    whole = lambda a: pl.BlockSpec(a.shape, lambda i: (0,) * a.ndim)
    tab = lambda k: pl.BlockSpec((None, tm, LANES), lambda i: (k, i, 0))
    in_specs = [
        pl.BlockSpec((tm, nq), lambda i: (i, cq_blk)),
        pl.BlockSpec((tm, nkv), lambda i: (i, ckv_blk)),
        pl.BlockSpec((tm, LANES), lambda i: (i, kr_blk)),
        whole(gq), whole(gkv), whole(wq), whole(wk), whole(wv),
        tab(0), tab(1), tab(2), tab(3),
    ]
    out_specs = [pl.BlockSpec((tm, wq.shape[1]), lambda i: (i, 0)),
                 pl.BlockSpec((tm, wk.shape[1]), lambda i: (i, 0)),
                 pl.BlockSpec((tm, wv.shape[1]), lambda i: (i, 0))]
    out_shape = [jax.ShapeDtypeStruct((t, wq.shape[1]), BF16),
                 jax.ShapeDtypeStruct((t, wk.shape[1]), BF16),
                 jax.ShapeDtypeStruct((t, wv.shape[1]), BF16)]
    return pl.pallas_call(
        _mla_prep_kernel, grid=(t // tm,), in_specs=in_specs, out_specs=out_specs,
        out_shape=out_shape, compiler_params=_params("parallel"), name="mla_prep",
    )(proj, proj, proj, gq, gkv, wq, wk, wv, tables, tables, tables, tables)


def _router_kernel(h_ref, g_ref, wr_ref, gate_ref, idx_ref, cnt_ref, npk_ref, carry_ref):
    tb = h_ref.shape[0]

    @pl.when(pl.program_id(0) == 0)
    def _():
        carry_ref[...] = jnp.zeros_like(carry_ref)

    n = _rms(h_ref[...], g_ref[...])
    npk_ref[...] = _pack_bf16_pairs(n)
    n_hi = n.astype(BF16)
    n_lo = (n - n_hi.astype(F32)).astype(BF16)
    logits = _dot(n_hi, wr_ref[0]) + (_dot(n_hi, wr_ref[1]) + _dot(n_lo, wr_ref[0]))
    lane = lax.broadcasted_iota(I32, (tb, LANES), 1).astype(F32)
    neg_inf = jnp.float32(-jnp.inf)
    lg = jnp.where(lane < N_EXPERTS, logits, neg_inf)
    m1 = jnp.max(lg, axis=-1, keepdims=True)
    e0 = jnp.min(jnp.where(lg == m1, lane, float(LANES)), axis=-1, keepdims=True)
    lg2 = jnp.where(lane == e0, neg_inf, lg)
    m2 = jnp.max(lg2, axis=-1, keepdims=True)
    e1 = jnp.min(jnp.where(lg2 == m2, lane, float(LANES)), axis=-1, keepdims=True)
    t = jnp.exp(m2 - m1)
    g0 = 1.0 / (1.0 + t)
    g1 = t / (1.0 + t)

    sel = ((lane == e0) | (lane == e1)).astype(F32)
    row = lax.broadcasted_iota(I32, (tb, tb), 0)
    col = lax.broadcasted_iota(I32, (tb, tb), 1)
    strict_lower = (col < row).astype(BF16)
    carry = carry_ref[...]
    excl = _dot(strict_lower, sel.astype(BF16)) + carry
    r0 = jnp.sum(jnp.where(lane == e0, excl, 0.0), axis=-1, keepdims=True)
    r1 = jnp.sum(jnp.where(lane == e1, excl, 0.0), axis=-1, keepdims=True)
    carry = carry + jnp.sum(sel, axis=0, keepdims=True)
    carry_ref[...] = carry

    gate_ref[...] = jnp.where(lane == 0, g0, jnp.where(lane == 1, g1, 0.0))
    idx_ref[...] = jnp.where(
        lane == 0, e0, jnp.where(lane == 1, e1, jnp.where(
            lane == 2, r0, jnp.where(lane == 3, r1, 0.0)))).astype(I32)
    cnt_ref[...] = jnp.broadcast_to(carry, cnt_ref.shape).astype(I32)


def router(h, g, wr_split, *, tb):
    t, d = h.shape
    return pl.pallas_call(
        _router_kernel, grid=(t // tb,),
        in_specs=[pl.BlockSpec((tb, d), lambda i: (i, 0)),
                  pl.BlockSpec((1, d), lambda i: (0, 0)),
                  pl.BlockSpec((2, d, LANES), lambda i: (0, 0, 0))],
        out_specs=[pl.BlockSpec((tb, LANES), lambda i: (i, 0)),
                   pl.BlockSpec((tb, LANES), lambda i: (i, 0)),
                   pl.BlockSpec((8, LANES), lambda i: (0, 0)),
                   pl.BlockSpec((tb, d // 2), lambda i: (i, 0))],
        out_shape=[jax.ShapeDtypeStruct((t, LANES), F32),
                   jax.ShapeDtypeStruct((t, LANES), I32),
                   jax.ShapeDtypeStruct((8, LANES), I32),
                   jax.ShapeDtypeStruct((t, d // 2), U32)],
        scratch_shapes=[pltpu.VMEM((1, LANES), F32)],
        compiler_params=_params("arbitrary"), name="router",
    )(h, g, wr_split)


def _row_copy(src_hbm, row, dst_ref, r, sem):
    return pltpu.make_async_copy(src_hbm.at[pl.ds(row, 1), :], dst_ref.at[pl.ds(r, 1), :], sem)


ROW_UNROLL = 8


def _for_rows(rows, body):
    def outer(o, carry):
        for u in range(ROW_UNROLL):
            body(o * ROW_UNROLL + u, u)
        return carry

    lax.fori_loop(0, rows // ROW_UNROLL, outer, 0)


GATHER_SLOTS = 3


def _gather_kernel(idx_first, idx_next, src_hbm, o_hbm, buf_ref, row_sem, blk_sem):
    i = pl.program_id(0)
    n = pl.num_programs(0)
    rows = idx_first.shape[-1]
    slot = lax.rem(i, GATHER_SLOTS)
    nxt = lax.rem(i + 1, GATHER_SLOTS)

    def start_rows(s, idx_ref):
        def start(r, u):
            _row_copy(src_hbm, idx_ref[0, 0, r], buf_ref.at[s], r, row_sem.at[s]).start(priority=u % 2)
        _for_rows(rows, start)

    def block_write(s, step):
        return pltpu.make_async_copy(buf_ref.at[s], o_hbm.at[pl.ds(step * rows, rows), :],
                                     blk_sem.at[s])

    @pl.when(i == 0)
    def _():
        start_rows(0, idx_first)

    @pl.when(i >= 2)
    def _():
        block_write(nxt, i - 2).wait()

    @pl.when(i + 1 < n)
    def _():
        start_rows(nxt, idx_next)

    _for_rows(rows, lambda r, u: _row_copy(src_hbm, 0, buf_ref.at[slot], r, row_sem.at[slot]).wait())
    block_write(slot, i).start()

    @pl.when(i == n - 1)
    def _():
        @pl.when(i >= 1)
        def _():
            block_write(lax.rem(i + 2, GATHER_SLOTS), i - 1).wait()
        block_write(slot, i).wait()


def gather_rows(src, idx, *, rows):
    p = idx.shape[0]
    d = src.shape[1]
    n = p // rows
    idx3 = idx.reshape(n, 1, rows)
    return pl.pallas_call(
        _gather_kernel, grid=(n,),
        in_specs=[pl.BlockSpec((1, 1, rows), lambda i: (0, 0, 0), memory_space=pltpu.SMEM),
                  pl.BlockSpec((1, 1, rows), lambda i: (jnp.minimum(i + 1, n - 1), 0, 0),
                               memory_space=pltpu.SMEM),
                  pl.BlockSpec(memory_space=pl.ANY)],
        out_specs=pl.BlockSpec(memory_space=pl.ANY),
        out_shape=jax.ShapeDtypeStruct((p, d), src.dtype),
        scratch_shapes=[pltpu.VMEM((GATHER_SLOTS, rows, d), src.dtype),
                        pltpu.SemaphoreType.DMA((GATHER_SLOTS,)),
                        pltpu.SemaphoreType.DMA((GATHER_SLOTS,))],
        compiler_params=_params("arbitrary"), name="gather_rows",
    )(idx3, idx3, src)


def _combine_kernel(p0_first, p1_first, p0_next, p1_next, h_ref, gate_ref, ys_hbm, o_ref,
                    buf_ref, sem):
    i = pl.program_id(0)
    n = pl.num_programs(0)
    rows = h_ref.shape[0]
    slot = i & 1

    def start_into(s, p0_ref, p1_ref):
        def start(r, u):
            _row_copy(ys_hbm, p0_ref[0, 0, r], buf_ref.at[s, 0], r, sem.at[s]).start(priority=0)
            _row_copy(ys_hbm, p1_ref[0, 0, r], buf_ref.at[s, 1], r, sem.at[s]).start(priority=1)
        _for_rows(rows, start)

    @pl.when(i == 0)
    def _():
        start_into(0, p0_first, p1_first)

    @pl.when(i + 1 < n)
    def _():
        start_into(1 - slot, p0_next, p1_next)

    def wait(r, u):
        _row_copy(ys_hbm, 0, buf_ref.at[slot, 0], r, sem.at[slot]).wait()
        _row_copy(ys_hbm, 0, buf_ref.at[slot, 1], r, sem.at[slot]).wait()

    _for_rows(rows, wait)
    gates = gate_ref[...]
    half = buf_ref.shape[-1]
    a0, b0 = _unpack_bf16_pairs(buf_ref[slot, 0])
    a1, b1 = _unpack_bf16_pairs(buf_ref[slot, 1])
    o_ref[:, :half] = h_ref[:, :half] + gates[:, 0:1] * a0 + gates[:, 1:2] * a1
    o_ref[:, half:] = h_ref[:, half:] + gates[:, 0:1] * b0 + gates[:, 1:2] * b1


def combine_rows(h, gates, ys, pos0, pos1, *, rows):
    t, d = h.shape
    n = t // rows
    first = lambda: pl.BlockSpec((1, 1, rows), lambda i: (0, 0, 0), memory_space=pltpu.SMEM)
    nxt = lambda: pl.BlockSpec((1, 1, rows), lambda i: (jnp.minimum(i + 1, n - 1), 0, 0),
                               memory_space=pltpu.SMEM)
    pos0 = pos0.reshape(n, 1, rows)
    pos1 = pos1.reshape(n, 1, rows)
    return pl.pallas_call(
        _combine_kernel, grid=(n,),
        in_specs=[first(), first(), nxt(), nxt(),
                  pl.BlockSpec((rows, d), lambda i: (i, 0)),
                  pl.BlockSpec((rows, LANES), lambda i: (i, 0)),
                  pl.BlockSpec(memory_space=pl.ANY)],
        out_specs=pl.BlockSpec((rows, d), lambda i: (i, 0)),
        out_shape=jax.ShapeDtypeStruct((t, d), F32),
        scratch_shapes=[pltpu.VMEM((2, 2, rows, ys.shape[1]), ys.dtype),
                        pltpu.SemaphoreType.DMA((2,))],
        compiler_params=_params("arbitrary"), name="combine_rows",
    )(pos0, pos1, pos0, pos1, h, gates, ys)


def _rmsnorm_kernel(x_ref, g_ref, o_ref):
    o_ref[...] = _rms(x_ref[...], g_ref[...])


def rmsnorm(x, g, *, tm):
    t, d = x.shape
    return pl.pallas_call(
        _rmsnorm_kernel, grid=(t // tm,),
        in_specs=[pl.BlockSpec((tm, d), lambda i: (i, 0)), pl.BlockSpec((1, d), lambda i: (0, 0))],
        out_specs=pl.BlockSpec((tm, d), lambda i: (i, 0)),
        out_shape=jax.ShapeDtypeStruct((t, d), F32),
        compiler_params=_params("parallel"), name="final_rmsnorm",
    )(x, g)


def _rope_cos_sin(positions, dim):
    inv_freq = 1.0 / jnp.power(ROPE_THETA, jnp.arange(0, dim, 2, dtype=F32) / dim)
    ang = positions.reshape(-1).astype(F32)[:, None] * inv_freq
    return jnp.cos(ang), jnp.sin(ang)


def _diff_tables(positions):
    cos, sin = _rope_cos_sin(positions, C_HEAD_DIM)
    cos_l = jnp.tile(cos, (1, 4))
    sin_l = jnp.concatenate([-sin, -sin, sin, sin], axis=1)
    k_tab = jnp.stack([cos_l, sin_l])
    return jnp.stack([k_tab * (C_HEAD_DIM ** -0.5 * LOG2E), k_tab])


def _diff_head_order(w, n_cols):
    d = w.shape[0]
    half = C_HEAD_DIM // 2
    head = w[:, :n_cols].reshape(d, n_cols // LANES, 2, 2, half)
    head = head.transpose(0, 1, 3, 2, 4).reshape(d, n_cols)
    return jnp.concatenate([head, w[:, n_cols:]], axis=1)


def _mla_tables(positions):
    cos, sin = _rope_cos_sin(positions, B_ROPE_DIM)
    t = cos.shape[0]
    scale = (B_NOPE_DIM + B_ROPE_DIM) ** -0.5 * LOG2E
    ones = lambda n: jnp.ones((t, n), F32)
    zeros = lambda n: jnp.zeros((t, n), F32)
    q_cos = jnp.concatenate([ones(B_NOPE_DIM), cos, cos, ones(32)], axis=1) * scale
    q_sin = jnp.concatenate([zeros(B_NOPE_DIM), -sin, sin, zeros(32)], axis=1) * scale
    k_cos = jnp.concatenate([cos, cos, ones(96)], axis=1)
    k_sin = jnp.concatenate([-sin, sin, zeros(96)], axis=1)
    return jnp.stack([q_cos, q_sin, k_cos, k_sin])


def _routing_plan(idx_out, counts, *, tm, n_tiles):
    t = idx_out.shape[0]
    e0, e1, r0, r1 = idx_out[:, 0], idx_out[:, 1], idx_out[:, 2], idx_out[:, 3]
    padded = ((counts + tm - 1) // tm) * tm
    ends = jnp.cumsum(padded)
    offs = ends - padded
    experts = jnp.arange(N_EXPERTS, dtype=I32)[None, :]
    pos0 = jnp.sum(jnp.where(e0[:, None] == experts, offs[None, :], 0), axis=1) + r0
    pos1 = jnp.sum(jnp.where(e1[:, None] == experts, offs[None, :], 0), axis=1) + r1
    tok = jnp.arange(t, dtype=I32)
    src = jnp.zeros((n_tiles * tm,), I32).at[pos0].set(tok).at[pos1].set(tok)
    n_valid = (ends[-1] // tm).astype(I32)
    tile = jnp.minimum(jnp.arange(n_tiles, dtype=I32), n_valid - 1)
    tile_expert = jnp.minimum(jnp.sum((tile * tm)[:, None] >= ends[None, :], axis=1),
                              N_EXPERTS - 1)
    return pos0.astype(I32), pos1.astype(I32), src, tile_expert.astype(I32), n_valid.reshape(1)


def _pad_cols(w, n):
    return jnp.pad(w, ((0, 0), (0, n - w.shape[1])))


def kernel(x, positions, mix_norm_g, ffn_norm_g, w_in_even, rel_bias_a, q_norm_b, w_uq_b, kv_norm_b, w_ukv_b, w_out_even, w_in_odd, lambda_q1, lambda_k1, lambda_q2, lambda_k2, subln_g, w_out_odd, w_gu_dense, w_down_dense, w_router, w_gu_moe, w_down_moe, final_norm_g):
    b, s, d = x.shape
    t = b * s
    depth = mix_norm_g.shape[0]
    a_width = 4 * LANES
    tm = 512
    tm_moe = 512
    n_tiles_moe = (2 * t) // tm_moe + N_EXPERTS

    diff_tab = _diff_tables(positions)
    mla_tab = _mla_tables(positions)
    row = lambda v: v.reshape(1, -1).astype(F32)

    h = x.reshape(t, d).astype(F32)
    for layer in range(depth):
        i = layer // 2
        if layer % 2 == 0:
            w_in = _pad_cols(w_in_even[i], 2048).astype(BF16)
            proj = rms_matmul(h, row(mix_norm_g[layer]), w_in, tm=tm, tn=1024)
            proj3 = proj.reshape(b, s, -1)
            out_a = band_attention(proj3, band_bias_tiles(rel_bias_a[i]),
                                   q_blk0=0, k_blk0=4, v_blk0=8)
            n_heads = w_uq_b.shape[2] // (B_NOPE_DIM + B_ROPE_DIM)
            wq = jnp.pad(w_uq_b[i].reshape(-1, n_heads, B_NOPE_DIM + B_ROPE_DIM),
                         ((0, 0), (0, 0), (0, 32))).reshape(-1, n_heads * LANES).astype(BF16)
            wkv = w_ukv_b[i].reshape(-1, n_heads, B_NOPE_DIM + B_V_DIM)
            wk = jnp.pad(wkv[:, :, :B_NOPE_DIM], ((0, 0), (0, 0), (0, LANES - B_NOPE_DIM)))
            wk = wk.reshape(-1, n_heads * LANES).astype(BF16)
            wv = wkv[:, :, B_NOPE_DIM:].reshape(-1, n_heads * B_V_DIM).astype(BF16)
            q_b, k_b, v_b = mla_prep(proj, row(q_norm_b[i]), row(kv_norm_b[i]), wq, wk, wv, mla_tab,
                                     tm=tm, cq_blk=6, ckv_blk=14, kr_blk=15)
            out_b = flash_attention(q_b.reshape(b, s, -1), k_b.reshape(b, s, -1),
                                    v_b.reshape(b, s, -1), mode="mla", q_blk0=0, k_blk0=0,
                                    v_blk0=0, n_groups=n_heads // 2, tq=512, groups=2)
            w_out = w_out_even[i].astype(BF16)
            h = matmul_residual([out_a.reshape(t, -1), out_b.reshape(t, -1)],
                                [w_out[:a_width], w_out[a_width:]], h, tm=tm)
            n_tiles = t // tm
            h = grouped_swiglu(h, row(ffn_norm_g[layer]), w_gu_dense[i][None].astype(BF16),
                               w_down_dense[i][None].astype(BF16), jnp.zeros((n_tiles,), I32),
                               jnp.full((1,), n_tiles, I32), tm=tm, tf=1408, add_res=True)
        else:
            c_width = w_out_odd.shape[1]
            n_heads = c_width // LANES
            w_in = _diff_head_order(w_in_odd[i], 2 * c_width).astype(BF16)
            qkv = rms_matmul(h, row(mix_norm_g[layer]), w_in, tm=tm, tn=1024,
                             rope=(diff_tab, c_width // 1024, LANES // 2))
            lam_init = 0.8 - 0.6 * math.exp(-0.3 * layer)
            lam = (jnp.exp(jnp.sum(lambda_q1[i].astype(F32) * lambda_k1[i].astype(F32)))
                   - jnp.exp(jnp.sum(lambda_q2[i].astype(F32) * lambda_k2[i].astype(F32)))
                   + lam_init).reshape(1).astype(F32)
            qkv3 = qkv.reshape(b, s, -1)
            out_c = flash_attention(qkv3, qkv3, qkv3, mode="diff", q_blk0=0, k_blk0=n_heads,
                                    v_blk0=2 * n_heads, n_groups=n_heads, tq=512, groups=2, lam=lam,
                                    subln_g=row(subln_g[i]), out_scale=1.0 - lam_init)
            h = matmul_residual([out_c.reshape(t, -1)], [w_out_odd[i].astype(BF16)], h, tm=tm)

            g_ffn = row(ffn_norm_g[layer])
            wr = _pad_cols(w_router[i].astype(F32), LANES)
            wr_hi = wr.astype(BF16)
            wr_split = jnp.stack([wr_hi, (wr - wr_hi.astype(F32)).astype(BF16)])
            gates, idx_out, cnt, n_packed = router(h, g_ffn, wr_split, tb=512)
            pos0, pos1, src, tile_expert, n_valid = _routing_plan(
                idx_out, cnt[0, :N_EXPERTS], tm=tm_moe, n_tiles=n_tiles_moe)
            xs = gather_rows(n_packed, src, rows=512)
            ys = grouped_swiglu(xs, None, w_gu_moe[i].astype(BF16), w_down_moe[i].astype(BF16),
                                tile_expert, n_valid, tm=tm_moe, tf=1792)
            h = combine_rows(h, gates, ys, pos0, pos1, rows=256)

    out = rmsnorm(h, row(final_norm_g), tm=tm)
    return out.reshape(b, s, d)
```

```python
import functools
import math

import jax
import jax.numpy as jnp
import numpy as np
from jax import lax
from jax.experimental import pallas as pl
from jax.experimental.pallas import tpu as pltpu

F32 = jnp.float32
BF16 = jnp.bfloat16
I32 = jnp.int32

NORM_EPS = 1e-6
ROPE_THETA = 10000.0
NEG = -1e30
LOG2E = math.log2(math.e)
LANES = 128
CHUNK = 64

A_LEFT_CHUNKS = 8
A_MAX_REL = 256
A_HEAD_DIM = 64
B_NOPE_DIM = 64
B_ROPE_DIM = 32
B_V_DIM = 64
C_HEAD_DIM = 64
N_EXPERTS = 8

VMEM_LIMIT = 56 * 1024 * 1024


def _params(*sem):
    return pltpu.CompilerParams(dimension_semantics=sem, vmem_limit_bytes=VMEM_LIMIT)


def _rms(x, g):
    ms = jnp.mean(x * x, axis=-1, keepdims=True)
    return x * lax.rsqrt(ms + NORM_EPS) * g


def _dot(a, b):
    return jnp.dot(a, b, preferred_element_type=F32)


def _dot_nt(a, b):
    return lax.dot_general(a, b, (((1,), (1,)), ((), ())), preferred_element_type=F32)


def _rope_lanes(x, cos, sin, half):
    if 2 * half == LANES:
        swapped = pltpu.roll(x, half, 1)
    else:
        lane = lax.broadcasted_iota(I32, x.shape, 1)
        fwd = pltpu.roll(x, LANES - half, 1)
        bwd = pltpu.roll(x, half, 1)
        swapped = jnp.where((lane & half) == 0, fwd, bwd)
    return x * cos + swapped * sin


def _rms_matmul_kernel(x_ref, g_ref, w_ref, o_ref, xn_ref):
    @pl.when(pl.program_id(1) == 0)
    def _():
        xn_ref[...] = _rms(x_ref[...], g_ref[...]).astype(BF16)

    o_ref[...] = _dot(xn_ref[...], w_ref[...]).astype(o_ref.dtype)


def _rms_matmul_rope_kernel(x_ref, g_ref, w_ref, cos_ref, sin_ref, o_ref, xn_ref, *,
                            n_rope_tiles, half):
    j = pl.program_id(1)

    @pl.when(j == 0)
    def _():
        xn_ref[...] = _rms(x_ref[...], g_ref[...]).astype(BF16)

    acc = _dot(xn_ref[...], w_ref[...])

    @pl.when(j < n_rope_tiles)
    def _():
        cos = cos_ref[...]
        sin = sin_ref[...]
        for c in range(acc.shape[1] // LANES):
            sl = slice(c * LANES, (c + 1) * LANES)
            o_ref[:, sl] = _rope_lanes(acc[:, sl], cos, sin, half).astype(o_ref.dtype)

    @pl.when(j >= n_rope_tiles)
    def _():
        o_ref[...] = acc.astype(o_ref.dtype)


def rms_matmul(x, g, w, *, tm, tn, rope=None):
    t, d = x.shape
    n = w.shape[1]
    grid = (t // tm, n // tn)
    x_spec = pl.BlockSpec((tm, d), lambda i, j: (i, 0))
    g_spec = pl.BlockSpec((1, d), lambda i, j: (0, 0))
    w_spec = pl.BlockSpec((d, tn), lambda i, j: (0, j))
    o_spec = pl.BlockSpec((tm, tn), lambda i, j: (i, j))
    scratch = [pltpu.VMEM((tm, d), BF16)]
    out_shape = jax.ShapeDtypeStruct((t, n), BF16)
    if rope is None:
        return pl.pallas_call(
            _rms_matmul_kernel, grid=grid, in_specs=[x_spec, g_spec, w_spec], out_specs=o_spec,
            out_shape=out_shape, scratch_shapes=scratch,
            compiler_params=_params("parallel", "arbitrary"), name="rms_matmul",
        )(x, g, w)
    tables, per_part, half = rope
    cos_spec = pl.BlockSpec((None, None, tm, LANES),
                            lambda i, j: (jnp.minimum(j // per_part, 1), 0, i, 0))
    sin_spec = pl.BlockSpec((None, None, tm, LANES),
                            lambda i, j: (jnp.minimum(j // per_part, 1), 1, i, 0))
    kern = functools.partial(_rms_matmul_rope_kernel, n_rope_tiles=2 * per_part, half=half)
    return pl.pallas_call(
        kern, grid=grid, in_specs=[x_spec, g_spec, w_spec, cos_spec, sin_spec], out_specs=o_spec,
        out_shape=out_shape, scratch_shapes=scratch,
        compiler_params=_params("parallel", "arbitrary"), name="rms_matmul_rope",
    )(x, g, w, tables, tables)


def _mm_res_kernel(*refs, n_in):
    res_ref = refs[2 * n_in]
    o_ref = refs[2 * n_in + 1]
    acc = res_ref[...]
    for k in range(n_in):
        acc = acc + _dot(refs[k][...], refs[n_in + k][...])
    o_ref[...] = acc


def matmul_residual(a_list, w_list, res, *, tm):
    t, n = res.shape
    n_in = len(a_list)
    in_specs = [pl.BlockSpec((tm, a.shape[1]), lambda i: (i, 0)) for a in a_list]
    in_specs += [pl.BlockSpec(w.shape, lambda i: (0, 0)) for w in w_list]
    in_specs += [pl.BlockSpec((tm, n), lambda i: (i, 0))]
    return pl.pallas_call(
        functools.partial(_mm_res_kernel, n_in=n_in), grid=(t // tm,), in_specs=in_specs,
        out_specs=pl.BlockSpec((tm, n), lambda i: (i, 0)),
        out_shape=jax.ShapeDtypeStruct((t, n), F32),
        compiler_params=_params("parallel"), name="matmul_residual",
    )(*a_list, *w_list, res)


U32 = jnp.uint32
HI16 = 0xFFFF0000


def _pack_bf16_pairs(x):
    n = x.shape[1] // 2
    bits = lax.bitcast_convert_type(x.astype(BF16).astype(F32), U32)
    return (bits[:, :n] & U32(HI16)) | (bits[:, n:] >> 16)


def _unpack_bf16_pairs(w):
    first = lax.bitcast_convert_type(w & U32(HI16), F32)
    second = lax.bitcast_convert_type(w << 16, F32)
    return first, second


def _ffn_kernel(te_ref, nv_ref, x_ref, g_ref, wg_ref, wu_ref, wd_ref, o_ref, xn_ref, acc_ref, *,
                add_res):
    del te_ref
    i = pl.program_id(0)
    f = pl.program_id(1)
    nf = pl.num_programs(1)
    valid = i < nv_ref[0]

    @pl.when(valid & (f == 0))
    def _():
        xn_ref[...] = _rms(x_ref[...], g_ref[...]).astype(BF16)
        acc_ref[...] = jnp.zeros_like(acc_ref)

    _ffn_body(valid, wg_ref, wu_ref, wd_ref, xn_ref, acc_ref)

    @pl.when(valid & (f == nf - 1))
    def _():
        if add_res:
            o_ref[...] = x_ref[...] + acc_ref[...]
        else:
            o_ref[...] = acc_ref[...]

    @pl.when(jnp.logical_not(valid) & (f == nf - 1))
    def _():
        o_ref[...] = jnp.zeros_like(o_ref)


def _ffn_body(valid, wg_ref, wu_ref, wd_ref, xn_ref, acc_ref):
    @pl.when(valid)
    def _():
        xn = xn_ref[...]
        gate = _dot(xn, wg_ref[...])
        up = _dot(xn, wu_ref[...])
        act = (gate * jax.nn.sigmoid(gate) * up).astype(BF16)
        acc_ref[...] += _dot(act, wd_ref[...])


def _ffn_packed_kernel(te_ref, nv_ref, x_ref, wg_ref, wu_ref, wd_ref, o_ref, xn_ref, acc_ref):
    del te_ref
    i = pl.program_id(0)
    f = pl.program_id(1)
    nf = pl.num_programs(1)
    valid = i < nv_ref[0]
    half = x_ref.shape[1]

    @pl.when(valid & (f == 0))
    def _():
        first, second = _unpack_bf16_pairs(x_ref[...])
        xn_ref[:, :half] = first.astype(BF16)
        xn_ref[:, half:] = second.astype(BF16)
        acc_ref[...] = jnp.zeros_like(acc_ref)

    _ffn_body(valid, wg_ref, wu_ref, wd_ref, xn_ref, acc_ref)

    @pl.when(valid & (f == nf - 1))
    def _():
        o_ref[...] = _pack_bf16_pairs(acc_ref[...])

    @pl.when(jnp.logical_not(valid) & (f == nf - 1))
    def _():
        o_ref[...] = jnp.zeros_like(o_ref)


def grouped_swiglu(x, g, w_gu, w_down, tile_expert, n_valid, *, tm, tf, add_res=False):
    packed = g is None
    assert not (packed and add_res)
    p = x.shape[0]
    d = w_gu.shape[1]
    ff = w_down.shape[1]
    nf = ff // tf
    grid = (p // tm, nf)

    def f_eff(i, f, nv):
        return jnp.where(i < nv[0], f, nf - 1)

    x_spec = pl.BlockSpec((tm, x.shape[1]), lambda i, f, te, nv: (i, 0))
    w_specs = [
        pl.BlockSpec((None, d, tf), lambda i, f, te, nv: (te[i], 0, f_eff(i, f, nv))),
        pl.BlockSpec((None, d, tf), lambda i, f, te, nv: (te[i], 0, nf + f_eff(i, f, nv))),
        pl.BlockSpec((None, tf, d), lambda i, f, te, nv: (te[i], f_eff(i, f, nv), 0)),
    ]
    if packed:
        kern, in_specs, operands = _ffn_packed_kernel, [x_spec] + w_specs, (x,)
    else:
        kern = functools.partial(_ffn_kernel, add_res=add_res)
        in_specs = [x_spec, pl.BlockSpec((1, d), lambda i, f, te, nv: (0, 0))] + w_specs
        operands = (x, g)
    grid_spec = pltpu.PrefetchScalarGridSpec(
        num_scalar_prefetch=2, grid=grid, in_specs=in_specs,
        out_specs=pl.BlockSpec((tm, x.shape[1]), lambda i, f, te, nv: (i, 0)),
        scratch_shapes=[pltpu.VMEM((tm, d), BF16), pltpu.VMEM((tm, d), F32)],
    )
    return pl.pallas_call(
        kern, grid_spec=grid_spec, out_shape=jax.ShapeDtypeStruct(x.shape, x.dtype),
        compiler_params=_params("parallel", "arbitrary"), name="grouped_swiglu",
    )(tile_expert, n_valid, *operands, w_gu, w_gu, w_down)


def _flash_kernel(lam_ref, q_ref, k_ref, v_ref, g_ref, o_ref, m_ref, l_ref, acc_ref, vt_ref,
                  s_ref, *, mode, tq, groups, out_scale):
    i = pl.program_id(2)
    s_len = v_ref.shape[0]
    wqk = q_ref.shape[1] // groups
    n_maps = 2 * groups
    sub = lax.broadcasted_iota(I32, (LANES, tq), 0)
    top = sub < (LANES // 2)

    @pl.when(i == 0)
    def _():
        for c in range(s_len // tq):
            sl = slice(c * tq, (c + 1) * tq)
            vt_ref[:, sl] = v_ref[sl, :].T

    qts = []
    for g in range(groups):
        if mode == "mla":
            qts += [q_ref[:, g * wqk:g * wqk + LANES].T, q_ref[:, g * wqk + LANES:(g + 1) * wqk].T]
        else:
            qt = q_ref[:, g * wqk:(g + 1) * wqk].T
            zero = jnp.zeros_like(qt)
            map1 = (sub & (C_HEAD_DIM // 2)) == 0
            qts += [jnp.where(map1, qt, zero), jnp.where(map1, zero, qt)]

    m_ref[...] = jnp.full_like(m_ref, NEG)
    l_ref[...] = jnp.zeros_like(l_ref)
    acc_ref[...] = jnp.zeros_like(acc_ref)

    def k_tile(mi, start):
        g, idx = divmod(mi, 2)
        if mode == "mla":
            return k_ref[pl.ds(start, tq), g * wqk + idx * LANES:g * wqk + (idx + 1) * LANES]
        return k_ref[pl.ds(start, tq), g * wqk:(g + 1) * wqk]

    def update(mi, s, vt_tile):
        m_prev = m_ref[mi]
        m_new = jnp.maximum(m_prev, jnp.max(s, axis=0, keepdims=True))
        alpha = jnp.exp2(m_prev - m_new)
        p = jnp.exp2(s - m_new)
        l_ref[mi] = alpha * l_ref[mi] + jnp.sum(p, axis=0, keepdims=True)
        acc_ref[mi] = alpha * acc_ref[mi] + _dot(vt_tile, p.astype(BF16))
        m_ref[mi] = m_new

    def scores_into(slot, t):
        start = pl.multiple_of(t * tq, tq)
        for mi in range(n_maps):
            s_ref[slot, mi] = _dot(k_tile(mi, start), qts[mi])

    def consume(slot, t, masked):
        start = pl.multiple_of(t * tq, tq)
        for mi in range(n_maps):
            g = mi // 2
            s = s_ref[slot, mi]
            if masked:
                row = lax.broadcasted_iota(I32, (tq, tq), 0)
                col = lax.broadcasted_iota(I32, (tq, tq), 1)
                shift = CHUNK.bit_length() - 1
                s = jnp.where((row >> shift) <= (col >> shift), s, NEG)
            update(mi, s, vt_ref[g * LANES:(g + 1) * LANES, pl.ds(start, tq)])

    scores_into(0, 0)

    def tile_pair(u, carry):
        t = 2 * u
        scores_into(1, t + 1)
        consume(0, t, False)
        scores_into(0, t + 2)
        consume(1, t + 1, False)
        return carry

    lax.fori_loop(0, i >> 1, tile_pair, 0)

    @pl.when((i & 1) == 1)
    def _():
        scores_into(1, i)
        consume(0, i - 1, False)
        consume(1, i, True)

    @pl.when((i & 1) == 0)
    def _():
        consume(0, i, True)

    for g in range(groups):
        o0 = acc_ref[2 * g] * (1.0 / l_ref[2 * g])
        o1 = acc_ref[2 * g + 1] * (1.0 / l_ref[2 * g + 1])
        cols = slice(g * LANES, (g + 1) * LANES)
        if mode == "mla":
            o_ref[:, cols] = jnp.where(top, o0, o1).T.astype(o_ref.dtype)
        else:
            o = (o0 - lam_ref[0] * o1).T
            o_ref[:, cols] = (_rms(o, g_ref[...]) * out_scale).astype(o_ref.dtype)


def flash_attention(q_arr, k_arr, v_arr, *, mode, q_blk0, k_blk0, v_blk0, n_groups, tq,
                    groups=1, lam=None, subln_g=None, out_scale=1.0):
    b, s, _ = q_arr.shape
    wqk = (2 * LANES if mode == "mla" else LANES) * groups
    wv = LANES * groups
    assert n_groups % groups == 0 and q_blk0 % groups == 0 and k_blk0 % groups == 0
    assert v_blk0 % groups == 0
    if lam is None:
        lam = jnp.zeros((1,), F32)
    if subln_g is None:
        subln_g = jnp.ones((1, LANES), F32)
    grid = (b, n_groups // groups, s // tq)
    qb, kb, vb = q_blk0 // groups, k_blk0 // groups, v_blk0 // groups
    in_specs = [
        pl.BlockSpec((None, tq, wqk), lambda bb, g, i, lam_r: (bb, i, qb + g)),
        pl.BlockSpec((None, s, wqk), lambda bb, g, i, lam_r: (bb, 0, kb + g)),
        pl.BlockSpec((None, s, wv), lambda bb, g, i, lam_r: (bb, 0, vb + g)),
        pl.BlockSpec((1, LANES), lambda bb, g, i, lam_r: (0, 0)),
    ]
    n_maps = 2 * groups
    grid_spec = pltpu.PrefetchScalarGridSpec(
        num_scalar_prefetch=1, grid=grid, in_specs=in_specs,
        out_specs=pl.BlockSpec((None, tq, wv), lambda bb, g, i, lam_r: (bb, i, g)),
        scratch_shapes=[pltpu.VMEM((n_maps, 1, tq), F32), pltpu.VMEM((n_maps, 1, tq), F32),
                        pltpu.VMEM((n_maps, LANES, tq), F32), pltpu.VMEM((wv, s), BF16),
                        pltpu.VMEM((2, n_maps, tq, tq), F32)],
    )
    kern = functools.partial(_flash_kernel, mode=mode, tq=tq, groups=groups, out_scale=out_scale)
    return pl.pallas_call(
        kern, grid_spec=grid_spec,
        out_shape=jax.ShapeDtypeStruct((b, s, n_groups * LANES), BF16),
        compiler_params=_params("parallel", "parallel", "arbitrary"), name="flash_" + mode,
    )(lam, q_arr, k_arr, v_arr, subln_g)


BAND_TQ = 256
BAND_TILES = 3
BAND_PAIRS = 4


def _band_kernel(q_ref, k0_ref, k1_ref, k2_ref, v0_ref, v1_ref, v2_ref, bias_ref, o_ref, *, scale):
    i = pl.program_id(2)
    lane = lax.broadcasted_iota(I32, (BAND_TQ, LANES), 1)
    low = lane < (LANES // 2)
    k_refs = (k0_ref, k1_ref, k2_ref)
    v_refs = (v0_ref, v1_ref, v2_ref)
    for pp in range(BAND_PAIRS):
        cols = slice(pp * LANES, (pp + 1) * LANES)
        q = q_ref[:, cols].astype(F32) * scale
        qs = (jnp.where(low, q, 0.0).astype(BF16), jnp.where(low, 0.0, q).astype(BF16))
        outs = []
        for h in range(2):
            scores = []
            for j in range(BAND_TILES):
                pen = jnp.where(i + j >= BAND_TILES - 1, 0.0, NEG).astype(F32)
                scores.append(_dot_nt(qs[h], k_refs[j][:, cols]) + bias_ref[pp, h, j] + pen)
            m = jnp.maximum(jnp.maximum(jnp.max(scores[0], axis=-1, keepdims=True),
                                        jnp.max(scores[1], axis=-1, keepdims=True)),
                            jnp.max(scores[2], axis=-1, keepdims=True))
            l = jnp.zeros_like(m)
            o = jnp.zeros((BAND_TQ, LANES), F32)
            for j in range(BAND_TILES):
                p = jnp.exp(scores[j] - m)
                l = l + jnp.sum(p, axis=-1, keepdims=True)
                o = o + _dot(p.astype(BF16), v_refs[j][:, cols])
            outs.append(o * (1.0 / l))
        o_ref[:, cols] = jnp.where(low, outs[0], outs[1]).astype(o_ref.dtype)


def band_bias_tiles(rel_bias):
    h = rel_bias.shape[0]
    n = BAND_TQ
    r = np.arange(n)[:, None]
    c = np.arange(n)[None, :]
    e_of = np.zeros(2 * n, np.int64)
    e_of[:n] = -np.arange(n)
    e_of[n + 1:] = n - 1 - np.arange(n - 1)
    tiles = []
    for j in range(BAND_TILES):
        rel = (BAND_TILES - 1 - j) * n + e_of
        gen = rel_bias[:, np.clip(rel, -A_MAX_REL, A_MAX_REL) + A_MAX_REL].astype(F32)
        skew = jnp.tile(gen, (1, n))[:, :n * (2 * n - 1)].reshape(h, n, 2 * n - 1)[:, :, :n]
        dist = (r // CHUNK) - (c // CHUNK) + (BAND_TILES - 1 - j) * (n // CHUNK)
        ok = (dist >= 0) & (dist <= A_LEFT_CHUNKS)
        tiles.append(jnp.where(ok[None], skew, NEG))
    t = jnp.stack(tiles, axis=1)
    return t.reshape(h // 2, 2, BAND_TILES, BAND_TQ, BAND_TQ)


def band_attention(proj, bias_tiles, *, q_blk0, k_blk0, v_blk0):
    b, s, _ = proj.shape
    n_pairs = bias_tiles.shape[0]
    width = BAND_PAIRS * LANES
    assert n_pairs % BAND_PAIRS == 0 and q_blk0 % BAND_PAIRS == 0
    assert k_blk0 % BAND_PAIRS == 0 and v_blk0 % BAND_PAIRS == 0
    grid = (n_pairs // BAND_PAIRS, b, s // BAND_TQ)

    def kv_spec(blk0, j):
        return pl.BlockSpec(
            (None, BAND_TQ, width),
            lambda p, bb, i: (bb, jnp.maximum(i + j - (BAND_TILES - 1), 0),
                              blk0 // BAND_PAIRS + p))

    in_specs = [pl.BlockSpec((None, BAND_TQ, width),
                             lambda p, bb, i: (bb, i, q_blk0 // BAND_PAIRS + p))]
    in_specs += [kv_spec(k_blk0, j) for j in range(BAND_TILES)]
    in_specs += [kv_spec(v_blk0, j) for j in range(BAND_TILES)]
    in_specs += [pl.BlockSpec((BAND_PAIRS, 2, BAND_TILES, BAND_TQ, BAND_TQ),
                              lambda p, bb, i: (p, 0, 0, 0, 0))]
    return pl.pallas_call(
        functools.partial(_band_kernel, scale=A_HEAD_DIM ** -0.5), grid=grid, in_specs=in_specs,
        out_specs=pl.BlockSpec((None, BAND_TQ, width), lambda p, bb, i: (bb, i, p)),
        out_shape=jax.ShapeDtypeStruct((b, s, n_pairs * LANES), BF16),
        compiler_params=_params("parallel", "parallel", "parallel"), name="band_attention",
    )(proj, proj, proj, proj, proj, proj, proj, bias_tiles)


def _mla_prep_kernel(cq_ref, ckv_ref, kr_ref, gq_ref, gkv_ref, wq_ref, wk_ref, wv_ref,
                     cq_cos_ref, cq_sin_ref, ck_cos_ref, ck_sin_ref, q_ref, k_ref, v_ref):
    half = B_ROPE_DIM // 2
    nq = _rms(cq_ref[...].astype(F32), gq_ref[...]).astype(BF16)
    nkv = _rms(ckv_ref[...].astype(F32), gkv_ref[...]).astype(BF16)
    q = _dot(nq, wq_ref[...])
    k = _dot(nkv, wk_ref[...])
    v_ref[...] = _dot(nkv, wv_ref[...]).astype(v_ref.dtype)
    kr = _rope_lanes(kr_ref[...].astype(F32), ck_cos_ref[...], ck_sin_ref[...], half)
    lane = lax.broadcasted_iota(I32, kr.shape, 1)
    kr = jnp.where(lane < B_ROPE_DIM, kr, 0.0)
    kr = pltpu.roll(kr, B_NOPE_DIM, 1)
    cos = cq_cos_ref[...]
    sin = cq_sin_ref[...]
    for c in range(q.shape[1] // LANES):
        sl = slice(c * LANES, (c + 1) * LANES)
        q_ref[:, sl] = _rope_lanes(q[:, sl], cos, sin, half).astype(q_ref.dtype)
        k_ref[:, sl] = (k[:, sl] + kr).astype(k_ref.dtype)


def mla_prep(proj, gq, gkv, wq, wk, wv, tables, *, tm, cq_blk, ckv_blk, kr_blk):
    t = proj.shape[0]
    nq = wq.shape[0]
    nkv = wk.shape[0]
    whole = lambda a: pl.BlockSpec(a.shape, lambda i: (0,) * a.ndim)
    tab = lambda k: pl.BlockSpec((None, tm, LANES), lambda i: (k, i, 0))
    in_specs = [
        pl.BlockSpec((tm, nq), lambda i: (i, cq_blk)),
        pl.BlockSpec((tm, nkv), lambda i: (i, ckv_blk)),
        pl.BlockSpec((tm, LANES), lambda i: (i, kr_blk)),
        whole(gq), whole(gkv), whole(wq), whole(wk), whole(wv),
        tab(0), tab(1), tab(2), tab(3),
    ]
    out_specs = [pl.BlockSpec((tm, wq.shape[1]), lambda i: (i, 0)),
                 pl.BlockSpec((tm, wk.shape[1]), lambda i: (i, 0)),
                 pl.BlockSpec((tm, wv.shape[1]), lambda i: (i, 0))]
    out_shape = [jax.ShapeDtypeStruct((t, wq.shape[1]), BF16),
                 jax.ShapeDtypeStruct((t, wk.shape[1]), BF16),
                 jax.ShapeDtypeStruct((t, wv.shape[1]), BF16)]
    return pl.pallas_call(
        _mla_prep_kernel, grid=(t // tm,), in_specs=in_specs, out_specs=out_specs,
        out_shape=out_shape, compiler_params=_params("parallel"), name="mla_prep",
    )(proj, proj, proj, gq, gkv, wq, wk, wv, tables, tables, tables, tables)


def _router_kernel(h_ref, g_ref, wr_ref, gate_ref, idx_ref, cnt_ref, npk_ref, carry_ref):
    tb = h_ref.shape[0]

    @pl.when(pl.program_id(0) == 0)
    def _():
        carry_ref[...] = jnp.zeros_like(carry_ref)

    n = _rms(h_ref[...], g_ref[...])
    npk_ref[...] = _pack_bf16_pairs(n)
    n_hi = n.astype(BF16)
    n_lo = (n - n_hi.astype(F32)).astype(BF16)
    logits = _dot(n_hi, wr_ref[0]) + (_dot(n_hi, wr_ref[1]) + _dot(n_lo, wr_ref[0]))
    lane = lax.broadcasted_iota(I32, (tb, LANES), 1).astype(F32)
    neg_inf = jnp.float32(-jnp.inf)
    lg = jnp.where(lane < N_EXPERTS, logits, neg_inf)
    m1 = jnp.max(lg, axis=-1, keepdims=True)
    e0 = jnp.min(jnp.where(lg == m1, lane, float(LANES)), axis=-1, keepdims=True)
    lg2 = jnp.where(lane == e0, neg_inf, lg)
    m2 = jnp.max(lg2, axis=-1, keepdims=True)
    e1 = jnp.min(jnp.where(lg2 == m2, lane, float(LANES)), axis=-1, keepdims=True)
    t = jnp.exp(m2 - m1)
    g0 = 1.0 / (1.0 + t)
    g1 = t / (1.0 + t)

    sel = ((lane == e0) | (lane == e1)).astype(F32)
    row = lax.broadcasted_iota(I32, (tb, tb), 0)
    col = lax.broadcasted_iota(I32, (tb, tb), 1)
    strict_lower = (col < row).astype(BF16)
    carry = carry_ref[...]
    excl = _dot(strict_lower, sel.astype(BF16)) + carry
    r0 = jnp.sum(jnp.where(lane == e0, excl, 0.0), axis=-1, keepdims=True)
    r1 = jnp.sum(jnp.where(lane == e1, excl, 0.0), axis=-1, keepdims=True)
    carry = carry + jnp.sum(sel, axis=0, keepdims=True)
    carry_ref[...] = carry

    gate_ref[...] = jnp.where(lane == 0, g0, jnp.where(lane == 1, g1, 0.0))
    idx_ref[...] = jnp.where(
        lane == 0, e0, jnp.where(lane == 1, e1, jnp.where(
            lane == 2, r0, jnp.where(lane == 3, r1, 0.0)))).astype(I32)
    cnt_ref[...] = jnp.broadcast_to(carry, cnt_ref.shape).astype(I32)


def router(h, g, wr_split, *, tb):
    t, d = h.shape
    return pl.pallas_call(
        _router_kernel, grid=(t // tb,),
        in_specs=[pl.BlockSpec((tb, d), lambda i: (i, 0)),
                  pl.BlockSpec((1, d), lambda i: (0, 0)),
                  pl.BlockSpec((2, d, LANES), lambda i: (0, 0, 0))],
        out_specs=[pl.BlockSpec((tb, LANES), lambda i: (i, 0)),
                   pl.BlockSpec((tb, LANES), lambda i: (i, 0)),
                   pl.BlockSpec((8, LANES), lambda i: (0, 0)),
                   pl.BlockSpec((tb, d // 2), lambda i: (i, 0))],
        out_shape=[jax.ShapeDtypeStruct((t, LANES), F32),
                   jax.ShapeDtypeStruct((t, LANES), I32),
                   jax.ShapeDtypeStruct((8, LANES), I32),
                   jax.ShapeDtypeStruct((t, d // 2), U32)],
        scratch_shapes=[pltpu.VMEM((1, LANES), F32)],
        compiler_params=_params("arbitrary"), name="router",
    )(h, g, wr_split)


def _row_copy(src_hbm, row, dst_ref, r, sem):
    return pltpu.make_async_copy(src_hbm.at[pl.ds(row, 1), :], dst_ref.at[pl.ds(r, 1), :], sem)


ROW_UNROLL = 8


def _for_rows(rows, body):
    def outer(o, carry):
        for u in range(ROW_UNROLL):
            body(o * ROW_UNROLL + u, u)
        return carry

    lax.fori_loop(0, rows // ROW_UNROLL, outer, 0)


SCATTER_SLOTS = 3


def _scatter_kernel(p0_ref, p1_ref, x_hbm, init_hbm, o_hbm, buf_ref, blk_sem, row_sem):
    del init_hbm
    i = pl.program_id(0)
    n = pl.num_programs(0)
    rows = p0_ref.shape[-1]
    slot = lax.rem(i, SCATTER_SLOTS)
    nxt = lax.rem(i + 1, SCATTER_SLOTS)

    def block_read(s, step):
        return pltpu.make_async_copy(x_hbm.at[pl.ds(step * rows, rows), :], buf_ref.at[s],
                                     blk_sem.at[s])

    def row_out(s, r, dst_row):
        return pltpu.make_async_copy(buf_ref.at[s, pl.ds(r, 1), :], o_hbm.at[pl.ds(dst_row, 1), :],
                                     row_sem.at[s])

    def start_rows(s):
        def start(r, u):
            row_out(s, r, p0_ref[0, 0, r]).start(priority=0)
            row_out(s, r, p1_ref[0, 0, r]).start(priority=1)
        _for_rows(rows, start)

    def wait_rows(s):
        def wait(r, u):
            row_out(s, r, 0).wait()
            row_out(s, r, 0).wait()
        _for_rows(rows, wait)

    @pl.when(i == 0)
    def _():
        block_read(0, 0).start()

    @pl.when(i >= 2)
    def _():
        wait_rows(nxt)

    @pl.when(i + 1 < n)
    def _():
        block_read(nxt, i + 1).start()

    block_read(slot, i).wait()
    start_rows(slot)

    @pl.when(i == n - 1)
    def _():
        @pl.when(i >= 1)
        def _():
            wait_rows(lax.rem(i + 2, SCATTER_SLOTS))
        wait_rows(slot)


def scatter_rows(x, pos0, pos1, n_out, *, rows):
    t, w = x.shape
    n = t // rows
    smem = lambda: pl.BlockSpec((1, 1, rows), lambda i: (i, 0, 0), memory_space=pltpu.SMEM)
    init = jnp.zeros((n_out, w), x.dtype)
    return pl.pallas_call(
        _scatter_kernel, grid=(n,),
        in_specs=[smem(), smem(), pl.BlockSpec(memory_space=pl.ANY),
                  pl.BlockSpec(memory_space=pl.ANY)],
        out_specs=pl.BlockSpec(memory_space=pl.ANY),
        out_shape=jax.ShapeDtypeStruct((n_out, w), x.dtype),
        input_output_aliases={3: 0},
        scratch_shapes=[pltpu.VMEM((SCATTER_SLOTS, rows, w), x.dtype),
                        pltpu.SemaphoreType.DMA((SCATTER_SLOTS,)),
                        pltpu.SemaphoreType.DMA((SCATTER_SLOTS,))],
        compiler_params=_params("arbitrary"), name="scatter_rows",
    )(pos0.reshape(n, 1, rows), pos1.reshape(n, 1, rows), x, init)


def _combine_kernel(p0_first, p1_first, p0_next, p1_next, h_ref, gate_ref, ys_hbm, *rest):
    norm_ref = rest[0] if len(rest) == 4 else None
    o_ref, buf_ref, sem = rest[-3:]
    i = pl.program_id(0)
    n = pl.num_programs(0)
    rows = h_ref.shape[0]
    slot = i & 1

    def start_into(s, p0_ref, p1_ref):
        def start(r, u):
            _row_copy(ys_hbm, p0_ref[0, 0, r], buf_ref.at[s, 0], r, sem.at[s]).start(priority=0)
            _row_copy(ys_hbm, p1_ref[0, 0, r], buf_ref.at[s, 1], r, sem.at[s]).start(priority=1)
        _for_rows(rows, start)

    @pl.when(i == 0)
    def _():
        start_into(0, p0_first, p1_first)

    @pl.when(i + 1 < n)
    def _():
        start_into(1 - slot, p0_next, p1_next)

    def wait(r, u):
        _row_copy(ys_hbm, 0, buf_ref.at[slot, 0], r, sem.at[slot]).wait()
        _row_copy(ys_hbm, 0, buf_ref.at[slot, 1], r, sem.at[slot]).wait()

    _for_rows(rows, wait)
    gates = gate_ref[...]
    half = buf_ref.shape[-1]
    a0, b0 = _unpack_bf16_pairs(buf_ref[slot, 0])
    a1, b1 = _unpack_bf16_pairs(buf_ref[slot, 1])
    lo = h_ref[:, :half] + gates[:, 0:1] * a0 + gates[:, 1:2] * a1
    hi = h_ref[:, half:] + gates[:, 0:1] * b0 + gates[:, 1:2] * b1
    if norm_ref is not None:
        ms = (jnp.sum(lo * lo, axis=-1, keepdims=True)
              + jnp.sum(hi * hi, axis=-1, keepdims=True)) * (1.0 / (2 * half))
        inv = lax.rsqrt(ms + NORM_EPS)
        lo = lo * inv * norm_ref[:, :half]
        hi = hi * inv * norm_ref[:, half:]
    o_ref[:, :half] = lo
    o_ref[:, half:] = hi


def combine_rows(h, gates, ys, pos0, pos1, *, rows, norm_g=None):
    t, d = h.shape
    n = t // rows
    first = lambda: pl.BlockSpec((1, 1, rows), lambda i: (0, 0, 0), memory_space=pltpu.SMEM)
    nxt = lambda: pl.BlockSpec((1, 1, rows), lambda i: (jnp.minimum(i + 1, n - 1), 0, 0),
                               memory_space=pltpu.SMEM)
    pos0 = pos0.reshape(n, 1, rows)
    pos1 = pos1.reshape(n, 1, rows)
    in_specs = [first(), first(), nxt(), nxt(),
                pl.BlockSpec((rows, d), lambda i: (i, 0)),
                pl.BlockSpec((rows, LANES), lambda i: (i, 0)),
                pl.BlockSpec(memory_space=pl.ANY)]
    operands = [pos0, pos1, pos0, pos1, h, gates, ys]
    if norm_g is not None:
        in_specs.append(pl.BlockSpec((1, d), lambda i: (0, 0)))
        operands.append(norm_g)
    return pl.pallas_call(
        _combine_kernel, grid=(n,), in_specs=in_specs,
        out_specs=pl.BlockSpec((rows, d), lambda i: (i, 0)),
        out_shape=jax.ShapeDtypeStruct((t, d), F32),
        scratch_shapes=[pltpu.VMEM((2, 2, rows, ys.shape[1]), ys.dtype),
                        pltpu.SemaphoreType.DMA((2,))],
        compiler_params=_params("arbitrary"), name="combine_rows",
    )(*operands)


def _rmsnorm_kernel(x_ref, g_ref, o_ref):
    o_ref[...] = _rms(x_ref[...], g_ref[...])


def rmsnorm(x, g, *, tm):
    t, d = x.shape
    return pl.pallas_call(
        _rmsnorm_kernel, grid=(t // tm,),
        in_specs=[pl.BlockSpec((tm, d), lambda i: (i, 0)), pl.BlockSpec((1, d), lambda i: (0, 0))],
        out_specs=pl.BlockSpec((tm, d), lambda i: (i, 0)),
        out_shape=jax.ShapeDtypeStruct((t, d), F32),
        compiler_params=_params("parallel"), name="final_rmsnorm",
    )(x, g)


def _rope_cos_sin(positions, dim):
    inv_freq = 1.0 / jnp.power(ROPE_THETA, jnp.arange(0, dim, 2, dtype=F32) / dim)
    ang = positions.reshape(-1).astype(F32)[:, None] * inv_freq
    return jnp.cos(ang), jnp.sin(ang)


def _diff_tables(positions):
    cos, sin = _rope_cos_sin(positions, C_HEAD_DIM)
    cos_l = jnp.tile(cos, (1, 4))
    sin_l = jnp.concatenate([-sin, -sin, sin, sin], axis=1)
    k_tab = jnp.stack([cos_l, sin_l])
    return jnp.stack([k_tab * (C_HEAD_DIM ** -0.5 * LOG2E), k_tab])


def _diff_head_order(w, n_cols):
    d = w.shape[0]
    half = C_HEAD_DIM // 2
    head = w[:, :n_cols].reshape(d, n_cols // LANES, 2, 2, half)
    head = head.transpose(0, 1, 3, 2, 4).reshape(d, n_cols)
    return jnp.concatenate([head, w[:, n_cols:]], axis=1)


def _mla_tables(positions):
    cos, sin = _rope_cos_sin(positions, B_ROPE_DIM)
    t = cos.shape[0]
    scale = (B_NOPE_DIM + B_ROPE_DIM) ** -0.5 * LOG2E
    ones = lambda n: jnp.ones((t, n), F32)
    zeros = lambda n: jnp.zeros((t, n), F32)
    q_cos = jnp.concatenate([ones(B_NOPE_DIM), cos, cos, ones(32)], axis=1) * scale
    q_sin = jnp.concatenate([zeros(B_NOPE_DIM), -sin, sin, zeros(32)], axis=1) * scale
    k_cos = jnp.concatenate([cos, cos, ones(96)], axis=1)
    k_sin = jnp.concatenate([-sin, sin, zeros(96)], axis=1)
    return jnp.stack([q_cos, q_sin, k_cos, k_sin])


def _routing_plan(idx_out, counts, *, tm, n_tiles):
    e0, e1, r0, r1 = idx_out[:, 0], idx_out[:, 1], idx_out[:, 2], idx_out[:, 3]
    padded = ((counts + tm - 1) // tm) * tm
    ends = jnp.cumsum(padded)
    offs = ends - padded
    experts = jnp.arange(N_EXPERTS, dtype=I32)[None, :]
    pos0 = jnp.sum(jnp.where(e0[:, None] == experts, offs[None, :], 0), axis=1) + r0
    pos1 = jnp.sum(jnp.where(e1[:, None] == experts, offs[None, :], 0), axis=1) + r1
    n_valid = (ends[-1] // tm).astype(I32)
    tile = jnp.minimum(jnp.arange(n_tiles, dtype=I32), n_valid - 1)
    tile_expert = jnp.minimum(jnp.sum((tile * tm)[:, None] >= ends[None, :], axis=1),
                              N_EXPERTS - 1)
    return pos0.astype(I32), pos1.astype(I32), tile_expert.astype(I32), n_valid.reshape(1)


def _pad_cols(w, n):
    return jnp.pad(w, ((0, 0), (0, n - w.shape[1])))


def kernel(x, positions, mix_norm_g, ffn_norm_g, w_in_even, rel_bias_a, q_norm_b, w_uq_b, kv_norm_b, w_ukv_b, w_out_even, w_in_odd, lambda_q1, lambda_k1, lambda_q2, lambda_k2, subln_g, w_out_odd, w_gu_dense, w_down_dense, w_router, w_gu_moe, w_down_moe, final_norm_g):
    b, s, d = x.shape
    t = b * s
    depth = mix_norm_g.shape[0]
    a_width = 4 * LANES
    tm = 512
    tm_moe = 512
    n_tiles_moe = (2 * t) // tm_moe + N_EXPERTS

    diff_tab = _diff_tables(positions)
    mla_tab = _mla_tables(positions)
    row = lambda v: v.reshape(1, -1).astype(F32)

    h = x.reshape(t, d).astype(F32)
    for layer in range(depth):
        i = layer // 2
        if layer % 2 == 0:
            w_in = _pad_cols(w_in_even[i], 2048).astype(BF16)
            proj = rms_matmul(h, row(mix_norm_g[layer]), w_in, tm=tm, tn=1024)
            proj3 = proj.reshape(b, s, -1)
            out_a = band_attention(proj3, band_bias_tiles(rel_bias_a[i]),
                                   q_blk0=0, k_blk0=4, v_blk0=8)
            n_heads = w_uq_b.shape[2] // (B_NOPE_DIM + B_ROPE_DIM)
            wq = jnp.pad(w_uq_b[i].reshape(-1, n_heads, B_NOPE_DIM + B_ROPE_DIM),
                         ((0, 0), (0, 0), (0, 32))).reshape(-1, n_heads * LANES).astype(BF16)
            wkv = w_ukv_b[i].reshape(-1, n_heads, B_NOPE_DIM + B_V_DIM)
            wk = jnp.pad(wkv[:, :, :B_NOPE_DIM], ((0, 0), (0, 0), (0, LANES - B_NOPE_DIM)))
            wk = wk.reshape(-1, n_heads * LANES).astype(BF16)
            wv = wkv[:, :, B_NOPE_DIM:].reshape(-1, n_heads * B_V_DIM).astype(BF16)
            q_b, k_b, v_b = mla_prep(proj, row(q_norm_b[i]), row(kv_norm_b[i]), wq, wk, wv, mla_tab,
                                     tm=tm, cq_blk=6, ckv_blk=14, kr_blk=15)
            out_b = flash_attention(q_b.reshape(b, s, -1), k_b.reshape(b, s, -1),
                                    v_b.reshape(b, s, -1), mode="mla", q_blk0=0, k_blk0=0,
                                    v_blk0=0, n_groups=n_heads // 2, tq=512, groups=2)
            w_out = w_out_even[i].astype(BF16)
            h = matmul_residual([out_a.reshape(t, -1), out_b.reshape(t, -1)],
                                [w_out[:a_width], w_out[a_width:]], h, tm=tm)
            n_tiles = t // tm
            h = grouped_swiglu(h, row(ffn_norm_g[layer]), w_gu_dense[i][None].astype(BF16),
                               w_down_dense[i][None].astype(BF16), jnp.zeros((n_tiles,), I32),
                               jnp.full((1,), n_tiles, I32), tm=tm, tf=1408, add_res=True)
        else:
            c_width = w_out_odd.shape[1]
            n_heads = c_width // LANES
            w_in = _diff_head_order(w_in_odd[i], 2 * c_width).astype(BF16)
            qkv = rms_matmul(h, row(mix_norm_g[layer]), w_in, tm=tm, tn=1024,
                             rope=(diff_tab, c_width // 1024, LANES // 2))
            lam_init = 0.8 - 0.6 * math.exp(-0.3 * layer)
            lam = (jnp.exp(jnp.sum(lambda_q1[i].astype(F32) * lambda_k1[i].astype(F32)))
                   - jnp.exp(jnp.sum(lambda_q2[i].astype(F32) * lambda_k2[i].astype(F32)))
                   + lam_init).reshape(1).astype(F32)
            qkv3 = qkv.reshape(b, s, -1)
            out_c = flash_attention(qkv3, qkv3, qkv3, mode="diff", q_blk0=0, k_blk0=n_heads,
                                    v_blk0=2 * n_heads, n_groups=n_heads, tq=512, groups=2, lam=lam,
                                    subln_g=row(subln_g[i]), out_scale=1.0 - lam_init)
            h = matmul_residual([out_c.reshape(t, -1)], [w_out_odd[i].astype(BF16)], h, tm=tm)

            g_ffn = row(ffn_norm_g[layer])
            wr = _pad_cols(w_router[i].astype(F32), LANES)
            wr_hi = wr.astype(BF16)
            wr_split = jnp.stack([wr_hi, (wr - wr_hi.astype(F32)).astype(BF16)])
            gates, idx_out, cnt, n_packed = router(h, g_ffn, wr_split, tb=512)
            pos0, pos1, tile_expert, n_valid = _routing_plan(
                idx_out, cnt[0, :N_EXPERTS], tm=tm_moe, n_tiles=n_tiles_moe)
            xs = scatter_rows(n_packed, pos0, pos1, n_tiles_moe * tm_moe, rows=256)
            ys = grouped_swiglu(xs, None, w_gu_moe[i].astype(BF16), w_down_moe[i].astype(BF16),
                                tile_expert, n_valid, tm=tm_moe, tf=1792)
            last = layer == depth - 1
            h = combine_rows(h, gates, ys, pos0, pos1, rows=256,
                             norm_g=row(final_norm_g) if last else None)

    if depth % 2 == 1:
        h = rmsnorm(h, row(final_norm_g), tm=tm)
    return h.reshape(b, s, d)
```

```python
import functools
import math

import jax
import jax.numpy as jnp
import numpy as np
from jax import lax
from jax.experimental import pallas as pl
from jax.experimental.pallas import tpu as pltpu

F32 = jnp.float32
BF16 = jnp.bfloat16
I32 = jnp.int32

NORM_EPS = 1e-6
ROPE_THETA = 10000.0
NEG = -1e30
LOG2E = math.log2(math.e)
LANES = 128
CHUNK = 64

A_LEFT_CHUNKS = 8
A_MAX_REL = 256
A_HEAD_DIM = 64
B_NOPE_DIM = 64
B_ROPE_DIM = 32
B_V_DIM = 64
C_HEAD_DIM = 64
N_EXPERTS = 8

VMEM_LIMIT = 56 * 1024 * 1024


def _params(*sem):
    return pltpu.CompilerParams(dimension_semantics=sem, vmem_limit_bytes=VMEM_LIMIT)


def _rms(x, g):
    ms = jnp.mean(x * x, axis=-1, keepdims=True)
    return x * lax.rsqrt(ms + NORM_EPS) * g


def _dot(a, b):
    return jnp.dot(a, b, preferred_element_type=F32)


def _dot_nt(a, b):
    return lax.dot_general(a, b, (((1,), (1,)), ((), ())), preferred_element_type=F32)


def _rope_lanes(x, cos, sin, half):
    if 2 * half == LANES:
        swapped = pltpu.roll(x, half, 1)
    else:
        lane = lax.broadcasted_iota(I32, x.shape, 1)
        fwd = pltpu.roll(x, LANES - half, 1)
        bwd = pltpu.roll(x, half, 1)
        swapped = jnp.where((lane & half) == 0, fwd, bwd)
    return x * cos + swapped * sin


def _rms_matmul_kernel(x_ref, g_ref, w_ref, o_ref, xn_ref):
    @pl.when(pl.program_id(1) == 0)
    def _():
        xn_ref[...] = _rms(x_ref[...], g_ref[...]).astype(BF16)

    o_ref[...] = _dot(xn_ref[...], w_ref[...]).astype(o_ref.dtype)


def _rms_matmul_rope_kernel(x_ref, g_ref, w_ref, cos_ref, sin_ref, o_ref, xn_ref, *,
                            n_rope_tiles, half):
    j = pl.program_id(1)

    @pl.when(j == 0)
    def _():
        xn_ref[...] = _rms(x_ref[...], g_ref[...]).astype(BF16)

    acc = _dot(xn_ref[...], w_ref[...])

    @pl.when(j < n_rope_tiles)
    def _():
        cos = cos_ref[...]
        sin = sin_ref[...]
        for c in range(acc.shape[1] // LANES):
            sl = slice(c * LANES, (c + 1) * LANES)
            o_ref[:, sl] = _rope_lanes(acc[:, sl], cos, sin, half).astype(o_ref.dtype)

    @pl.when(j >= n_rope_tiles)
    def _():
        o_ref[...] = acc.astype(o_ref.dtype)


def rms_matmul(x, g, w, *, tm, tn, rope=None):
    t, d = x.shape
    n = w.shape[1]
    grid = (t // tm, n // tn)
    x_spec = pl.BlockSpec((tm, d), lambda i, j: (i, 0))
    g_spec = pl.BlockSpec((1, d), lambda i, j: (0, 0))
    w_spec = pl.BlockSpec((d, tn), lambda i, j: (0, j))
    o_spec = pl.BlockSpec((tm, tn), lambda i, j: (i, j))
    scratch = [pltpu.VMEM((tm, d), BF16)]
    out_shape = jax.ShapeDtypeStruct((t, n), BF16)
    if rope is None:
        return pl.pallas_call(
            _rms_matmul_kernel, grid=grid, in_specs=[x_spec, g_spec, w_spec], out_specs=o_spec,
            out_shape=out_shape, scratch_shapes=scratch,
            compiler_params=_params("parallel", "arbitrary"), name="rms_matmul",
        )(x, g, w)
    tables, per_part, half = rope
    cos_spec = pl.BlockSpec((None, None, tm, LANES),
                            lambda i, j: (jnp.minimum(j // per_part, 1), 0, i, 0))
    sin_spec = pl.BlockSpec((None, None, tm, LANES),
                            lambda i, j: (jnp.minimum(j // per_part, 1), 1, i, 0))
    kern = functools.partial(_rms_matmul_rope_kernel, n_rope_tiles=2 * per_part, half=half)
    return pl.pallas_call(
        kern, grid=grid, in_specs=[x_spec, g_spec, w_spec, cos_spec, sin_spec], out_specs=o_spec,
        out_shape=out_shape, scratch_shapes=scratch,
        compiler_params=_params("parallel", "arbitrary"), name="rms_matmul_rope",
    )(x, g, w, tables, tables)


def _mm_res_kernel(*refs, n_in):
    res_ref = refs[2 * n_in]
    o_ref = refs[2 * n_in + 1]
    acc = res_ref[...]
    for k in range(n_in):
        acc = acc + _dot(refs[k][...], refs[n_in + k][...])
    o_ref[...] = acc


def matmul_residual(a_list, w_list, res, *, tm):
    t, n = res.shape
    n_in = len(a_list)
    in_specs = [pl.BlockSpec((tm, a.shape[1]), lambda i: (i, 0)) for a in a_list]
    in_specs += [pl.BlockSpec(w.shape, lambda i: (0, 0)) for w in w_list]
    in_specs += [pl.BlockSpec((tm, n), lambda i: (i, 0))]
    return pl.pallas_call(
        functools.partial(_mm_res_kernel, n_in=n_in), grid=(t // tm,), in_specs=in_specs,
        out_specs=pl.BlockSpec((tm, n), lambda i: (i, 0)),
        out_shape=jax.ShapeDtypeStruct((t, n), F32),
        compiler_params=_params("parallel"), name="matmul_residual",
    )(*a_list, *w_list, res)


U32 = jnp.uint32
HI16 = 0xFFFF0000


def _pack_bf16_pairs(x):
    n = x.shape[1] // 2
    bits = lax.bitcast_convert_type(x.astype(BF16).astype(F32), U32)
    return (bits[:, :n] & U32(HI16)) | (bits[:, n:] >> 16)


def _unpack_bf16_pairs(w):
    first = lax.bitcast_convert_type(w & U32(HI16), F32)
    second = lax.bitcast_convert_type(w << 16, F32)
    return first, second


def _ffn_kernel(te_ref, nv_ref, x_ref, g_ref, wg_ref, wu_ref, wd_ref, o_ref, xn_ref, acc_ref, *,
                add_res):
    del te_ref
    i = pl.program_id(0)
    f = pl.program_id(1)
    nf = pl.num_programs(1)
    valid = i < nv_ref[0]

    @pl.when(valid & (f == 0))
    def _():
        xn_ref[...] = _rms(x_ref[...], g_ref[...]).astype(BF16)
        acc_ref[...] = jnp.zeros_like(acc_ref)

    _ffn_body(valid, wg_ref, wu_ref, wd_ref, xn_ref, acc_ref)

    @pl.when(valid & (f == nf - 1))
    def _():
        if add_res:
            o_ref[...] = x_ref[...] + acc_ref[...]
        else:
            o_ref[...] = acc_ref[...]

    @pl.when(jnp.logical_not(valid) & (f == nf - 1))
    def _():
        o_ref[...] = jnp.zeros_like(o_ref)


def _ffn_body(valid, wg_ref, wu_ref, wd_ref, xn_ref, acc_ref):
    @pl.when(valid)
    def _():
        xn = xn_ref[...]
        gate = _dot(xn, wg_ref[...])
        up = _dot(xn, wu_ref[...])
        act = (gate * jax.nn.sigmoid(gate) * up).astype(BF16)
        acc_ref[...] += _dot(act, wd_ref[...])


def _ffn_packed_kernel(te_ref, nv_ref, x_ref, wg_ref, wu_ref, wd_ref, o_ref, xn_ref, acc_ref):
    del te_ref
    i = pl.program_id(0)
    f = pl.program_id(1)
    nf = pl.num_programs(1)
    valid = i < nv_ref[0]
    half = x_ref.shape[1]

    @pl.when(valid & (f == 0))
    def _():
        first, second = _unpack_bf16_pairs(x_ref[...])
        xn_ref[:, :half] = first.astype(BF16)
        xn_ref[:, half:] = second.astype(BF16)
        acc_ref[...] = jnp.zeros_like(acc_ref)

    _ffn_body(valid, wg_ref, wu_ref, wd_ref, xn_ref, acc_ref)

    @pl.when(valid & (f == nf - 1))
    def _():
        o_ref[...] = _pack_bf16_pairs(acc_ref[...])

    @pl.when(jnp.logical_not(valid) & (f == nf - 1))
    def _():
        o_ref[...] = jnp.zeros_like(o_ref)


def grouped_swiglu(x, g, w_gu, w_down, tile_expert, n_valid, *, tm, tf, add_res=False):
    packed = g is None
    assert not (packed and add_res)
    p = x.shape[0]
    d = w_gu.shape[1]
    ff = w_down.shape[1]
    nf = ff // tf
    grid = (p // tm, nf)

    def f_eff(i, f, nv):
        return jnp.where(i < nv[0], f, nf - 1)

    x_spec = pl.BlockSpec((tm, x.shape[1]), lambda i, f, te, nv: (i, 0))
    w_specs = [
        pl.BlockSpec((None, d, tf), lambda i, f, te, nv: (te[i], 0, f_eff(i, f, nv))),
        pl.BlockSpec((None, d, tf), lambda i, f, te, nv: (te[i], 0, nf + f_eff(i, f, nv))),
        pl.BlockSpec((None, tf, d), lambda i, f, te, nv: (te[i], f_eff(i, f, nv), 0)),
    ]
    if packed:
        kern, in_specs, operands = _ffn_packed_kernel, [x_spec] + w_specs, (x,)
    else:
        kern = functools.partial(_ffn_kernel, add_res=add_res)
        in_specs = [x_spec, pl.BlockSpec((1, d), lambda i, f, te, nv: (0, 0))] + w_specs
        operands = (x, g)
    grid_spec = pltpu.PrefetchScalarGridSpec(
        num_scalar_prefetch=2, grid=grid, in_specs=in_specs,
        out_specs=pl.BlockSpec((tm, x.shape[1]), lambda i, f, te, nv: (i, 0)),
        scratch_shapes=[pltpu.VMEM((tm, d), BF16), pltpu.VMEM((tm, d), F32)],
    )
    return pl.pallas_call(
        kern, grid_spec=grid_spec, out_shape=jax.ShapeDtypeStruct(x.shape, x.dtype),
        compiler_params=_params("parallel", "arbitrary"), name="grouped_swiglu",
    )(tile_expert, n_valid, *operands, w_gu, w_gu, w_down)


def _flash_kernel(lam_ref, q_ref, k_ref, v_ref, g_ref, o_ref, m_ref, l_ref, acc_ref, vt_ref,
                  s_ref, *, mode, tq, groups, out_scale):
    i = pl.program_id(2)
    s_len = v_ref.shape[0]
    wqk = q_ref.shape[1] // groups
    n_maps = 2 * groups
    sub = lax.broadcasted_iota(I32, (LANES, tq), 0)
    top = sub < (LANES // 2)

    @pl.when(i == 0)
    def _():
        for c in range(s_len // tq):
            sl = slice(c * tq, (c + 1) * tq)
            vt_ref[:, sl] = v_ref[sl, :].T

    qts = []
    for g in range(groups):
        if mode == "mla":
            qts += [q_ref[:, g * wqk:g * wqk + LANES].T, q_ref[:, g * wqk + LANES:(g + 1) * wqk].T]
        else:
            qt = q_ref[:, g * wqk:(g + 1) * wqk].T
            zero = jnp.zeros_like(qt)
            map1 = (sub & (C_HEAD_DIM // 2)) == 0
            qts += [jnp.where(map1, qt, zero), jnp.where(map1, zero, qt)]

    m_ref[...] = jnp.full_like(m_ref, NEG)
    l_ref[...] = jnp.zeros_like(l_ref)
    acc_ref[...] = jnp.zeros_like(acc_ref)

    def k_tile(mi, start):
        g, idx = divmod(mi, 2)
        if mode == "mla":
            return k_ref[pl.ds(start, tq), g * wqk + idx * LANES:g * wqk + (idx + 1) * LANES]
        return k_ref[pl.ds(start, tq), g * wqk:(g + 1) * wqk]

    def update(mi, s, vt_tile):
        m_prev = m_ref[mi]
        m_new = jnp.maximum(m_prev, jnp.max(s, axis=0, keepdims=True))
        alpha = jnp.exp2(m_prev - m_new)
        p = jnp.exp2(s - m_new)
        l_ref[mi] = alpha * l_ref[mi] + jnp.sum(p, axis=0, keepdims=True)
        acc_ref[mi] = alpha * acc_ref[mi] + _dot(vt_tile, p.astype(BF16))
        m_ref[mi] = m_new

    def scores_into(slot, t):
        start = pl.multiple_of(t * tq, tq)
        for mi in range(n_maps):
            s_ref[slot, mi] = _dot(k_tile(mi, start), qts[mi])

    def consume(slot, t, masked):
        start = pl.multiple_of(t * tq, tq)
        for mi in range(n_maps):
            g = mi // 2
            s = s_ref[slot, mi]
            if masked:
                row = lax.broadcasted_iota(I32, (tq, tq), 0)
                col = lax.broadcasted_iota(I32, (tq, tq), 1)
                shift = CHUNK.bit_length() - 1
                s = jnp.where((row >> shift) <= (col >> shift), s, NEG)
            update(mi, s, vt_ref[g * LANES:(g + 1) * LANES, pl.ds(start, tq)])

    scores_into(0, 0)

    def tile_pair(u, carry):
        t = 2 * u
        scores_into(1, t + 1)
        consume(0, t, False)
        scores_into(0, t + 2)
        consume(1, t + 1, False)
        return carry

    lax.fori_loop(0, i >> 1, tile_pair, 0)

    @pl.when((i & 1) == 1)
    def _():
        scores_into(1, i)
        consume(0, i - 1, False)
        consume(1, i, True)

    @pl.when((i & 1) == 0)
    def _():
        consume(0, i, True)

    for g in range(groups):
        o0 = acc_ref[2 * g] * (1.0 / l_ref[2 * g])
        o1 = acc_ref[2 * g + 1] * (1.0 / l_ref[2 * g + 1])
        cols = slice(g * LANES, (g + 1) * LANES)
        if mode == "mla":
            o_ref[:, cols] = jnp.where(top, o0, o1).T.astype(o_ref.dtype)
        else:
            o = (o0 - lam_ref[0] * o1).T
            o_ref[:, cols] = (_rms(o, g_ref[...]) * out_scale).astype(o_ref.dtype)


def flash_attention(q_arr, k_arr, v_arr, *, mode, q_blk0, k_blk0, v_blk0, n_groups, tq,
                    groups=1, lam=None, subln_g=None, out_scale=1.0):
    b, s, _ = q_arr.shape
    wqk = (2 * LANES if mode == "mla" else LANES) * groups
    wv = LANES * groups
    assert n_groups % groups == 0 and q_blk0 % groups == 0 and k_blk0 % groups == 0
    assert v_blk0 % groups == 0
    if lam is None:
        lam = jnp.zeros((1,), F32)
    if subln_g is None:
        subln_g = jnp.ones((1, LANES), F32)
    grid = (b, n_groups // groups, s // tq)
    qb, kb, vb = q_blk0 // groups, k_blk0 // groups, v_blk0 // groups
    in_specs = [
        pl.BlockSpec((None, tq, wqk), lambda bb, g, i, lam_r: (bb, i, qb + g)),
        pl.BlockSpec((None, s, wqk), lambda bb, g, i, lam_r: (bb, 0, kb + g)),
        pl.BlockSpec((None, s, wv), lambda bb, g, i, lam_r: (bb, 0, vb + g)),
        pl.BlockSpec((1, LANES), lambda bb, g, i, lam_r: (0, 0)),
    ]
    n_maps = 2 * groups
    grid_spec = pltpu.PrefetchScalarGridSpec(
        num_scalar_prefetch=1, grid=grid, in_specs=in_specs,
        out_specs=pl.BlockSpec((None, tq, wv), lambda bb, g, i, lam_r: (bb, i, g)),
        scratch_shapes=[pltpu.VMEM((n_maps, 1, tq), F32), pltpu.VMEM((n_maps, 1, tq), F32),
                        pltpu.VMEM((n_maps, LANES, tq), F32), pltpu.VMEM((wv, s), BF16),
                        pltpu.VMEM((2, n_maps, tq, tq), F32)],
    )
    kern = functools.partial(_flash_kernel, mode=mode, tq=tq, groups=groups, out_scale=out_scale)
    return pl.pallas_call(
        kern, grid_spec=grid_spec,
        out_shape=jax.ShapeDtypeStruct((b, s, n_groups * LANES), BF16),
        compiler_params=_params("parallel", "parallel", "arbitrary"), name="flash_" + mode,
    )(lam, q_arr, k_arr, v_arr, subln_g)


BAND_TQ = 256
BAND_TILES = 3
BAND_PAIRS = 4


def _band_kernel(q_ref, k0_ref, k1_ref, k2_ref, v0_ref, v1_ref, v2_ref, bias_ref, o_ref, *, scale):
    i = pl.program_id(2)
    lane = lax.broadcasted_iota(I32, (BAND_TQ, LANES), 1)
    low = lane < (LANES // 2)
    k_refs = (k0_ref, k1_ref, k2_ref)
    v_refs = (v0_ref, v1_ref, v2_ref)
    for pp in range(BAND_PAIRS):
        cols = slice(pp * LANES, (pp + 1) * LANES)
        q = q_ref[:, cols].astype(F32) * scale
        qs = (jnp.where(low, q, 0.0).astype(BF16), jnp.where(low, 0.0, q).astype(BF16))
        outs = []
        for h in range(2):
            scores = []
            for j in range(BAND_TILES):
                pen = jnp.where(i + j >= BAND_TILES - 1, 0.0, NEG).astype(F32)
                scores.append(_dot_nt(qs[h], k_refs[j][:, cols]) + bias_ref[pp, h, j] + pen)
            m = jnp.maximum(jnp.maximum(jnp.max(scores[0], axis=-1, keepdims=True),
                                        jnp.max(scores[1], axis=-1, keepdims=True)),
                            jnp.max(scores[2], axis=-1, keepdims=True))
            l = jnp.zeros_like(m)
            o = jnp.zeros((BAND_TQ, LANES), F32)
            for j in range(BAND_TILES):
                p = jnp.exp(scores[j] - m)
                l = l + jnp.sum(p, axis=-1, keepdims=True)
                o = o + _dot(p.astype(BF16), v_refs[j][:, cols])
            outs.append(o * (1.0 / l))
        o_ref[:, cols] = jnp.where(low, outs[0], outs[1]).astype(o_ref.dtype)


def band_bias_tiles(rel_bias):
    h = rel_bias.shape[0]
    n = BAND_TQ
    r = np.arange(n)[:, None]
    c = np.arange(n)[None, :]
    e_of = np.zeros(2 * n, np.int64)
    e_of[:n] = -np.arange(n)
    e_of[n + 1:] = n - 1 - np.arange(n - 1)
    tiles = []
    for j in range(BAND_TILES):
        rel = (BAND_TILES - 1 - j) * n + e_of
        gen = rel_bias[:, np.clip(rel, -A_MAX_REL, A_MAX_REL) + A_MAX_REL].astype(F32)
        skew = jnp.tile(gen, (1, n))[:, :n * (2 * n - 1)].reshape(h, n, 2 * n - 1)[:, :, :n]
        dist = (r // CHUNK) - (c // CHUNK) + (BAND_TILES - 1 - j) * (n // CHUNK)
        ok = (dist >= 0) & (dist <= A_LEFT_CHUNKS)
        tiles.append(jnp.where(ok[None], skew, NEG))
    t = jnp.stack(tiles, axis=1)
    return t.reshape(h // 2, 2, BAND_TILES, BAND_TQ, BAND_TQ)


def band_attention(proj, bias_tiles, *, q_blk0, k_blk0, v_blk0):
    b, s, _ = proj.shape
    n_pairs = bias_tiles.shape[0]
    width = BAND_PAIRS * LANES
    assert n_pairs % BAND_PAIRS == 0 and q_blk0 % BAND_PAIRS == 0
    assert k_blk0 % BAND_PAIRS == 0 and v_blk0 % BAND_PAIRS == 0
    grid = (n_pairs // BAND_PAIRS, b, s // BAND_TQ)

    def kv_spec(blk0, j):
        return pl.BlockSpec(
            (None, BAND_TQ, width),
            lambda p, bb, i: (bb, jnp.maximum(i + j - (BAND_TILES - 1), 0),
                              blk0 // BAND_PAIRS + p))

    in_specs = [pl.BlockSpec((None, BAND_TQ, width),
                             lambda p, bb, i: (bb, i, q_blk0 // BAND_PAIRS + p))]
    in_specs += [kv_spec(k_blk0, j) for j in range(BAND_TILES)]
    in_specs += [kv_spec(v_blk0, j) for j in range(BAND_TILES)]
    in_specs += [pl.BlockSpec((BAND_PAIRS, 2, BAND_TILES, BAND_TQ, BAND_TQ),
                              lambda p, bb, i: (p, 0, 0, 0, 0))]
    return pl.pallas_call(
        functools.partial(_band_kernel, scale=A_HEAD_DIM ** -0.5), grid=grid, in_specs=in_specs,
        out_specs=pl.BlockSpec((None, BAND_TQ, width), lambda p, bb, i: (bb, i, p)),
        out_shape=jax.ShapeDtypeStruct((b, s, n_pairs * LANES), BF16),
        compiler_params=_params("parallel", "parallel", "parallel"), name="band_attention",
    )(proj, proj, proj, proj, proj, proj, proj, bias_tiles)


def _mla_prep_kernel(cq_ref, ckv_ref, kr_ref, gq_ref, gkv_ref, wq_ref, wk_ref, wv_ref,
                     cq_cos_ref, cq_sin_ref, ck_cos_ref, ck_sin_ref, q_ref, k_ref, v_ref):
    half = B_ROPE_DIM // 2
    nq = _rms(cq_ref[...].astype(F32), gq_ref[...]).astype(BF16)
    nkv = _rms(ckv_ref[...].astype(F32), gkv_ref[...]).astype(BF16)
    q = _dot(nq, wq_ref[...])
    k = _dot(nkv, wk_ref[...])
    v_ref[...] = _dot(nkv, wv_ref[...]).astype(v_ref.dtype)
    kr = _rope_lanes(kr_ref[...].astype(F32), ck_cos_ref[...], ck_sin_ref[...], half)
    lane = lax.broadcasted_iota(I32, kr.shape, 1)
    kr = jnp.where(lane < B_ROPE_DIM, kr, 0.0)
    kr = pltpu.roll(kr, B_NOPE_DIM, 1)
    cos = cq_cos_ref[...]
    sin = cq_sin_ref[...]
    for c in range(q.shape[1] // LANES):
        sl = slice(c * LANES, (c + 1) * LANES)
        q_ref[:, sl] = _rope_lanes(q[:, sl], cos, sin, half).astype(q_ref.dtype)
        k_ref[:, sl] = (k[:, sl] + kr).astype(k_ref.dtype)


def mla_prep(proj, gq, gkv, wq, wk, wv, tables, *, tm, cq_blk, ckv_blk, kr_blk):
    t = proj.shape[0]
    nq = wq.shape[0]
    nkv = wk.shape[0]
    whole = lambda a: pl.BlockSpec(a.shape, lambda i: (0,) * a.ndim)
    tab = lambda k: pl.BlockSpec((None, tm, LANES), lambda i: (k, i, 0))
    in_specs = [
        pl.BlockSpec((tm, nq), lambda i: (i, cq_blk)),
        pl.BlockSpec((tm, nkv), lambda i: (i, ckv_blk)),
        pl.BlockSpec((tm, LANES), lambda i: (i, kr_blk)),
        whole(gq), whole(gkv), whole(wq), whole(wk), whole(wv),
        tab(0), tab(1), tab(2), tab(3),
    ]
    out_specs = [pl.BlockSpec((tm, wq.shape[1]), lambda i: (i, 0)),
                 pl.BlockSpec((tm, wk.shape[1]), lambda i: (i, 0)),
                 pl.BlockSpec((tm, wv.shape[1]), lambda i: (i, 0))]
    out_shape = [jax.ShapeDtypeStruct((t, wq.shape[1]), BF16),
                 jax.ShapeDtypeStruct((t, wk.shape[1]), BF16),
                 jax.ShapeDtypeStruct((t, wv.shape[1]), BF16)]
    return pl.pallas_call(
        _mla_prep_kernel, grid=(t // tm,), in_specs=in_specs, out_specs=out_specs,
        out_shape=out_shape, compiler_params=_params("parallel"), name="mla_prep",
    )(proj, proj, proj, gq, gkv, wq, wk, wv, tables, tables, tables, tables)


def _router_kernel(h_ref, g_ref, wr_ref, gate_ref, idx_ref, cnt_ref, npk_ref, carry_ref):
    tb = h_ref.shape[0]

    @pl.when(pl.program_id(0) == 0)
    def _():
        carry_ref[...] = jnp.zeros_like(carry_ref)

    n = _rms(h_ref[...], g_ref[...])
    npk_ref[...] = _pack_bf16_pairs(n)
    n_hi = n.astype(BF16)
    n_lo = (n - n_hi.astype(F32)).astype(BF16)
    logits = _dot(n_hi, wr_ref[0]) + (_dot(n_hi, wr_ref[1]) + _dot(n_lo, wr_ref[0]))
    lane = lax.broadcasted_iota(I32, (tb, LANES), 1).astype(F32)
    neg_inf = jnp.float32(-jnp.inf)
    lg = jnp.where(lane < N_EXPERTS, logits, neg_inf)
    m1 = jnp.max(lg, axis=-1, keepdims=True)
    e0 = jnp.min(jnp.where(lg == m1, lane, float(LANES)), axis=-1, keepdims=True)
    lg2 = jnp.where(lane == e0, neg_inf, lg)
    m2 = jnp.max(lg2, axis=-1, keepdims=True)
    e1 = jnp.min(jnp.where(lg2 == m2, lane, float(LANES)), axis=-1, keepdims=True)
    t = jnp.exp(m2 - m1)
    g0 = 1.0 / (1.0 + t)
    g1 = t / (1.0 + t)

    sel = ((lane == e0) | (lane == e1)).astype(F32)
    row = lax.broadcasted_iota(I32, (tb, tb), 0)
    col = lax.broadcasted_iota(I32, (tb, tb), 1)
    strict_lower = (col < row).astype(BF16)
    carry = carry_ref[...]
    excl = _dot(strict_lower, sel.astype(BF16)) + carry
    r0 = jnp.sum(jnp.where(lane == e0, excl, 0.0), axis=-1, keepdims=True)
    r1 = jnp.sum(jnp.where(lane == e1, excl, 0.0), axis=-1, keepdims=True)
    carry = carry + jnp.sum(sel, axis=0, keepdims=True)
    carry_ref[...] = carry

    gate_ref[...] = jnp.where(lane == 0, g0, jnp.where(lane == 1, g1, 0.0))
    idx_ref[...] = jnp.where(
        lane == 0, e0, jnp.where(lane == 1, e1, jnp.where(
            lane == 2, r0, jnp.where(lane == 3, r1, 0.0)))).astype(I32)
    cnt_ref[...] = jnp.broadcast_to(carry, cnt_ref.shape).astype(I32)


def router(h, g, wr_split, *, tb):
    t, d = h.shape
    return pl.pallas_call(
        _router_kernel, grid=(t // tb,),
        in_specs=[pl.BlockSpec((tb, d), lambda i: (i, 0)),
                  pl.BlockSpec((1, d), lambda i: (0, 0)),
                  pl.BlockSpec((2, d, LANES), lambda i: (0, 0, 0))],
        out_specs=[pl.BlockSpec((tb, LANES), lambda i: (i, 0)),
                   pl.BlockSpec((tb, LANES), lambda i: (i, 0)),
                   pl.BlockSpec((8, LANES), lambda i: (0, 0)),
                   pl.BlockSpec((tb, d // 2), lambda i: (i, 0))],
        out_shape=[jax.ShapeDtypeStruct((t, LANES), F32),
                   jax.ShapeDtypeStruct((t, LANES), I32),
                   jax.ShapeDtypeStruct((8, LANES), I32),
                   jax.ShapeDtypeStruct((t, d // 2), U32)],
        scratch_shapes=[pltpu.VMEM((1, LANES), F32)],
        compiler_params=_params("arbitrary"), name="router",
    )(h, g, wr_split)


def _row_copy(src_hbm, row, dst_ref, r, sem):
    return pltpu.make_async_copy(src_hbm.at[pl.ds(row, 1), :], dst_ref.at[pl.ds(r, 1), :], sem)


ROW_UNROLL = 8


def _for_rows(rows, body):
    def outer(o, carry):
        for u in range(ROW_UNROLL):
            body(o * ROW_UNROLL + u, u)
        return carry

    lax.fori_loop(0, rows // ROW_UNROLL, outer, 0)


SCATTER_SLOTS = 3


def _scatter_kernel(p0_ref, p1_ref, x_hbm, init_hbm, o_hbm, buf_ref, blk_sem, row_sem):
    del init_hbm
    i = pl.program_id(0)
    n = pl.num_programs(0)
    rows = p0_ref.shape[-1]
    slot = lax.rem(i, SCATTER_SLOTS)
    nxt = lax.rem(i + 1, SCATTER_SLOTS)

    def block_read(s, step):
        return pltpu.make_async_copy(x_hbm.at[pl.ds(step * rows, rows), :], buf_ref.at[s],
                                     blk_sem.at[s])

    def row_out(s, r, dst_row):
        return pltpu.make_async_copy(buf_ref.at[s, pl.ds(r, 1), :], o_hbm.at[pl.ds(dst_row, 1), :],
                                     row_sem.at[s])

    def start_rows(s):
        def start(r, u):
            row_out(s, r, p0_ref[0, 0, r]).start(priority=0)
            row_out(s, r, p1_ref[0, 0, r]).start(priority=1)
        _for_rows(rows, start)

    def wait_rows(s):
        def wait(r, u):
            row_out(s, r, 0).wait()
            row_out(s, r, 0).wait()
        _for_rows(rows, wait)

    @pl.when(i == 0)
    def _():
        block_read(0, 0).start()

    @pl.when(i >= 2)
    def _():
        wait_rows(nxt)

    @pl.when(i + 1 < n)
    def _():
        block_read(nxt, i + 1).start()

    block_read(slot, i).wait()
    start_rows(slot)

    @pl.when(i == n - 1)
    def _():
        @pl.when(i >= 1)
        def _():
            wait_rows(lax.rem(i + 2, SCATTER_SLOTS))
        wait_rows(slot)


def scatter_rows(x, pos0, pos1, n_out, *, rows):
    t, w = x.shape
    n = t // rows
    smem = lambda: pl.BlockSpec((1, 1, rows), lambda i: (i, 0, 0), memory_space=pltpu.SMEM)
    init = jnp.zeros((n_out, w), x.dtype)
    return pl.pallas_call(
        _scatter_kernel, grid=(n,),
        in_specs=[smem(), smem(), pl.BlockSpec(memory_space=pl.ANY),
                  pl.BlockSpec(memory_space=pl.ANY)],
        out_specs=pl.BlockSpec(memory_space=pl.ANY),
        out_shape=jax.ShapeDtypeStruct((n_out, w), x.dtype),
        input_output_aliases={3: 0},
        scratch_shapes=[pltpu.VMEM((SCATTER_SLOTS, rows, w), x.dtype),
                        pltpu.SemaphoreType.DMA((SCATTER_SLOTS,)),
                        pltpu.SemaphoreType.DMA((SCATTER_SLOTS,))],
        compiler_params=_params("arbitrary"), name="scatter_rows",
    )(pos0.reshape(n, 1, rows), pos1.reshape(n, 1, rows), x, init)


def _combine_kernel(p0_first, p1_first, p0_next, p1_next, h_ref, gate_ref, ys_hbm, *rest):
    norm_ref = rest[0] if len(rest) == 4 else None
    o_ref, buf_ref, sem = rest[-3:]
    i = pl.program_id(0)
    n = pl.num_programs(0)
    rows = h_ref.shape[0]
    slot = i & 1

    def start_into(s, p0_ref, p1_ref):
        def start(r, u):
            _row_copy(ys_hbm, p0_ref[0, 0, r], buf_ref.at[s, 0], r, sem.at[s]).start(priority=0)
            _row_copy(ys_hbm, p1_ref[0, 0, r], buf_ref.at[s, 1], r, sem.at[s]).start(priority=1)
        _for_rows(rows, start)

    @pl.when(i == 0)
    def _():
        start_into(0, p0_first, p1_first)

    @pl.when(i + 1 < n)
    def _():
        start_into(1 - slot, p0_next, p1_next)

    def wait(r, u):
        _row_copy(ys_hbm, 0, buf_ref.at[slot, 0], r, sem.at[slot]).wait()
        _row_copy(ys_hbm, 0, buf_ref.at[slot, 1], r, sem.at[slot]).wait()

    _for_rows(rows, wait)
    gates = gate_ref[...]
    half = buf_ref.shape[-1]
    a0, b0 = _unpack_bf16_pairs(buf_ref[slot, 0])
    a1, b1 = _unpack_bf16_pairs(buf_ref[slot, 1])
    lo = h_ref[:, :half] + gates[:, 0:1] * a0 + gates[:, 1:2] * a1
    hi = h_ref[:, half:] + gates[:, 0:1] * b0 + gates[:, 1:2] * b1
    if norm_ref is not None:
        ms = (jnp.sum(lo * lo, axis=-1, keepdims=True)
              + jnp.sum(hi * hi, axis=-1, keepdims=True)) * (1.0 / (2 * half))
        inv = lax.rsqrt(ms + NORM_EPS)
        lo = lo * inv * norm_ref[:, :half]
        hi = hi * inv * norm_ref[:, half:]
    o_ref[:, :half] = lo
    o_ref[:, half:] = hi


def combine_rows(h, gates, ys, pos0, pos1, *, rows, norm_g=None):
    t, d = h.shape
    n = t // rows
    first = lambda: pl.BlockSpec((1, 1, rows), lambda i: (0, 0, 0), memory_space=pltpu.SMEM)
    nxt = lambda: pl.BlockSpec((1, 1, rows), lambda i: (jnp.minimum(i + 1, n - 1), 0, 0),
                               memory_space=pltpu.SMEM)
    pos0 = pos0.reshape(n, 1, rows)
    pos1 = pos1.reshape(n, 1, rows)
    in_specs = [first(), first(), nxt(), nxt(),
                pl.BlockSpec((rows, d), lambda i: (i, 0)),
                pl.BlockSpec((rows, LANES), lambda i: (i, 0)),
                pl.BlockSpec(memory_space=pl.ANY)]
    operands = [pos0, pos1, pos0, pos1, h, gates, ys]
    if norm_g is not None:
        in_specs.append(pl.BlockSpec((1, d), lambda i: (0, 0)))
        operands.append(norm_g)
    return pl.pallas_call(
        _combine_kernel, grid=(n,), in_specs=in_specs,
        out_specs=pl.BlockSpec((rows, d), lambda i: (i, 0)),
        out_shape=jax.ShapeDtypeStruct((t, d), F32),
        scratch_shapes=[pltpu.VMEM((2, 2, rows, ys.shape[1]), ys.dtype),
                        pltpu.SemaphoreType.DMA((2,))],
        compiler_params=_params("arbitrary"), name="combine_rows",
    )(*operands)


def _rmsnorm_kernel(x_ref, g_ref, o_ref):
    o_ref[...] = _rms(x_ref[...], g_ref[...])


def rmsnorm(x, g, *, tm):
    t, d = x.shape
    return pl.pallas_call(
        _rmsnorm_kernel, grid=(t // tm,),
        in_specs=[pl.BlockSpec((tm, d), lambda i: (i, 0)), pl.BlockSpec((1, d), lambda i: (0, 0))],
        out_specs=pl.BlockSpec((tm, d), lambda i: (i, 0)),
        out_shape=jax.ShapeDtypeStruct((t, d), F32),
        compiler_params=_params("parallel"), name="final_rmsnorm",
    )(x, g)


def _rope_cos_sin(positions, dim):
    inv_freq = 1.0 / jnp.power(ROPE_THETA, jnp.arange(0, dim, 2, dtype=F32) / dim)
    ang = positions.reshape(-1).astype(F32)[:, None] * inv_freq
    return jnp.cos(ang), jnp.sin(ang)


def _diff_tables(positions):
    cos, sin = _rope_cos_sin(positions, C_HEAD_DIM)
    cos_l = jnp.tile(cos, (1, 4))
    sin_l = jnp.concatenate([-sin, -sin, sin, sin], axis=1)
    k_tab = jnp.stack([cos_l, sin_l])
    return jnp.stack([k_tab * (C_HEAD_DIM ** -0.5 * LOG2E), k_tab])


def _diff_head_order(w, n_cols):
    d = w.shape[0]
    half = C_HEAD_DIM // 2
    head = w[:, :n_cols].reshape(d, n_cols // LANES, 2, 2, half)
    head = head.transpose(0, 1, 3, 2, 4).reshape(d, n_cols)
    return jnp.concatenate([head, w[:, n_cols:]], axis=1)


def _mla_tables(positions):
    cos, sin = _rope_cos_sin(positions, B_ROPE_DIM)
    t = cos.shape[0]
    scale = (B_NOPE_DIM + B_ROPE_DIM) ** -0.5 * LOG2E
    ones = lambda n: jnp.ones((t, n), F32)
    zeros = lambda n: jnp.zeros((t, n), F32)
    q_cos = jnp.concatenate([ones(B_NOPE_DIM), cos, cos, ones(32)], axis=1) * scale
    q_sin = jnp.concatenate([zeros(B_NOPE_DIM), -sin, sin, zeros(32)], axis=1) * scale
    k_cos = jnp.concatenate([cos, cos, ones(96)], axis=1)
    k_sin = jnp.concatenate([-sin, sin, zeros(96)], axis=1)
    return jnp.stack([q_cos, q_sin, k_cos, k_sin])


def _routing_plan(idx_out, counts, *, tm, n_tiles):
    e0, e1, r0, r1 = idx_out[:, 0], idx_out[:, 1], idx_out[:, 2], idx_out[:, 3]
    padded = ((counts + tm - 1) // tm) * tm
    ends = jnp.cumsum(padded)
    offs = ends - padded
    experts = jnp.arange(N_EXPERTS, dtype=I32)[None, :]
    pos0 = jnp.sum(jnp.where(e0[:, None] == experts, offs[None, :], 0), axis=1) + r0
    pos1 = jnp.sum(jnp.where(e1[:, None] == experts, offs[None, :], 0), axis=1) + r1
    n_valid = (ends[-1] // tm).astype(I32)
    tile = jnp.minimum(jnp.arange(n_tiles, dtype=I32), n_valid - 1)
    tile_expert = jnp.minimum(jnp.sum((tile * tm)[:, None] >= ends[None, :], axis=1),
                              N_EXPERTS - 1)
    return pos0.astype(I32), pos1.astype(I32), tile_expert.astype(I32), n_valid.reshape(1)


def _pad_cols(w, n):
    return jnp.pad(w, ((0, 0), (0, n - w.shape[1])))


def kernel(x, positions, mix_norm_g, ffn_norm_g, w_in_even, rel_bias_a, q_norm_b, w_uq_b, kv_norm_b, w_ukv_b, w_out_even, w_in_odd, lambda_q1, lambda_k1, lambda_q2, lambda_k2, subln_g, w_out_odd, w_gu_dense, w_down_dense, w_router, w_gu_moe, w_down_moe, final_norm_g):
    b, s, d = x.shape
    t = b * s
    depth = mix_norm_g.shape[0]
    a_width = 4 * LANES
    tm = 512
    tm_moe = 512
    n_tiles_moe = (2 * t) // tm_moe + N_EXPERTS

    w_gu_all = w_gu_moe.reshape((-1,) + w_gu_moe.shape[2:]).astype(BF16)
    w_down_all = w_down_moe.reshape((-1,) + w_down_moe.shape[2:]).astype(BF16)
    diff_tab = _diff_tables(positions)
    mla_tab = _mla_tables(positions)
    row = lambda v: v.reshape(1, -1).astype(F32)

    h = x.reshape(t, d).astype(F32)
    for layer in range(depth):
        i = layer // 2
        if layer % 2 == 0:
            w_in = _pad_cols(w_in_even[i], 2048).astype(BF16)
            proj = rms_matmul(h, row(mix_norm_g[layer]), w_in, tm=tm, tn=1024)
            proj3 = proj.reshape(b, s, -1)
            out_a = band_attention(proj3, band_bias_tiles(rel_bias_a[i]),
                                   q_blk0=0, k_blk0=4, v_blk0=8)
            n_heads = w_uq_b.shape[2] // (B_NOPE_DIM + B_ROPE_DIM)
            wq = jnp.pad(w_uq_b[i].reshape(-1, n_heads, B_NOPE_DIM + B_ROPE_DIM),
                         ((0, 0), (0, 0), (0, 32))).reshape(-1, n_heads * LANES).astype(BF16)
            wkv = w_ukv_b[i].reshape(-1, n_heads, B_NOPE_DIM + B_V_DIM)
            wk = jnp.pad(wkv[:, :, :B_NOPE_DIM], ((0, 0), (0, 0), (0, LANES - B_NOPE_DIM)))
            wk = wk.reshape(-1, n_heads * LANES).astype(BF16)
            wv = wkv[:, :, B_NOPE_DIM:].reshape(-1, n_heads * B_V_DIM).astype(BF16)
            q_b, k_b, v_b = mla_prep(proj, row(q_norm_b[i]), row(kv_norm_b[i]), wq, wk, wv, mla_tab,
                                     tm=tm, cq_blk=6, ckv_blk=14, kr_blk=15)
            out_b = flash_attention(q_b.reshape(b, s, -1), k_b.reshape(b, s, -1),
                                    v_b.reshape(b, s, -1), mode="mla", q_blk0=0, k_blk0=0,
                                    v_blk0=0, n_groups=n_heads // 2, tq=512, groups=2)
            w_out = w_out_even[i].astype(BF16)
            h = matmul_residual([out_a.reshape(t, -1), out_b.reshape(t, -1)],
                                [w_out[:a_width], w_out[a_width:]], h, tm=tm)
            n_tiles = t // tm
            h = grouped_swiglu(h, row(ffn_norm_g[layer]), w_gu_dense.astype(BF16),
                               w_down_dense.astype(BF16), jnp.full((n_tiles,), i, I32),
                               jnp.full((1,), n_tiles, I32), tm=tm, tf=1408, add_res=True)
        else:
            c_width = w_out_odd.shape[1]
            n_heads = c_width // LANES
            w_in = _diff_head_order(w_in_odd[i], 2 * c_width).astype(BF16)
            qkv = rms_matmul(h, row(mix_norm_g[layer]), w_in, tm=tm, tn=1024,
                             rope=(diff_tab, c_width // 1024, LANES // 2))
            lam_init = 0.8 - 0.6 * math.exp(-0.3 * layer)
            lam = (jnp.exp(jnp.sum(lambda_q1[i].astype(F32) * lambda_k1[i].astype(F32)))
                   - jnp.exp(jnp.sum(lambda_q2[i].astype(F32) * lambda_k2[i].astype(F32)))
                   + lam_init).reshape(1).astype(F32)
            qkv3 = qkv.reshape(b, s, -1)
            out_c = flash_attention(qkv3, qkv3, qkv3, mode="diff", q_blk0=0, k_blk0=n_heads,
                                    v_blk0=2 * n_heads, n_groups=n_heads, tq=512, groups=2, lam=lam,
                                    subln_g=row(subln_g[i]), out_scale=1.0 - lam_init)
            h = matmul_residual([out_c.reshape(t, -1)], [w_out_odd[i].astype(BF16)], h, tm=tm)

            g_ffn = row(ffn_norm_g[layer])
            wr = _pad_cols(w_router[i].astype(F32), LANES)
            wr_hi = wr.astype(BF16)
            wr_split = jnp.stack([wr_hi, (wr - wr_hi.astype(F32)).astype(BF16)])
            gates, idx_out, cnt, n_packed = router(h, g_ffn, wr_split, tb=512)
            pos0, pos1, tile_expert, n_valid = _routing_plan(
                idx_out, cnt[0, :N_EXPERTS], tm=tm_moe, n_tiles=n_tiles_moe)
            xs = scatter_rows(n_packed, pos0, pos1, n_tiles_moe * tm_moe, rows=256)
            ys = grouped_swiglu(xs, None, w_gu_all, w_down_all, tile_expert + i * N_EXPERTS,
                                n_valid, tm=tm_moe, tf=1792)
            last = layer == depth - 1
            h = combine_rows(h, gates, ys, pos0, pos1, rows=256,
                             norm_g=row(final_norm_g) if last else None)

    if depth % 2 == 1:
        h = rmsnorm(h, row(final_norm_g), tm=tm)
    return h.reshape(b, s, d)
```

```python
import functools
import math

import jax
import jax.numpy as jnp
import numpy as np
from jax import lax
from jax.experimental import pallas as pl
from jax.experimental.pallas import tpu as pltpu

F32 = jnp.float32
BF16 = jnp.bfloat16
I32 = jnp.int32
U32 = jnp.uint32
HI16 = 0xFFFF0000

NORM_EPS = 1e-6
ROPE_THETA = 10000.0
NEG = -1e30
LOG2E = math.log2(math.e)
LANES = 128
CHUNK = 64

A_LEFT_CHUNKS = 8
A_MAX_REL = 256
A_HEAD_DIM = 64
B_NOPE_DIM = 64
B_ROPE_DIM = 32
B_V_DIM = 64
C_HEAD_DIM = 64
N_EXPERTS = 8

VMEM_LIMIT = 56 * 1024 * 1024

ROW_TILE = 512
PROJ_COL_TILE = 1024
FF_STEPS = 2
FLASH_TQ = 512
FLASH_GROUPS = 2
DMA_ROWS = 256


def _params(*sem):
    return pltpu.CompilerParams(dimension_semantics=sem, vmem_limit_bytes=VMEM_LIMIT)


def _rms(x, g):
    ms = jnp.mean(x * x, axis=-1, keepdims=True)
    return x * lax.rsqrt(ms + NORM_EPS) * g


def _dot(a, b):
    return jnp.dot(a, b, preferred_element_type=F32)


def _dot_nt(a, b):
    return lax.dot_general(a, b, (((1,), (1,)), ((), ())), preferred_element_type=F32)


def _rope_lanes(x, cos, sin, half):
    if 2 * half == LANES:
        swapped = pltpu.roll(x, half, 1)
    else:
        lane = lax.broadcasted_iota(I32, x.shape, 1)
        fwd = pltpu.roll(x, LANES - half, 1)
        bwd = pltpu.roll(x, half, 1)
        swapped = jnp.where((lane & half) == 0, fwd, bwd)
    return x * cos + swapped * sin


def _rms_matmul_kernel(x_ref, g_ref, w_ref, o_ref, xn_ref):
    @pl.when(pl.program_id(1) == 0)
    def _():
        xn_ref[...] = _rms(x_ref[...], g_ref[...]).astype(BF16)

    o_ref[...] = _dot(xn_ref[...], w_ref[...]).astype(o_ref.dtype)


def _rms_matmul_rope_kernel(x_ref, g_ref, w_ref, cos_ref, sin_ref, o_ref, xn_ref, *,
                            n_rope_tiles, half):
    j = pl.program_id(1)

    @pl.when(j == 0)
    def _():
        xn_ref[...] = _rms(x_ref[...], g_ref[...]).astype(BF16)

    acc = _dot(xn_ref[...], w_ref[...])

    @pl.when(j < n_rope_tiles)
    def _():
        cos = cos_ref[...]
        sin = sin_ref[...]
        for c in range(acc.shape[1] // LANES):
            sl = slice(c * LANES, (c + 1) * LANES)
            o_ref[:, sl] = _rope_lanes(acc[:, sl], cos, sin, half).astype(o_ref.dtype)

    @pl.when(j >= n_rope_tiles)
    def _():
        o_ref[...] = acc.astype(o_ref.dtype)


def rms_matmul(x, g, w, *, tm, tn, rope=None):
    t, d = x.shape
    n = w.shape[1]
    grid = (t // tm, n // tn)
    x_spec = pl.BlockSpec((tm, d), lambda i, j: (i, 0))
    g_spec = pl.BlockSpec((1, d), lambda i, j: (0, 0))
    w_spec = pl.BlockSpec((d, tn), lambda i, j: (0, j))
    o_spec = pl.BlockSpec((tm, tn), lambda i, j: (i, j))
    scratch = [pltpu.VMEM((tm, d), BF16)]
    out_shape = jax.ShapeDtypeStruct((t, n), BF16)
    if rope is None:
        return pl.pallas_call(
            _rms_matmul_kernel, grid=grid, in_specs=[x_spec, g_spec, w_spec], out_specs=o_spec,
            out_shape=out_shape, scratch_shapes=scratch,
            compiler_params=_params("parallel", "arbitrary"), name="rms_matmul",
        )(x, g, w)
    tables, per_part, half = rope
    cos_spec = pl.BlockSpec((None, None, tm, LANES),
                            lambda i, j: (jnp.minimum(j // per_part, 1), 0, i, 0))
    sin_spec = pl.BlockSpec((None, None, tm, LANES),
                            lambda i, j: (jnp.minimum(j // per_part, 1), 1, i, 0))
    kern = functools.partial(_rms_matmul_rope_kernel, n_rope_tiles=2 * per_part, half=half)
    return pl.pallas_call(
        kern, grid=grid, in_specs=[x_spec, g_spec, w_spec, cos_spec, sin_spec], out_specs=o_spec,
        out_shape=out_shape, scratch_shapes=scratch,
        compiler_params=_params("parallel", "arbitrary"), name="rms_matmul_rope",
    )(x, g, w, tables, tables)


def _mm_res_kernel(*refs, n_in):
    res_ref = refs[2 * n_in]
    o_ref = refs[2 * n_in + 1]
    acc = res_ref[...]
    for k in range(n_in):
        acc = acc + _dot(refs[k][...], refs[n_in + k][...])
    o_ref[...] = acc


def matmul_residual(a_list, w_list, res, *, tm):
    t, n = res.shape
    n_in = len(a_list)
    in_specs = [pl.BlockSpec((tm, a.shape[1]), lambda i: (i, 0)) for a in a_list]
    in_specs += [pl.BlockSpec(w.shape, lambda i: (0, 0)) for w in w_list]
    in_specs += [pl.BlockSpec((tm, n), lambda i: (i, 0))]
    return pl.pallas_call(
        functools.partial(_mm_res_kernel, n_in=n_in), grid=(t // tm,), in_specs=in_specs,
        out_specs=pl.BlockSpec((tm, n), lambda i: (i, 0)),
        out_shape=jax.ShapeDtypeStruct((t, n), F32),
        compiler_params=_params("parallel"), name="matmul_residual",
    )(*a_list, *w_list, res)


def _pack_bf16_pairs(x):
    n = x.shape[1] // 2
    bits = lax.bitcast_convert_type(x.astype(BF16).astype(F32), U32)
    return (bits[:, :n] & U32(HI16)) | (bits[:, n:] >> 16)


def _unpack_bf16_pairs(w):
    first = lax.bitcast_convert_type(w & U32(HI16), F32)
    second = lax.bitcast_convert_type(w << 16, F32)
    return first, second


def _ffn_kernel(te_ref, nv_ref, x_ref, g_ref, wg_ref, wu_ref, wd_ref, o_ref, xn_ref, acc_ref, *,
                add_res):
    del te_ref
    i = pl.program_id(0)
    f = pl.program_id(1)
    nf = pl.num_programs(1)
    valid = i < nv_ref[0]

    @pl.when(valid & (f == 0))
    def _():
        xn_ref[...] = _rms(x_ref[...], g_ref[...]).astype(BF16)
        acc_ref[...] = jnp.zeros_like(acc_ref)

    _ffn_body(valid, wg_ref, wu_ref, wd_ref, xn_ref, acc_ref)

    @pl.when(valid & (f == nf - 1))
    def _():
        if add_res:
            o_ref[...] = x_ref[...] + acc_ref[...]
        else:
            o_ref[...] = acc_ref[...]

    @pl.when(jnp.logical_not(valid) & (f == nf - 1))
    def _():
        o_ref[...] = jnp.zeros_like(o_ref)


def _ffn_body(valid, wg_ref, wu_ref, wd_ref, xn_ref, acc_ref):
    @pl.when(valid)
    def _():
        xn = xn_ref[...]
        gate = _dot(xn, wg_ref[...])
        up = _dot(xn, wu_ref[...])
        act = (gate * jax.nn.sigmoid(gate) * up).astype(BF16)
        acc_ref[...] += _dot(act, wd_ref[...])


def _ffn_packed_kernel(te_ref, nv_ref, x_ref, wg_ref, wu_ref, wd_ref, o_ref, xn_ref, acc_ref):
    del te_ref
    i = pl.program_id(0)
    f = pl.program_id(1)
    nf = pl.num_programs(1)
    valid = i < nv_ref[0]
    half = x_ref.shape[1]

    @pl.when(valid & (f == 0))
    def _():
        first, second = _unpack_bf16_pairs(x_ref[...])
        xn_ref[:, :half] = first.astype(BF16)
        xn_ref[:, half:] = second.astype(BF16)
        acc_ref[...] = jnp.zeros_like(acc_ref)

    _ffn_body(valid, wg_ref, wu_ref, wd_ref, xn_ref, acc_ref)

    @pl.when(valid & (f == nf - 1))
    def _():
        o_ref[...] = _pack_bf16_pairs(acc_ref[...])

    @pl.when(jnp.logical_not(valid) & (f == nf - 1))
    def _():
        o_ref[...] = jnp.zeros_like(o_ref)


def grouped_swiglu(x, g, w_gu, w_down, tile_expert, n_valid, *, tm, tf, add_res=False):
    packed = g is None
    assert not (packed and add_res)
    p = x.shape[0]
    d = w_gu.shape[1]
    ff = w_down.shape[1]
    nf = ff // tf
    grid = (p // tm, nf)

    def f_eff(i, f, nv):
        return jnp.where(i < nv[0], f, nf - 1)

    x_spec = pl.BlockSpec((tm, x.shape[1]), lambda i, f, te, nv: (i, 0))
    w_specs = [
        pl.BlockSpec((None, d, tf), lambda i, f, te, nv: (te[i], 0, f_eff(i, f, nv))),
        pl.BlockSpec((None, d, tf), lambda i, f, te, nv: (te[i], 0, nf + f_eff(i, f, nv))),
        pl.BlockSpec((None, tf, d), lambda i, f, te, nv: (te[i], f_eff(i, f, nv), 0)),
    ]
    if packed:
        kern, in_specs, operands = _ffn_packed_kernel, [x_spec] + w_specs, (x,)
    else:
        kern = functools.partial(_ffn_kernel, add_res=add_res)
        in_specs = [x_spec, pl.BlockSpec((1, d), lambda i, f, te, nv: (0, 0))] + w_specs
        operands = (x, g)
    grid_spec = pltpu.PrefetchScalarGridSpec(
        num_scalar_prefetch=2, grid=grid, in_specs=in_specs,
        out_specs=pl.BlockSpec((tm, x.shape[1]), lambda i, f, te, nv: (i, 0)),
        scratch_shapes=[pltpu.VMEM((tm, d), BF16), pltpu.VMEM((tm, d), F32)],
    )
    return pl.pallas_call(
        kern, grid_spec=grid_spec, out_shape=jax.ShapeDtypeStruct(x.shape, x.dtype),
        compiler_params=_params("parallel", "arbitrary"), name="grouped_swiglu",
    )(tile_expert, n_valid, *operands, w_gu, w_gu, w_down)


def _flash_kernel(lam_ref, q_ref, k_ref, v_ref, g_ref, o_ref, m_ref, l_ref, acc_ref, vt_ref,
                  s_ref, *, mode, tq, groups, out_scale):
    i = pl.program_id(2)
    s_len = v_ref.shape[0]
    wqk = q_ref.shape[1] // groups
    n_maps = 2 * groups
    sub = lax.broadcasted_iota(I32, (LANES, tq), 0)
    top = sub < (LANES // 2)

    @pl.when(i == 0)
    def _():
        for c in range(s_len // tq):
            sl = slice(c * tq, (c + 1) * tq)
            vt_ref[:, sl] = v_ref[sl, :].T

    qts = []
    for g in range(groups):
        if mode == "mla":
            qts += [q_ref[:, g * wqk:g * wqk + LANES].T, q_ref[:, g * wqk + LANES:(g + 1) * wqk].T]
        else:
            qt = q_ref[:, g * wqk:(g + 1) * wqk].T
            zero = jnp.zeros_like(qt)
            map1 = (sub & (C_HEAD_DIM // 2)) == 0
            qts += [jnp.where(map1, qt, zero), jnp.where(map1, zero, qt)]

    m_ref[...] = jnp.full_like(m_ref, NEG)
    l_ref[...] = jnp.zeros_like(l_ref)
    acc_ref[...] = jnp.zeros_like(acc_ref)

    def k_tile(mi, start):
        g, idx = divmod(mi, 2)
        if mode == "mla":
            return k_ref[pl.ds(start, tq), g * wqk + idx * LANES:g * wqk + (idx + 1) * LANES]
        return k_ref[pl.ds(start, tq), g * wqk:(g + 1) * wqk]

    def update(mi, s, vt_tile):
        m_prev = m_ref[mi]
        m_new = jnp.maximum(m_prev, jnp.max(s, axis=0, keepdims=True))
        alpha = jnp.exp2(m_prev - m_new)
        p = jnp.exp2(s - m_new)
        l_ref[mi] = alpha * l_ref[mi] + jnp.sum(p, axis=0, keepdims=True)
        acc_ref[mi] = alpha * acc_ref[mi] + _dot(vt_tile, p.astype(BF16))
        m_ref[mi] = m_new

    def scores_into(slot, t):
        start = pl.multiple_of(t * tq, tq)
        for mi in range(n_maps):
            s_ref[slot, mi] = _dot(k_tile(mi, start), qts[mi])

    def consume(slot, t, masked):
        start = pl.multiple_of(t * tq, tq)
        if masked:
            row = lax.broadcasted_iota(I32, (tq, tq), 0)
            col = lax.broadcasted_iota(I32, (tq, tq), 1)
            shift = CHUNK.bit_length() - 1
            visible = (row >> shift) <= (col >> shift)
        for mi in range(n_maps):
            g = mi // 2
            s = s_ref[slot, mi]
            if masked:
                s = jnp.where(visible, s, NEG)
            update(mi, s, vt_ref[g * LANES:(g + 1) * LANES, pl.ds(start, tq)])

    scores_into(0, 0)

    def tile_pair(u, carry):
        t = 2 * u
        scores_into(1, t + 1)
        consume(0, t, False)
        scores_into(0, t + 2)
        consume(1, t + 1, False)
        return carry

    lax.fori_loop(0, i >> 1, tile_pair, 0)

    @pl.when((i & 1) == 1)
    def _():
        scores_into(1, i)
        consume(0, i - 1, False)
        consume(1, i, True)

    @pl.when((i & 1) == 0)
    def _():
        consume(0, i, True)

    for g in range(groups):
        o0 = acc_ref[2 * g] * (1.0 / l_ref[2 * g])
        o1 = acc_ref[2 * g + 1] * (1.0 / l_ref[2 * g + 1])
        cols = slice(g * LANES, (g + 1) * LANES)
        if mode == "mla":
            o_ref[:, cols] = jnp.where(top, o0, o1).T.astype(o_ref.dtype)
        else:
            o = (o0 - lam_ref[0] * o1).T
            o_ref[:, cols] = (_rms(o, g_ref[...]) * out_scale).astype(o_ref.dtype)


def flash_attention(q_arr, k_arr, v_arr, *, mode, q_blk0, k_blk0, v_blk0, n_groups, tq,
                    groups=1, lam=None, subln_g=None, out_scale=1.0):
    b, s, _ = q_arr.shape
    wqk = (2 * LANES if mode == "mla" else LANES) * groups
    wv = LANES * groups
    assert n_groups % groups == 0 and q_blk0 % groups == 0 and k_blk0 % groups == 0
    assert v_blk0 % groups == 0
    if lam is None:
        lam = jnp.zeros((1,), F32)
    if subln_g is None:
        subln_g = jnp.ones((1, LANES), F32)
    grid = (b, n_groups // groups, s // tq)
    qb, kb, vb = q_blk0 // groups, k_blk0 // groups, v_blk0 // groups
    in_specs = [
        pl.BlockSpec((None, tq, wqk), lambda bb, g, i, lam_r: (bb, i, qb + g)),
        pl.BlockSpec((None, s, wqk), lambda bb, g, i, lam_r: (bb, 0, kb + g)),
        pl.BlockSpec((None, s, wv), lambda bb, g, i, lam_r: (bb, 0, vb + g)),
        pl.BlockSpec((1, LANES), lambda bb, g, i, lam_r: (0, 0)),
    ]
    n_maps = 2 * groups
    grid_spec = pltpu.PrefetchScalarGridSpec(
        num_scalar_prefetch=1, grid=grid, in_specs=in_specs,
        out_specs=pl.BlockSpec((None, tq, wv), lambda bb, g, i, lam_r: (bb, i, g)),
        scratch_shapes=[pltpu.VMEM((n_maps, 1, tq), F32), pltpu.VMEM((n_maps, 1, tq), F32),
                        pltpu.VMEM((n_maps, LANES, tq), F32), pltpu.VMEM((wv, s), BF16),
                        pltpu.VMEM((2, n_maps, tq, tq), F32)],
    )
    kern = functools.partial(_flash_kernel, mode=mode, tq=tq, groups=groups, out_scale=out_scale)
    return pl.pallas_call(
        kern, grid_spec=grid_spec,
        out_shape=jax.ShapeDtypeStruct((b, s, n_groups * LANES), BF16),
        compiler_params=_params("parallel", "parallel", "arbitrary"), name="flash_" + mode,
    )(lam, q_arr, k_arr, v_arr, subln_g)


BAND_TQ = 256
BAND_TILES = 3
BAND_PAIRS = 4


def _band_kernel(q_ref, k0_ref, k1_ref, k2_ref, v0_ref, v1_ref, v2_ref, bias_ref, o_ref, *, scale):
    i = pl.program_id(2)
    lane = lax.broadcasted_iota(I32, (BAND_TQ, LANES), 1)
    low = lane < (LANES // 2)
    k_refs = (k0_ref, k1_ref, k2_ref)
    v_refs = (v0_ref, v1_ref, v2_ref)
    for pp in range(BAND_PAIRS):
        cols = slice(pp * LANES, (pp + 1) * LANES)
        q = q_ref[:, cols].astype(F32) * scale
        qs = (jnp.where(low, q, 0.0).astype(BF16), jnp.where(low, 0.0, q).astype(BF16))
        outs = []
        for h in range(2):
            scores = []
            for j in range(BAND_TILES):
                pen = jnp.where(i + j >= BAND_TILES - 1, 0.0, NEG).astype(F32)
                scores.append(_dot_nt(qs[h], k_refs[j][:, cols]) + bias_ref[pp, h, j] + pen)
            m = jnp.maximum(jnp.maximum(jnp.max(scores[0], axis=-1, keepdims=True),
                                        jnp.max(scores[1], axis=-1, keepdims=True)),
                            jnp.max(scores[2], axis=-1, keepdims=True))
            l = jnp.zeros_like(m)
            o = jnp.zeros((BAND_TQ, LANES), F32)
            for j in range(BAND_TILES):
                p = jnp.exp(scores[j] - m)
                l = l + jnp.sum(p, axis=-1, keepdims=True)
                o = o + _dot(p.astype(BF16), v_refs[j][:, cols])
            outs.append(o * (1.0 / l))
        o_ref[:, cols] = jnp.where(low, outs[0], outs[1]).astype(o_ref.dtype)


def band_bias_tiles(rel_bias):
    h = rel_bias.shape[0]
    n = BAND_TQ
    r = np.arange(n)[:, None]
    c = np.arange(n)[None, :]
    e_of = np.zeros(2 * n, np.int64)
    e_of[:n] = -np.arange(n)
    e_of[n + 1:] = n - 1 - np.arange(n - 1)
    tiles = []
    for j in range(BAND_TILES):
        rel = (BAND_TILES - 1 - j) * n + e_of
        gen = rel_bias[:, np.clip(rel, -A_MAX_REL, A_MAX_REL) + A_MAX_REL].astype(F32)
        skew = jnp.tile(gen, (1, n))[:, :n * (2 * n - 1)].reshape(h, n, 2 * n - 1)[:, :, :n]
        dist = (r // CHUNK) - (c // CHUNK) + (BAND_TILES - 1 - j) * (n // CHUNK)
        ok = (dist >= 0) & (dist <= A_LEFT_CHUNKS)
        tiles.append(jnp.where(ok[None], skew, NEG))
    t = jnp.stack(tiles, axis=1)
    return t.reshape(h // 2, 2, BAND_TILES, BAND_TQ, BAND_TQ)


def band_attention(proj, bias_tiles, *, q_blk0, k_blk0, v_blk0):
    b, s, _ = proj.shape
    n_pairs = bias_tiles.shape[0]
    width = BAND_PAIRS * LANES
    assert n_pairs % BAND_PAIRS == 0 and q_blk0 % BAND_PAIRS == 0
    assert k_blk0 % BAND_PAIRS == 0 and v_blk0 % BAND_PAIRS == 0
    grid = (n_pairs // BAND_PAIRS, b, s // BAND_TQ)

    def kv_spec(blk0, j):
        return pl.BlockSpec(
            (None, BAND_TQ, width),
            lambda p, bb, i: (bb, jnp.maximum(i + j - (BAND_TILES - 1), 0),
                              blk0 // BAND_PAIRS + p))

    in_specs = [pl.BlockSpec((None, BAND_TQ, width),
                             lambda p, bb, i: (bb, i, q_blk0 // BAND_PAIRS + p))]
    in_specs += [kv_spec(k_blk0, j) for j in range(BAND_TILES)]
    in_specs += [kv_spec(v_blk0, j) for j in range(BAND_TILES)]
    in_specs += [pl.BlockSpec((BAND_PAIRS, 2, BAND_TILES, BAND_TQ, BAND_TQ),
                              lambda p, bb, i: (p, 0, 0, 0, 0))]
    return pl.pallas_call(
        functools.partial(_band_kernel, scale=A_HEAD_DIM ** -0.5), grid=grid, in_specs=in_specs,
        out_specs=pl.BlockSpec((None, BAND_TQ, width), lambda p, bb, i: (bb, i, p)),
        out_shape=jax.ShapeDtypeStruct((b, s, n_pairs * LANES), BF16),
        compiler_params=_params("parallel", "parallel", "parallel"), name="band_attention",
    )(proj, proj, proj, proj, proj, proj, proj, bias_tiles)


def _mla_prep_kernel(cq_ref, ckv_ref, kr_ref, gq_ref, gkv_ref, wq_ref, wk_ref, wv_ref,
                     cq_cos_ref, cq_sin_ref, ck_cos_ref, ck_sin_ref, q_ref, k_ref, v_ref):
    half = B_ROPE_DIM // 2
    nq = _rms(cq_ref[...].astype(F32), gq_ref[...]).astype(BF16)
    nkv = _rms(ckv_ref[...].astype(F32), gkv_ref[...]).astype(BF16)
    q = _dot(nq, wq_ref[...])
    k = _dot(nkv, wk_ref[...])
    v_ref[...] = _dot(nkv, wv_ref[...]).astype(v_ref.dtype)
    kr = _rope_lanes(kr_ref[...].astype(F32), ck_cos_ref[...], ck_sin_ref[...], half)
    lane = lax.broadcasted_iota(I32, kr.shape, 1)
    kr = jnp.where(lane < B_ROPE_DIM, kr, 0.0)
    kr = pltpu.roll(kr, B_NOPE_DIM, 1)
    cos = cq_cos_ref[...]
    sin = cq_sin_ref[...]
    for c in range(q.shape[1] // LANES):
        sl = slice(c * LANES, (c + 1) * LANES)
        q_ref[:, sl] = _rope_lanes(q[:, sl], cos, sin, half).astype(q_ref.dtype)
        k_ref[:, sl] = (k[:, sl] + kr).astype(k_ref.dtype)


def mla_prep(proj, gq, gkv, wq, wk, wv, tables, *, tm, cq_blk, ckv_blk, kr_blk):
    t = proj.shape[0]
    nq = wq.shape[0]
    nkv = wk.shape[0]
    whole = lambda a: pl.BlockSpec(a.shape, lambda i: (0,) * a.ndim)
    tab = lambda k: pl.BlockSpec((None, tm, LANES), lambda i: (k, i, 0))
    in_specs = [
        pl.BlockSpec((tm, nq), lambda i: (i, cq_blk)),
        pl.BlockSpec((tm, nkv), lambda i: (i, ckv_blk)),
        pl.BlockSpec((tm, LANES), lambda i: (i, kr_blk)),
        whole(gq), whole(gkv), whole(wq), whole(wk), whole(wv),
        tab(0), tab(1), tab(2), tab(3),
    ]
    out_specs = [pl.BlockSpec((tm, wq.shape[1]), lambda i: (i, 0)),
                 pl.BlockSpec((tm, wk.shape[1]), lambda i: (i, 0)),
                 pl.BlockSpec((tm, wv.shape[1]), lambda i: (i, 0))]
    out_shape = [jax.ShapeDtypeStruct((t, wq.shape[1]), BF16),
                 jax.ShapeDtypeStruct((t, wk.shape[1]), BF16),
                 jax.ShapeDtypeStruct((t, wv.shape[1]), BF16)]
    return pl.pallas_call(
        _mla_prep_kernel, grid=(t // tm,), in_specs=in_specs, out_specs=out_specs,
        out_shape=out_shape, compiler_params=_params("parallel"), name="mla_prep",
    )(proj, proj, proj, gq, gkv, wq, wk, wv, tables, tables, tables, tables)


def _router_kernel(h_ref, g_ref, wr_ref, gate_ref, idx_ref, cnt_ref, npk_ref, carry_ref):
    tb = h_ref.shape[0]

    @pl.when(pl.program_id(0) == 0)
    def _():
        carry_ref[...] = jnp.zeros_like(carry_ref)

    n = _rms(h_ref[...], g_ref[...])
    npk_ref[...] = _pack_bf16_pairs(n)
    n_hi = n.astype(BF16)
    n_lo = (n - n_hi.astype(F32)).astype(BF16)
    logits = _dot(n_hi, wr_ref[0]) + (_dot(n_hi, wr_ref[1]) + _dot(n_lo, wr_ref[0]))
    lane = lax.broadcasted_iota(I32, (tb, LANES), 1).astype(F32)
    neg_inf = jnp.float32(-jnp.inf)
    lg = jnp.where(lane < N_EXPERTS, logits, neg_inf)
    m1 = jnp.max(lg, axis=-1, keepdims=True)
    e0 = jnp.min(jnp.where(lg == m1, lane, float(LANES)), axis=-1, keepdims=True)
    lg2 = jnp.where(lane == e0, neg_inf, lg)
    m2 = jnp.max(lg2, axis=-1, keepdims=True)
    e1 = jnp.min(jnp.where(lg2 == m2, lane, float(LANES)), axis=-1, keepdims=True)
    t = jnp.exp(m2 - m1)
    g0 = 1.0 / (1.0 + t)
    g1 = t / (1.0 + t)

    sel = ((lane == e0) | (lane == e1)).astype(F32)
    row = lax.broadcasted_iota(I32, (tb, tb), 0)
    col = lax.broadcasted_iota(I32, (tb, tb), 1)
    strict_lower = (col < row).astype(BF16)
    carry = carry_ref[...]
    excl = _dot(strict_lower, sel.astype(BF16)) + carry
    r0 = jnp.sum(jnp.where(lane == e0, excl, 0.0), axis=-1, keepdims=True)
    r1 = jnp.sum(jnp.where(lane == e1, excl, 0.0), axis=-1, keepdims=True)
    carry = carry + jnp.sum(sel, axis=0, keepdims=True)
    carry_ref[...] = carry

    gate_ref[...] = jnp.where(lane == 0, g0, jnp.where(lane == 1, g1, 0.0))
    idx_ref[...] = jnp.where(
        lane == 0, e0, jnp.where(lane == 1, e1, jnp.where(
            lane == 2, r0, jnp.where(lane == 3, r1, 0.0)))).astype(I32)
    cnt_ref[...] = jnp.broadcast_to(carry, cnt_ref.shape).astype(I32)


def router(h, g, wr_split, *, tb):
    t, d = h.shape
    return pl.pallas_call(
        _router_kernel, grid=(t // tb,),
        in_specs=[pl.BlockSpec((tb, d), lambda i: (i, 0)),
                  pl.BlockSpec((1, d), lambda i: (0, 0)),
                  pl.BlockSpec((2, d, LANES), lambda i: (0, 0, 0))],
        out_specs=[pl.BlockSpec((tb, LANES), lambda i: (i, 0)),
                   pl.BlockSpec((tb, LANES), lambda i: (i, 0)),
                   pl.BlockSpec((8, LANES), lambda i: (0, 0)),
                   pl.BlockSpec((tb, d // 2), lambda i: (i, 0))],
        out_shape=[jax.ShapeDtypeStruct((t, LANES), F32),
                   jax.ShapeDtypeStruct((t, LANES), I32),
                   jax.ShapeDtypeStruct((8, LANES), I32),
                   jax.ShapeDtypeStruct((t, d // 2), U32)],
        scratch_shapes=[pltpu.VMEM((1, LANES), F32)],
        compiler_params=_params("arbitrary"), name="router",
    )(h, g, wr_split)


def _row_copy(src_hbm, row, dst_ref, r, sem):
    return pltpu.make_async_copy(src_hbm.at[pl.ds(row, 1), :], dst_ref.at[pl.ds(r, 1), :], sem)


ROW_UNROLL = 8


def _for_rows(rows, body):
    def outer(o, carry):
        for u in range(ROW_UNROLL):
            body(o * ROW_UNROLL + u, u)
        return carry

    lax.fori_loop(0, rows // ROW_UNROLL, outer, 0)


SCATTER_SLOTS = 3


def _scatter_kernel(p0_ref, p1_ref, x_hbm, init_hbm, o_hbm, buf_ref, blk_sem, row_sem):
    del init_hbm
    i = pl.program_id(0)
    n = pl.num_programs(0)
    rows = p0_ref.shape[-1]
    slot = lax.rem(i, SCATTER_SLOTS)
    nxt = lax.rem(i + 1, SCATTER_SLOTS)

    def block_read(s, step):
        return pltpu.make_async_copy(x_hbm.at[pl.ds(step * rows, rows), :], buf_ref.at[s],
                                     blk_sem.at[s])

    def row_out(s, r, dst_row):
        return pltpu.make_async_copy(buf_ref.at[s, pl.ds(r, 1), :], o_hbm.at[pl.ds(dst_row, 1), :],
                                     row_sem.at[s])

    def start_rows(s):
        def start(r, u):
            row_out(s, r, p0_ref[0, 0, r]).start(priority=0)
            row_out(s, r, p1_ref[0, 0, r]).start(priority=1)
        _for_rows(rows, start)

    def wait_rows(s):
        def wait(r, u):
            row_out(s, r, 0).wait()
            row_out(s, r, 0).wait()
        _for_rows(rows, wait)

    @pl.when(i == 0)
    def _():
        block_read(0, 0).start()

    @pl.when(i >= 2)
    def _():
        wait_rows(nxt)

    @pl.when(i + 1 < n)
    def _():
        block_read(nxt, i + 1).start()

    block_read(slot, i).wait()
    start_rows(slot)

    @pl.when(i == n - 1)
    def _():
        @pl.when(i >= 1)
        def _():
            wait_rows(lax.rem(i + 2, SCATTER_SLOTS))
        wait_rows(slot)


def scatter_rows(x, pos0, pos1, n_out, *, rows):
    t, w = x.shape
    n = t // rows
    smem = lambda: pl.BlockSpec((1, 1, rows), lambda i: (i, 0, 0), memory_space=pltpu.SMEM)
    init = jnp.zeros((n_out, w), x.dtype)
    return pl.pallas_call(
        _scatter_kernel, grid=(n,),
        in_specs=[smem(), smem(), pl.BlockSpec(memory_space=pl.ANY),
                  pl.BlockSpec(memory_space=pl.ANY)],
        out_specs=pl.BlockSpec(memory_space=pl.ANY),
        out_shape=jax.ShapeDtypeStruct((n_out, w), x.dtype),
        input_output_aliases={3: 0},
        scratch_shapes=[pltpu.VMEM((SCATTER_SLOTS, rows, w), x.dtype),
                        pltpu.SemaphoreType.DMA((SCATTER_SLOTS,)),
                        pltpu.SemaphoreType.DMA((SCATTER_SLOTS,))],
        compiler_params=_params("arbitrary"), name="scatter_rows",
    )(pos0.reshape(n, 1, rows), pos1.reshape(n, 1, rows), x, init)


def _combine_kernel(p0_first, p1_first, p0_next, p1_next, h_ref, gate_ref, ys_hbm, *rest):
    norm_ref = rest[0] if len(rest) == 4 else None
    o_ref, buf_ref, sem = rest[-3:]
    i = pl.program_id(0)
    n = pl.num_programs(0)
    rows = h_ref.shape[0]
    slot = i & 1

    def start_into(s, p0_ref, p1_ref):
        def start(r, u):
            _row_copy(ys_hbm, p0_ref[0, 0, r], buf_ref.at[s, 0], r, sem.at[s]).start(priority=0)
            _row_copy(ys_hbm, p1_ref[0, 0, r], buf_ref.at[s, 1], r, sem.at[s]).start(priority=1)
        _for_rows(rows, start)

    @pl.when(i == 0)
    def _():
        start_into(0, p0_first, p1_first)

    @pl.when(i + 1 < n)
    def _():
        start_into(1 - slot, p0_next, p1_next)

    def wait(r, u):
        _row_copy(ys_hbm, 0, buf_ref.at[slot, 0], r, sem.at[slot]).wait()
        _row_copy(ys_hbm, 0, buf_ref.at[slot, 1], r, sem.at[slot]).wait()

    _for_rows(rows, wait)
    gates = gate_ref[...]
    half = buf_ref.shape[-1]
    a0, b0 = _unpack_bf16_pairs(buf_ref[slot, 0])
    a1, b1 = _unpack_bf16_pairs(buf_ref[slot, 1])
    lo = h_ref[:, :half] + gates[:, 0:1] * a0 + gates[:, 1:2] * a1
    hi = h_ref[:, half:] + gates[:, 0:1] * b0 + gates[:, 1:2] * b1
    if norm_ref is not None:
        ms = (jnp.sum(lo * lo, axis=-1, keepdims=True)
              + jnp.sum(hi * hi, axis=-1, keepdims=True)) * (1.0 / (2 * half))
        inv = lax.rsqrt(ms + NORM_EPS)
        lo = lo * inv * norm_ref[:, :half]
        hi = hi * inv * norm_ref[:, half:]
    o_ref[:, :half] = lo
    o_ref[:, half:] = hi


def combine_rows(h, gates, ys, pos0, pos1, *, rows, norm_g=None):
    t, d = h.shape
    n = t // rows
    first = lambda: pl.BlockSpec((1, 1, rows), lambda i: (0, 0, 0), memory_space=pltpu.SMEM)
    nxt = lambda: pl.BlockSpec((1, 1, rows), lambda i: (jnp.minimum(i + 1, n - 1), 0, 0),
                               memory_space=pltpu.SMEM)
    pos0 = pos0.reshape(n, 1, rows)
    pos1 = pos1.reshape(n, 1, rows)
    in_specs = [first(), first(), nxt(), nxt(),
                pl.BlockSpec((rows, d), lambda i: (i, 0)),
                pl.BlockSpec((rows, LANES), lambda i: (i, 0)),
                pl.BlockSpec(memory_space=pl.ANY)]
    operands = [pos0, pos1, pos0, pos1, h, gates, ys]
    if norm_g is not None:
        in_specs.append(pl.BlockSpec((1, d), lambda i: (0, 0)))
        operands.append(norm_g)
    return pl.pallas_call(
        _combine_kernel, grid=(n,), in_specs=in_specs,
        out_specs=pl.BlockSpec((rows, d), lambda i: (i, 0)),
        out_shape=jax.ShapeDtypeStruct((t, d), F32),
        scratch_shapes=[pltpu.VMEM((2, 2, rows, ys.shape[1]), ys.dtype),
                        pltpu.SemaphoreType.DMA((2,))],
        compiler_params=_params("arbitrary"), name="combine_rows",
    )(*operands)


def _rmsnorm_kernel(x_ref, g_ref, o_ref):
    o_ref[...] = _rms(x_ref[...], g_ref[...])


def rmsnorm(x, g, *, tm):
    t, d = x.shape
    return pl.pallas_call(
        _rmsnorm_kernel, grid=(t // tm,),
        in_specs=[pl.BlockSpec((tm, d), lambda i: (i, 0)), pl.BlockSpec((1, d), lambda i: (0, 0))],
        out_specs=pl.BlockSpec((tm, d), lambda i: (i, 0)),
        out_shape=jax.ShapeDtypeStruct((t, d), F32),
        compiler_params=_params("parallel"), name="final_rmsnorm",
    )(x, g)


def _rope_cos_sin(positions, dim):
    inv_freq = 1.0 / jnp.power(ROPE_THETA, jnp.arange(0, dim, 2, dtype=F32) / dim)
    ang = positions.reshape(-1).astype(F32)[:, None] * inv_freq
    return jnp.cos(ang), jnp.sin(ang)


def _diff_tables(positions):
    cos, sin = _rope_cos_sin(positions, C_HEAD_DIM)
    cos_l = jnp.tile(cos, (1, 4))
    sin_l = jnp.concatenate([-sin, -sin, sin, sin], axis=1)
    k_tab = jnp.stack([cos_l, sin_l])
    return jnp.stack([k_tab * (C_HEAD_DIM ** -0.5 * LOG2E), k_tab])


def _diff_head_order(w, n_cols):
    d = w.shape[0]
    half = C_HEAD_DIM // 2
    head = w[:, :n_cols].reshape(d, n_cols // LANES, 2, 2, half)
    head = head.transpose(0, 1, 3, 2, 4).reshape(d, n_cols)
    return jnp.concatenate([head, w[:, n_cols:]], axis=1)


def _mla_tables(positions):
    cos, sin = _rope_cos_sin(positions, B_ROPE_DIM)
    t = cos.shape[0]
    scale = (B_NOPE_DIM + B_ROPE_DIM) ** -0.5 * LOG2E
    ones = lambda n: jnp.ones((t, n), F32)
    zeros = lambda n: jnp.zeros((t, n), F32)
    q_cos = jnp.concatenate([ones(B_NOPE_DIM), cos, cos, ones(32)], axis=1) * scale
    q_sin = jnp.concatenate([zeros(B_NOPE_DIM), -sin, sin, zeros(32)], axis=1) * scale
    k_cos = jnp.concatenate([cos, cos, ones(96)], axis=1)
    k_sin = jnp.concatenate([-sin, sin, zeros(96)], axis=1)
    return jnp.stack([q_cos, q_sin, k_cos, k_sin])


def _routing_plan(idx_out, counts, *, tm, n_tiles):
    e0, e1, r0, r1 = idx_out[:, 0], idx_out[:, 1], idx_out[:, 2], idx_out[:, 3]
    padded = ((counts + tm - 1) // tm) * tm
    ends = jnp.cumsum(padded)
    offs = ends - padded
    experts = jnp.arange(N_EXPERTS, dtype=I32)[None, :]
    pos0 = jnp.sum(jnp.where(e0[:, None] == experts, offs[None, :], 0), axis=1) + r0
    pos1 = jnp.sum(jnp.where(e1[:, None] == experts, offs[None, :], 0), axis=1) + r1
    n_valid = (ends[-1] // tm).astype(I32)
    tile = jnp.minimum(jnp.arange(n_tiles, dtype=I32), n_valid - 1)
    tile_expert = jnp.minimum(jnp.sum((tile * tm)[:, None] >= ends[None, :], axis=1),
                              N_EXPERTS - 1)
    return pos0.astype(I32), pos1.astype(I32), tile_expert.astype(I32), n_valid.reshape(1)


def _pad_cols(w, n):
    return jnp.pad(w, ((0, 0), (0, n - w.shape[1])))


def kernel(x, positions, mix_norm_g, ffn_norm_g, w_in_even, rel_bias_a, q_norm_b, w_uq_b, kv_norm_b, w_ukv_b, w_out_even, w_in_odd, lambda_q1, lambda_k1, lambda_q2, lambda_k2, subln_g, w_out_odd, w_gu_dense, w_down_dense, w_router, w_gu_moe, w_down_moe, final_norm_g):
    b, s, d = x.shape
    t = b * s
    depth = mix_norm_g.shape[0]
    tm = ROW_TILE
    tm_moe = ROW_TILE
    n_tiles_moe = (2 * t) // tm_moe + N_EXPERTS
    a_width = w_out_even.shape[1] - w_uq_b.shape[2] // (B_NOPE_DIM + B_ROPE_DIM) * B_V_DIM
    q_lora, kv_lora = w_uq_b.shape[1], w_ukv_b.shape[1]
    even_in = -(-w_in_even.shape[2] // PROJ_COL_TILE) * PROJ_COL_TILE
    cq_blk = 3 * a_width // q_lora
    ckv_blk = (3 * a_width + q_lora) // kv_lora
    kr_blk = (3 * a_width + q_lora + kv_lora) // LANES

    w_gu_all = w_gu_moe.reshape((-1,) + w_gu_moe.shape[2:]).astype(BF16)
    w_down_all = w_down_moe.reshape((-1,) + w_down_moe.shape[2:]).astype(BF16)
    diff_tab = _diff_tables(positions)
    mla_tab = _mla_tables(positions)
    row = lambda v: v.reshape(1, -1).astype(F32)

    h = x.reshape(t, d).astype(F32)
    for layer in range(depth):
        i = layer // 2
        if layer % 2 == 0:
            w_in = _pad_cols(w_in_even[i], even_in).astype(BF16)
            proj = rms_matmul(h, row(mix_norm_g[layer]), w_in, tm=tm, tn=PROJ_COL_TILE)
            proj3 = proj.reshape(b, s, -1)
            a_pairs = a_width // LANES
            out_a = band_attention(proj3, band_bias_tiles(rel_bias_a[i]),
                                   q_blk0=0, k_blk0=a_pairs, v_blk0=2 * a_pairs)
            n_heads = w_uq_b.shape[2] // (B_NOPE_DIM + B_ROPE_DIM)
            wq = jnp.pad(w_uq_b[i].reshape(-1, n_heads, B_NOPE_DIM + B_ROPE_DIM),
                         ((0, 0), (0, 0), (0, LANES - B_NOPE_DIM - B_ROPE_DIM)))
            wq = wq.reshape(-1, n_heads * LANES).astype(BF16)
            wkv = w_ukv_b[i].reshape(-1, n_heads, B_NOPE_DIM + B_V_DIM)
            wk = jnp.pad(wkv[:, :, :B_NOPE_DIM], ((0, 0), (0, 0), (0, LANES - B_NOPE_DIM)))
            wk = wk.reshape(-1, n_heads * LANES).astype(BF16)
            wv = wkv[:, :, B_NOPE_DIM:].reshape(-1, n_heads * B_V_DIM).astype(BF16)
            q_b, k_b, v_b = mla_prep(proj, row(q_norm_b[i]), row(kv_norm_b[i]), wq, wk, wv, mla_tab,
                                     tm=tm, cq_blk=cq_blk, ckv_blk=ckv_blk, kr_blk=kr_blk)
            out_b = flash_attention(q_b.reshape(b, s, -1), k_b.reshape(b, s, -1),
                                    v_b.reshape(b, s, -1), mode="mla", q_blk0=0, k_blk0=0,
                                    v_blk0=0, n_groups=n_heads // 2, tq=FLASH_TQ,
                                    groups=FLASH_GROUPS)
            w_out = w_out_even[i].astype(BF16)
            h = matmul_residual([out_a.reshape(t, -1), out_b.reshape(t, -1)],
                                [w_out[:a_width], w_out[a_width:]], h, tm=tm)
            n_tiles = t // tm
            h = grouped_swiglu(h, row(ffn_norm_g[layer]), w_gu_dense.astype(BF16),
                               w_down_dense.astype(BF16), jnp.full((n_tiles,), i, I32),
                               jnp.full((1,), n_tiles, I32), tm=tm,
                               tf=w_down_dense.shape[1] // FF_STEPS, add_res=True)
        else:
            c_width = w_out_odd.shape[1]
            n_heads = c_width // LANES
            w_in = _diff_head_order(w_in_odd[i], 2 * c_width).astype(BF16)
            qkv = rms_matmul(h, row(mix_norm_g[layer]), w_in, tm=tm, tn=PROJ_COL_TILE,
                             rope=(diff_tab, c_width // PROJ_COL_TILE, LANES // 2))
            lam_init = 0.8 - 0.6 * math.exp(-0.3 * layer)
            lam = (jnp.exp(jnp.sum(lambda_q1[i].astype(F32) * lambda_k1[i].astype(F32)))
                   - jnp.exp(jnp.sum(lambda_q2[i].astype(F32) * lambda_k2[i].astype(F32)))
                   + lam_init).reshape(1).astype(F32)
            qkv3 = qkv.reshape(b, s, -1)
            out_c = flash_attention(qkv3, qkv3, qkv3, mode="diff", q_blk0=0, k_blk0=n_heads,
                                    v_blk0=2 * n_heads, n_groups=n_heads, tq=FLASH_TQ,
                                    groups=FLASH_GROUPS, lam=lam,
                                    subln_g=row(subln_g[i]), out_scale=1.0 - lam_init)
            h = matmul_residual([out_c.reshape(t, -1)], [w_out_odd[i].astype(BF16)], h, tm=tm)

            g_ffn = row(ffn_norm_g[layer])
            wr = _pad_cols(w_router[i].astype(F32), LANES)
            wr_hi = wr.astype(BF16)
            wr_split = jnp.stack([wr_hi, (wr - wr_hi.astype(F32)).astype(BF16)])
            gates, idx_out, cnt, n_packed = router(h, g_ffn, wr_split, tb=ROW_TILE)
            pos0, pos1, tile_expert, n_valid = _routing_plan(
                idx_out, cnt[0, :N_EXPERTS], tm=tm_moe, n_tiles=n_tiles_moe)
            xs = scatter_rows(n_packed, pos0, pos1, n_tiles_moe * tm_moe, rows=DMA_ROWS)
            ys = grouped_swiglu(xs, None, w_gu_all, w_down_all, tile_expert + i * N_EXPERTS,
                                n_valid, tm=tm_moe, tf=w_down_all.shape[1] // FF_STEPS)
            last = layer == depth - 1
            h = combine_rows(h, gates, ys, pos0, pos1, rows=DMA_ROWS,
                             norm_g=row(final_norm_g) if last else None)

    if depth % 2 == 1:
        h = rmsnorm(h, row(final_norm_g), tm=tm)
    return h.reshape(b, s, d)
```

```python
import functools
import math

import jax
import jax.numpy as jnp
import numpy as np
from jax import lax
from jax.experimental import pallas as pl
from jax.experimental.pallas import tpu as pltpu

F32 = jnp.float32
BF16 = jnp.bfloat16
I32 = jnp.int32
U32 = jnp.uint32
HI16 = 0xFFFF0000

NORM_EPS = 1e-6
ROPE_THETA = 10000.0
NEG = -1e30
LOG2E = math.log2(math.e)
LANES = 128
CHUNK = 64

A_LEFT_CHUNKS = 8
A_MAX_REL = 256
A_HEAD_DIM = 64
B_NOPE_DIM = 64
B_ROPE_DIM = 32
B_V_DIM = 64
C_HEAD_DIM = 64
N_EXPERTS = 8

VMEM_LIMIT = 56 * 1024 * 1024

ROW_TILE = 512
PROJ_COL_TILE = 1024
FF_STEPS = 2
FLASH_TQ = 512
FLASH_GROUPS = 2
DMA_ROWS = 512
OUT_PROJ_ROWS = 1024


def _params(*sem):
    return pltpu.CompilerParams(dimension_semantics=sem, vmem_limit_bytes=VMEM_LIMIT)


def _rms(x, g):
    ms = jnp.mean(x * x, axis=-1, keepdims=True)
    return x * lax.rsqrt(ms + NORM_EPS) * g


def _dot(a, b):
    return jnp.dot(a, b, preferred_element_type=F32)


def _dot_nt(a, b):
    return lax.dot_general(a, b, (((1,), (1,)), ((), ())), preferred_element_type=F32)


def _rope_lanes(x, cos, sin, half):
    if 2 * half == LANES:
        swapped = pltpu.roll(x, half, 1)
    else:
        lane = lax.broadcasted_iota(I32, x.shape, 1)
        fwd = pltpu.roll(x, LANES - half, 1)
        bwd = pltpu.roll(x, half, 1)
        swapped = jnp.where((lane & half) == 0, fwd, bwd)
    return x * cos + swapped * sin


def _rms_matmul_kernel(x_ref, g_ref, w_ref, o_ref, xn_ref):
    @pl.when(pl.program_id(1) == 0)
    def _():
        xn_ref[...] = _rms(x_ref[...], g_ref[...]).astype(BF16)

    o_ref[...] = _dot(xn_ref[...], w_ref[...]).astype(o_ref.dtype)


def _rms_matmul_rope_kernel(x_ref, g_ref, w_ref, cos_ref, sin_ref, o_ref, xn_ref, *,
                            n_rope_tiles, half):
    j = pl.program_id(1)

    @pl.when(j == 0)
    def _():
        xn_ref[...] = _rms(x_ref[...], g_ref[...]).astype(BF16)

    acc = _dot(xn_ref[...], w_ref[...])

    @pl.when(j < n_rope_tiles)
    def _():
        cos = cos_ref[...]
        sin = sin_ref[...]
        for c in range(acc.shape[1] // LANES):
            sl = slice(c * LANES, (c + 1) * LANES)
            o_ref[:, sl] = _rope_lanes(acc[:, sl], cos, sin, half).astype(o_ref.dtype)

    @pl.when(j >= n_rope_tiles)
    def _():
        o_ref[...] = acc.astype(o_ref.dtype)


def rms_matmul(x, g, w, *, tm, tn, rope=None):
    t, d = x.shape
    n = w.shape[1]
    grid = (t // tm, n // tn)
    x_spec = pl.BlockSpec((tm, d), lambda i, j: (i, 0))
    g_spec = pl.BlockSpec((1, d), lambda i, j: (0, 0))
    w_spec = pl.BlockSpec((d, tn), lambda i, j: (0, j))
    o_spec = pl.BlockSpec((tm, tn), lambda i, j: (i, j))
    scratch = [pltpu.VMEM((tm, d), BF16)]
    out_shape = jax.ShapeDtypeStruct((t, n), BF16)
    if rope is None:
        return pl.pallas_call(
            _rms_matmul_kernel, grid=grid, in_specs=[x_spec, g_spec, w_spec], out_specs=o_spec,
            out_shape=out_shape, scratch_shapes=scratch,
            compiler_params=_params("parallel", "arbitrary"), name="rms_matmul",
        )(x, g, w)
    tables, per_part, half = rope
    cos_spec = pl.BlockSpec((None, None, tm, LANES),
                            lambda i, j: (jnp.minimum(j // per_part, 1), 0, i, 0))
    sin_spec = pl.BlockSpec((None, None, tm, LANES),
                            lambda i, j: (jnp.minimum(j // per_part, 1), 1, i, 0))
    kern = functools.partial(_rms_matmul_rope_kernel, n_rope_tiles=2 * per_part, half=half)
    return pl.pallas_call(
        kern, grid=grid, in_specs=[x_spec, g_spec, w_spec, cos_spec, sin_spec], out_specs=o_spec,
        out_shape=out_shape, scratch_shapes=scratch,
        compiler_params=_params("parallel", "arbitrary"), name="rms_matmul_rope",
    )(x, g, w, tables, tables)


def _mm_res_kernel(*refs, n_in):
    res_ref = refs[2 * n_in]
    o_ref = refs[2 * n_in + 1]
    acc = res_ref[...]
    for k in range(n_in):
        acc = acc + _dot(refs[k][...], refs[n_in + k][...])
    o_ref[...] = acc


def matmul_residual(a_list, w_list, res, *, tm):
    t, n = res.shape
    n_in = len(a_list)
    in_specs = [pl.BlockSpec((tm, a.shape[1]), lambda i: (i, 0)) for a in a_list]
    in_specs += [pl.BlockSpec(w.shape, lambda i: (0, 0)) for w in w_list]
    in_specs += [pl.BlockSpec((tm, n), lambda i: (i, 0))]
    return pl.pallas_call(
        functools.partial(_mm_res_kernel, n_in=n_in), grid=(t // tm,), in_specs=in_specs,
        out_specs=pl.BlockSpec((tm, n), lambda i: (i, 0)),
        out_shape=jax.ShapeDtypeStruct((t, n), F32),
        compiler_params=_params("parallel"), name="matmul_residual",
    )(*a_list, *w_list, res)


def _pack_bf16_pairs(x):
    n = x.shape[1] // 2
    bits = lax.bitcast_convert_type(x.astype(BF16).astype(F32), U32)
    return (bits[:, :n] & U32(HI16)) | (bits[:, n:] >> 16)


def _unpack_bf16_pairs(w):
    first = lax.bitcast_convert_type(w & U32(HI16), F32)
    second = lax.bitcast_convert_type(w << 16, F32)
    return first, second


def _ffn_kernel(te_ref, nv_ref, x_ref, g_ref, wg_ref, wu_ref, wd_ref, o_ref, xn_ref, acc_ref, *,
                add_res):
    del te_ref
    i = pl.program_id(0)
    f = pl.program_id(1)
    nf = pl.num_programs(1)
    valid = i < nv_ref[0]

    @pl.when(valid & (f == 0))
    def _():
        xn_ref[...] = _rms(x_ref[...], g_ref[...]).astype(BF16)
        acc_ref[...] = jnp.zeros_like(acc_ref)

    _ffn_body(valid, wg_ref, wu_ref, wd_ref, xn_ref, acc_ref)

    @pl.when(valid & (f == nf - 1))
    def _():
        if add_res:
            o_ref[...] = x_ref[...] + acc_ref[...]
        else:
            o_ref[...] = acc_ref[...]

    @pl.when(jnp.logical_not(valid) & (f == nf - 1))
    def _():
        o_ref[...] = jnp.zeros_like(o_ref)


def _ffn_body(valid, wg_ref, wu_ref, wd_ref, xn_ref, acc_ref):
    @pl.when(valid)
    def _():
        xn = xn_ref[...]
        gate = _dot(xn, wg_ref[...])
        up = _dot(xn, wu_ref[...])
        act = (gate * jax.nn.sigmoid(gate) * up).astype(BF16)
        acc_ref[...] += _dot(act, wd_ref[...])


def _ffn_packed_kernel(te_ref, nv_ref, x_ref, wg_ref, wu_ref, wd_ref, o_ref, xn_ref, acc_ref):
    del te_ref
    i = pl.program_id(0)
    f = pl.program_id(1)
    nf = pl.num_programs(1)
    valid = i < nv_ref[0]
    half = x_ref.shape[1]

    @pl.when(valid & (f == 0))
    def _():
        first, second = _unpack_bf16_pairs(x_ref[...])
        xn_ref[:, :half] = first.astype(BF16)
        xn_ref[:, half:] = second.astype(BF16)
        acc_ref[...] = jnp.zeros_like(acc_ref)

    _ffn_body(valid, wg_ref, wu_ref, wd_ref, xn_ref, acc_ref)

    @pl.when(valid & (f == nf - 1))
    def _():
        o_ref[...] = _pack_bf16_pairs(acc_ref[...])

    @pl.when(jnp.logical_not(valid) & (f == nf - 1))
    def _():
        o_ref[...] = jnp.zeros_like(o_ref)


def grouped_swiglu(x, g, w_gu, w_down, tile_expert, n_valid, *, tm, tf, add_res=False):
    packed = g is None
    assert not (packed and add_res)
    p = x.shape[0]
    d = w_gu.shape[1]
    ff = w_down.shape[1]
    nf = ff // tf
    grid = (p // tm, nf)

    def f_eff(i, f, nv):
        return jnp.where(i < nv[0], f, nf - 1)

    x_spec = pl.BlockSpec((tm, x.shape[1]), lambda i, f, te, nv: (i, 0))
    w_specs = [
        pl.BlockSpec((None, d, tf), lambda i, f, te, nv: (te[i], 0, f_eff(i, f, nv))),
        pl.BlockSpec((None, d, tf), lambda i, f, te, nv: (te[i], 0, nf + f_eff(i, f, nv))),
        pl.BlockSpec((None, tf, d), lambda i, f, te, nv: (te[i], f_eff(i, f, nv), 0)),
    ]
    if packed:
        kern, in_specs, operands = _ffn_packed_kernel, [x_spec] + w_specs, (x,)
    else:
        kern = functools.partial(_ffn_kernel, add_res=add_res)
        in_specs = [x_spec, pl.BlockSpec((1, d), lambda i, f, te, nv: (0, 0))] + w_specs
        operands = (x, g)
    grid_spec = pltpu.PrefetchScalarGridSpec(
        num_scalar_prefetch=2, grid=grid, in_specs=in_specs,
        out_specs=pl.BlockSpec((tm, x.shape[1]), lambda i, f, te, nv: (i, 0)),
        scratch_shapes=[pltpu.VMEM((tm, d), BF16), pltpu.VMEM((tm, d), F32)],
    )
    return pl.pallas_call(
        kern, grid_spec=grid_spec, out_shape=jax.ShapeDtypeStruct(x.shape, x.dtype),
        compiler_params=_params("parallel", "arbitrary"), name="grouped_swiglu",
    )(tile_expert, n_valid, *operands, w_gu, w_gu, w_down)


def _flash_kernel(lam_ref, q_ref, k_ref, v_ref, g_ref, o_ref, m_ref, l_ref, acc_ref, vt_ref,
                  s_ref, *, mode, tq, groups, out_scale):
    i = pl.program_id(2)
    s_len = v_ref.shape[0]
    wqk = q_ref.shape[1] // groups
    n_maps = 2 * groups
    sub = lax.broadcasted_iota(I32, (LANES, tq), 0)
    top = sub < (LANES // 2)

    @pl.when(i == 0)
    def _():
        for c in range(s_len // tq):
            sl = slice(c * tq, (c + 1) * tq)
            vt_ref[:, sl] = v_ref[sl, :].T

    qts = []
    for g in range(groups):
        if mode == "mla":
            qts += [q_ref[:, g * wqk:g * wqk + LANES].T, q_ref[:, g * wqk + LANES:(g + 1) * wqk].T]
        else:
            qt = q_ref[:, g * wqk:(g + 1) * wqk].T
            zero = jnp.zeros_like(qt)
            map1 = (sub & (C_HEAD_DIM // 2)) == 0
            qts += [jnp.where(map1, qt, zero), jnp.where(map1, zero, qt)]

    m_ref[...] = jnp.full_like(m_ref, NEG)
    l_ref[...] = jnp.zeros_like(l_ref)
    acc_ref[...] = jnp.zeros_like(acc_ref)

    def k_tile(mi, start):
        g, idx = divmod(mi, 2)
        if mode == "mla":
            return k_ref[pl.ds(start, tq), g * wqk + idx * LANES:g * wqk + (idx + 1) * LANES]
        return k_ref[pl.ds(start, tq), g * wqk:(g + 1) * wqk]

    def update(mi, s, vt_tile):
        m_prev = m_ref[mi]
        m_new = jnp.maximum(m_prev, jnp.max(s, axis=0, keepdims=True))
        alpha = jnp.exp2(m_prev - m_new)
        p = jnp.exp2(s - m_new)
        l_ref[mi] = alpha * l_ref[mi] + jnp.sum(p, axis=0, keepdims=True)
        acc_ref[mi] = alpha * acc_ref[mi] + _dot(vt_tile, p.astype(BF16))
        m_ref[mi] = m_new

    def scores_into(slot, t):
        start = pl.multiple_of(t * tq, tq)
        for mi in range(n_maps):
            s_ref[slot, mi] = _dot(k_tile(mi, start), qts[mi])

    def consume(slot, t, masked):
        start = pl.multiple_of(t * tq, tq)
        if masked:
            row = lax.broadcasted_iota(I32, (tq, tq), 0)
            col = lax.broadcasted_iota(I32, (tq, tq), 1)
            shift = CHUNK.bit_length() - 1
            visible = (row >> shift) <= (col >> shift)
        for mi in range(n_maps):
            g = mi // 2
            s = s_ref[slot, mi]
            if masked:
                s = jnp.where(visible, s, NEG)
            update(mi, s, vt_ref[g * LANES:(g + 1) * LANES, pl.ds(start, tq)])

    scores_into(0, 0)

    def tile_pair(u, carry):
        t = 2 * u
        scores_into(1, t + 1)
        consume(0, t, False)
        scores_into(0, t + 2)
        consume(1, t + 1, False)
        return carry

    lax.fori_loop(0, i >> 1, tile_pair, 0)

    @pl.when((i & 1) == 1)
    def _():
        scores_into(1, i)
        consume(0, i - 1, False)
        consume(1, i, True)

    @pl.when((i & 1) == 0)
    def _():
        consume(0, i, True)

    for g in range(groups):
        o0 = acc_ref[2 * g] * (1.0 / l_ref[2 * g])
        o1 = acc_ref[2 * g + 1] * (1.0 / l_ref[2 * g + 1])
        cols = slice(g * LANES, (g + 1) * LANES)
        if mode == "mla":
            o_ref[:, cols] = jnp.where(top, o0, o1).T.astype(o_ref.dtype)
        else:
            o = (o0 - lam_ref[0] * o1).T
            o_ref[:, cols] = (_rms(o, g_ref[...]) * out_scale).astype(o_ref.dtype)


def flash_attention(q_arr, k_arr, v_arr, *, mode, q_blk0, k_blk0, v_blk0, n_groups, tq,
                    groups=1, lam=None, subln_g=None, out_scale=1.0):
    b, s, _ = q_arr.shape
    wqk = (2 * LANES if mode == "mla" else LANES) * groups
    wv = LANES * groups
    assert n_groups % groups == 0 and q_blk0 % groups == 0 and k_blk0 % groups == 0
    assert v_blk0 % groups == 0
    if lam is None:
        lam = jnp.zeros((1,), F32)
    if subln_g is None:
        subln_g = jnp.ones((1, LANES), F32)
    grid = (b, n_groups // groups, s // tq)
    qb, kb, vb = q_blk0 // groups, k_blk0 // groups, v_blk0 // groups
    in_specs = [
        pl.BlockSpec((None, tq, wqk), lambda bb, g, i, lam_r: (bb, i, qb + g)),
        pl.BlockSpec((None, s, wqk), lambda bb, g, i, lam_r: (bb, 0, kb + g)),
        pl.BlockSpec((None, s, wv), lambda bb, g, i, lam_r: (bb, 0, vb + g)),
        pl.BlockSpec((1, LANES), lambda bb, g, i, lam_r: (0, 0)),
    ]
    n_maps = 2 * groups
    grid_spec = pltpu.PrefetchScalarGridSpec(
        num_scalar_prefetch=1, grid=grid, in_specs=in_specs,
        out_specs=pl.BlockSpec((None, tq, wv), lambda bb, g, i, lam_r: (bb, i, g)),
        scratch_shapes=[pltpu.VMEM((n_maps, 1, tq), F32), pltpu.VMEM((n_maps, 1, tq), F32),
                        pltpu.VMEM((n_maps, LANES, tq), F32), pltpu.VMEM((wv, s), BF16),
                        pltpu.VMEM((2, n_maps, tq, tq), F32)],
    )
    kern = functools.partial(_flash_kernel, mode=mode, tq=tq, groups=groups, out_scale=out_scale)
    return pl.pallas_call(
        kern, grid_spec=grid_spec,
        out_shape=jax.ShapeDtypeStruct((b, s, n_groups * LANES), BF16),
        compiler_params=_params("parallel", "parallel", "arbitrary"), name="flash_" + mode,
    )(lam, q_arr, k_arr, v_arr, subln_g)


BAND_TQ = 256
BAND_TILES = 3
BAND_PAIRS = 4


def _band_kernel(q_ref, k0_ref, k1_ref, k2_ref, v0_ref, v1_ref, v2_ref, bias_ref, o_ref, *, scale):
    i = pl.program_id(2)
    lane = lax.broadcasted_iota(I32, (BAND_TQ, LANES), 1)
    low = lane < (LANES // 2)
    k_refs = (k0_ref, k1_ref, k2_ref)
    v_refs = (v0_ref, v1_ref, v2_ref)
    for pp in range(BAND_PAIRS):
        cols = slice(pp * LANES, (pp + 1) * LANES)
        q = q_ref[:, cols].astype(F32) * scale
        qs = (jnp.where(low, q, 0.0).astype(BF16), jnp.where(low, 0.0, q).astype(BF16))
        outs = []
        for h in range(2):
            scores = []
            for j in range(BAND_TILES):
                pen = jnp.where(i + j >= BAND_TILES - 1, 0.0, NEG).astype(F32)
                scores.append(_dot_nt(qs[h], k_refs[j][:, cols]) + bias_ref[pp, h, j] + pen)
            m = jnp.maximum(jnp.maximum(jnp.max(scores[0], axis=-1, keepdims=True),
                                        jnp.max(scores[1], axis=-1, keepdims=True)),
                            jnp.max(scores[2], axis=-1, keepdims=True))
            l = jnp.zeros_like(m)
            o = jnp.zeros((BAND_TQ, LANES), F32)
            for j in range(BAND_TILES):
                p = jnp.exp(scores[j] - m)
                l = l + jnp.sum(p, axis=-1, keepdims=True)
                o = o + _dot(p.astype(BF16), v_refs[j][:, cols])
            outs.append(o * (1.0 / l))
        o_ref[:, cols] = jnp.where(low, outs[0], outs[1]).astype(o_ref.dtype)


def band_bias_tiles(rel_bias):
    h = rel_bias.shape[0]
    n = BAND_TQ
    r = np.arange(n)[:, None]
    c = np.arange(n)[None, :]
    e_of = np.zeros(2 * n, np.int64)
    e_of[:n] = -np.arange(n)
    e_of[n + 1:] = n - 1 - np.arange(n - 1)
    tiles = []
    for j in range(BAND_TILES):
        rel = (BAND_TILES - 1 - j) * n + e_of
        gen = rel_bias[:, np.clip(rel, -A_MAX_REL, A_MAX_REL) + A_MAX_REL].astype(F32)
        skew = jnp.tile(gen, (1, n))[:, :n * (2 * n - 1)].reshape(h, n, 2 * n - 1)[:, :, :n]
        dist = (r // CHUNK) - (c // CHUNK) + (BAND_TILES - 1 - j) * (n // CHUNK)
        ok = (dist >= 0) & (dist <= A_LEFT_CHUNKS)
        tiles.append(jnp.where(ok[None], skew, NEG))
    t = jnp.stack(tiles, axis=1)
    return t.reshape(h // 2, 2, BAND_TILES, BAND_TQ, BAND_TQ)


def band_attention(proj, bias_tiles, *, q_blk0, k_blk0, v_blk0):
    b, s, _ = proj.shape
    n_pairs = bias_tiles.shape[0]
    width = BAND_PAIRS * LANES
    assert n_pairs % BAND_PAIRS == 0 and q_blk0 % BAND_PAIRS == 0
    assert k_blk0 % BAND_PAIRS == 0 and v_blk0 % BAND_PAIRS == 0
    grid = (n_pairs // BAND_PAIRS, b, s // BAND_TQ)

    def kv_spec(blk0, j):
        return pl.BlockSpec(
            (None, BAND_TQ, width),
            lambda p, bb, i: (bb, jnp.maximum(i + j - (BAND_TILES - 1), 0),
                              blk0 // BAND_PAIRS + p))

    in_specs = [pl.BlockSpec((None, BAND_TQ, width),
                             lambda p, bb, i: (bb, i, q_blk0 // BAND_PAIRS + p))]
    in_specs += [kv_spec(k_blk0, j) for j in range(BAND_TILES)]
    in_specs += [kv_spec(v_blk0, j) for j in range(BAND_TILES)]
    in_specs += [pl.BlockSpec((BAND_PAIRS, 2, BAND_TILES, BAND_TQ, BAND_TQ),
                              lambda p, bb, i: (p, 0, 0, 0, 0))]
    return pl.pallas_call(
        functools.partial(_band_kernel, scale=A_HEAD_DIM ** -0.5), grid=grid, in_specs=in_specs,
        out_specs=pl.BlockSpec((None, BAND_TQ, width), lambda p, bb, i: (bb, i, p)),
        out_shape=jax.ShapeDtypeStruct((b, s, n_pairs * LANES), BF16),
        compiler_params=_params("parallel", "parallel", "parallel"), name="band_attention",
    )(proj, proj, proj, proj, proj, proj, proj, bias_tiles)


def _mla_prep_kernel(cq_ref, ckv_ref, kr_ref, gq_ref, gkv_ref, wq_ref, wk_ref, wv_ref,
                     cq_cos_ref, cq_sin_ref, ck_cos_ref, ck_sin_ref, q_ref, k_ref, v_ref):
    half = B_ROPE_DIM // 2
    nq = _rms(cq_ref[...].astype(F32), gq_ref[...]).astype(BF16)
    nkv = _rms(ckv_ref[...].astype(F32), gkv_ref[...]).astype(BF16)
    q = _dot(nq, wq_ref[...])
    k = _dot(nkv, wk_ref[...])
    v_ref[...] = _dot(nkv, wv_ref[...]).astype(v_ref.dtype)
    kr = _rope_lanes(kr_ref[...].astype(F32), ck_cos_ref[...], ck_sin_ref[...], half)
    lane = lax.broadcasted_iota(I32, kr.shape, 1)
    kr = jnp.where(lane < B_ROPE_DIM, kr, 0.0)
    kr = pltpu.roll(kr, B_NOPE_DIM, 1)
    cos = cq_cos_ref[...]
    sin = cq_sin_ref[...]
    for c in range(q.shape[1] // LANES):
        sl = slice(c * LANES, (c + 1) * LANES)
        q_ref[:, sl] = _rope_lanes(q[:, sl], cos, sin, half).astype(q_ref.dtype)
        k_ref[:, sl] = (k[:, sl] + kr).astype(k_ref.dtype)


def mla_prep(proj, gq, gkv, wq, wk, wv, tables, *, tm, cq_blk, ckv_blk, kr_blk):
    t = proj.shape[0]
    nq = wq.shape[0]
    nkv = wk.shape[0]
    whole = lambda a: pl.BlockSpec(a.shape, lambda i: (0,) * a.ndim)
    tab = lambda k: pl.BlockSpec((None, tm, LANES), lambda i: (k, i, 0))
    in_specs = [
        pl.BlockSpec((tm, nq), lambda i: (i, cq_blk)),
        pl.BlockSpec((tm, nkv), lambda i: (i, ckv_blk)),
        pl.BlockSpec((tm, LANES), lambda i: (i, kr_blk)),
        whole(gq), whole(gkv), whole(wq), whole(wk), whole(wv),
        tab(0), tab(1), tab(2), tab(3),
    ]
    out_specs = [pl.BlockSpec((tm, wq.shape[1]), lambda i: (i, 0)),
                 pl.BlockSpec((tm, wk.shape[1]), lambda i: (i, 0)),
                 pl.BlockSpec((tm, wv.shape[1]), lambda i: (i, 0))]
    out_shape = [jax.ShapeDtypeStruct((t, wq.shape[1]), BF16),
                 jax.ShapeDtypeStruct((t, wk.shape[1]), BF16),
                 jax.ShapeDtypeStruct((t, wv.shape[1]), BF16)]
    return pl.pallas_call(
        _mla_prep_kernel, grid=(t // tm,), in_specs=in_specs, out_specs=out_specs,
        out_shape=out_shape, compiler_params=_params("parallel"), name="mla_prep",
    )(proj, proj, proj, gq, gkv, wq, wk, wv, tables, tables, tables, tables)


def _router_kernel(h_ref, g_ref, wr_ref, gate_ref, idx_ref, cnt_ref, npk_ref, carry_ref):
    tb = h_ref.shape[0]

    @pl.when(pl.program_id(0) == 0)
    def _():
        carry_ref[...] = jnp.zeros_like(carry_ref)

    n = _rms(h_ref[...], g_ref[...])
    npk_ref[...] = _pack_bf16_pairs(n)
    n_hi = n.astype(BF16)
    n_lo = (n - n_hi.astype(F32)).astype(BF16)
    logits = _dot(n_hi, wr_ref[0]) + (_dot(n_hi, wr_ref[1]) + _dot(n_lo, wr_ref[0]))
    lane = lax.broadcasted_iota(I32, (tb, LANES), 1).astype(F32)
    neg_inf = jnp.float32(-jnp.inf)
    lg = jnp.where(lane < N_EXPERTS, logits, neg_inf)
    m1 = jnp.max(lg, axis=-1, keepdims=True)
    e0 = jnp.min(jnp.where(lg == m1, lane, float(LANES)), axis=-1, keepdims=True)
    lg2 = jnp.where(lane == e0, neg_inf, lg)
    m2 = jnp.max(lg2, axis=-1, keepdims=True)
    e1 = jnp.min(jnp.where(lg2 == m2, lane, float(LANES)), axis=-1, keepdims=True)
    t = jnp.exp(m2 - m1)
    g0 = 1.0 / (1.0 + t)
    g1 = t / (1.0 + t)

    sel = ((lane == e0) | (lane == e1)).astype(F32)
    row = lax.broadcasted_iota(I32, (tb, tb), 0)
    col = lax.broadcasted_iota(I32, (tb, tb), 1)
    strict_lower = (col < row).astype(BF16)
    carry = carry_ref[...]
    excl = _dot(strict_lower, sel.astype(BF16)) + carry
    r0 = jnp.sum(jnp.where(lane == e0, excl, 0.0), axis=-1, keepdims=True)
    r1 = jnp.sum(jnp.where(lane == e1, excl, 0.0), axis=-1, keepdims=True)
    carry = carry + jnp.sum(sel, axis=0, keepdims=True)
    carry_ref[...] = carry

    gate_ref[...] = jnp.where(lane == 0, g0, jnp.where(lane == 1, g1, 0.0))
    idx_ref[...] = jnp.where(
        lane == 0, e0, jnp.where(lane == 1, e1, jnp.where(
            lane == 2, r0, jnp.where(lane == 3, r1, 0.0)))).astype(I32)
    cnt_ref[...] = jnp.broadcast_to(carry, cnt_ref.shape).astype(I32)


def router(h, g, wr_split, *, tb):
    t, d = h.shape
    return pl.pallas_call(
        _router_kernel, grid=(t // tb,),
        in_specs=[pl.BlockSpec((tb, d), lambda i: (i, 0)),
                  pl.BlockSpec((1, d), lambda i: (0, 0)),
                  pl.BlockSpec((2, d, LANES), lambda i: (0, 0, 0))],
        out_specs=[pl.BlockSpec((tb, LANES), lambda i: (i, 0)),
                   pl.BlockSpec((tb, LANES), lambda i: (i, 0)),
                   pl.BlockSpec((8, LANES), lambda i: (0, 0)),
                   pl.BlockSpec((tb, d // 2), lambda i: (i, 0))],
        out_shape=[jax.ShapeDtypeStruct((t, LANES), F32),
                   jax.ShapeDtypeStruct((t, LANES), I32),
                   jax.ShapeDtypeStruct((8, LANES), I32),
                   jax.ShapeDtypeStruct((t, d // 2), U32)],
        scratch_shapes=[pltpu.VMEM((1, LANES), F32)],
        compiler_params=_params("arbitrary"), name="router",
    )(h, g, wr_split)


def _row_copy(src_hbm, row, dst_ref, r, sem):
    return pltpu.make_async_copy(src_hbm.at[pl.ds(row, 1), :], dst_ref.at[pl.ds(r, 1), :], sem)


ROW_UNROLL = 8


def _for_rows(rows, body):
    def outer(o, carry):
        for u in range(ROW_UNROLL):
            body(o * ROW_UNROLL + u, u)
        return carry

    lax.fori_loop(0, rows // ROW_UNROLL, outer, 0)


SCATTER_SLOTS = 3


def _scatter_kernel(p0_ref, p1_ref, x_hbm, init_hbm, o_hbm, buf_ref, blk_sem, row_sem):
    del init_hbm
    i = pl.program_id(0)
    n = pl.num_programs(0)
    rows = p0_ref.shape[-1]
    slot = lax.rem(i, SCATTER_SLOTS)
    nxt = lax.rem(i + 1, SCATTER_SLOTS)

    def block_read(s, step):
        return pltpu.make_async_copy(x_hbm.at[pl.ds(step * rows, rows), :], buf_ref.at[s],
                                     blk_sem.at[s])

    def row_out(s, r, dst_row):
        return pltpu.make_async_copy(buf_ref.at[s, pl.ds(r, 1), :], o_hbm.at[pl.ds(dst_row, 1), :],
                                     row_sem.at[s])

    def start_rows(s):
        def start(r, u):
            row_out(s, r, p0_ref[0, 0, r]).start(priority=0)
            row_out(s, r, p1_ref[0, 0, r]).start(priority=1)
        _for_rows(rows, start)

    def wait_rows(s):
        def wait(r, u):
            row_out(s, r, 0).wait()
            row_out(s, r, 0).wait()
        _for_rows(rows, wait)

    @pl.when(i == 0)
    def _():
        block_read(0, 0).start()

    @pl.when(i >= 2)
    def _():
        wait_rows(nxt)

    @pl.when(i + 1 < n)
    def _():
        block_read(nxt, i + 1).start()

    block_read(slot, i).wait()
    start_rows(slot)

    @pl.when(i == n - 1)
    def _():
        @pl.when(i >= 1)
        def _():
            wait_rows(lax.rem(i + 2, SCATTER_SLOTS))
        wait_rows(slot)


def scatter_rows(x, pos0, pos1, n_out, *, rows):
    t, w = x.shape
    n = t // rows
    smem = lambda: pl.BlockSpec((1, 1, rows), lambda i: (i, 0, 0), memory_space=pltpu.SMEM)
    init = jnp.zeros((n_out, w), x.dtype)
    return pl.pallas_call(
        _scatter_kernel, grid=(n,),
        in_specs=[smem(), smem(), pl.BlockSpec(memory_space=pl.ANY),
                  pl.BlockSpec(memory_space=pl.ANY)],
        out_specs=pl.BlockSpec(memory_space=pl.ANY),
        out_shape=jax.ShapeDtypeStruct((n_out, w), x.dtype),
        input_output_aliases={3: 0},
        scratch_shapes=[pltpu.VMEM((SCATTER_SLOTS, rows, w), x.dtype),
                        pltpu.SemaphoreType.DMA((SCATTER_SLOTS,)),
                        pltpu.SemaphoreType.DMA((SCATTER_SLOTS,))],
        compiler_params=_params("arbitrary"), name="scatter_rows",
    )(pos0.reshape(n, 1, rows), pos1.reshape(n, 1, rows), x, init)


def _combine_kernel(p0_first, p1_first, p0_next, p1_next, h_ref, gate_ref, ys_hbm, *rest):
    norm_ref = rest[0] if len(rest) == 4 else None
    o_ref, buf_ref, sem = rest[-3:]
    i = pl.program_id(0)
    n = pl.num_programs(0)
    rows = h_ref.shape[0]
    slot = i & 1

    def start_into(s, p0_ref, p1_ref):
        def start(r, u):
            _row_copy(ys_hbm, p0_ref[0, 0, r], buf_ref.at[s, 0], r, sem.at[s]).start(priority=0)
            _row_copy(ys_hbm, p1_ref[0, 0, r], buf_ref.at[s, 1], r, sem.at[s]).start(priority=1)
        _for_rows(rows, start)

    @pl.when(i == 0)
    def _():
        start_into(0, p0_first, p1_first)

    @pl.when(i + 1 < n)
    def _():
        start_into(1 - slot, p0_next, p1_next)

    def wait(r, u):
        _row_copy(ys_hbm, 0, buf_ref.at[slot, 0], r, sem.at[slot]).wait()
        _row_copy(ys_hbm, 0, buf_ref.at[slot, 1], r, sem.at[slot]).wait()

    _for_rows(rows, wait)
    gates = gate_ref[...]
    half = buf_ref.shape[-1]
    a0, b0 = _unpack_bf16_pairs(buf_ref[slot, 0])
    a1, b1 = _unpack_bf16_pairs(buf_ref[slot, 1])
    lo = h_ref[:, :half] + gates[:, 0:1] * a0 + gates[:, 1:2] * a1
    hi = h_ref[:, half:] + gates[:, 0:1] * b0 + gates[:, 1:2] * b1
    if norm_ref is not None:
        ms = (jnp.sum(lo * lo, axis=-1, keepdims=True)
              + jnp.sum(hi * hi, axis=-1, keepdims=True)) * (1.0 / (2 * half))
        inv = lax.rsqrt(ms + NORM_EPS)
        lo = lo * inv * norm_ref[:, :half]
        hi = hi * inv * norm_ref[:, half:]
    o_ref[:, :half] = lo
    o_ref[:, half:] = hi


def combine_rows(h, gates, ys, pos0, pos1, *, rows, norm_g=None):
    t, d = h.shape
    n = t // rows
    first = lambda: pl.BlockSpec((1, 1, rows), lambda i: (0, 0, 0), memory_space=pltpu.SMEM)
    nxt = lambda: pl.BlockSpec((1, 1, rows), lambda i: (jnp.minimum(i + 1, n - 1), 0, 0),
                               memory_space=pltpu.SMEM)
    pos0 = pos0.reshape(n, 1, rows)
    pos1 = pos1.reshape(n, 1, rows)
    in_specs = [first(), first(), nxt(), nxt(),
                pl.BlockSpec((rows, d), lambda i: (i, 0)),
                pl.BlockSpec((rows, LANES), lambda i: (i, 0)),
                pl.BlockSpec(memory_space=pl.ANY)]
    operands = [pos0, pos1, pos0, pos1, h, gates, ys]
    if norm_g is not None:
        in_specs.append(pl.BlockSpec((1, d), lambda i: (0, 0)))
        operands.append(norm_g)
    return pl.pallas_call(
        _combine_kernel, grid=(n,), in_specs=in_specs,
        out_specs=pl.BlockSpec((rows, d), lambda i: (i, 0)),
        out_shape=jax.ShapeDtypeStruct((t, d), F32),
        scratch_shapes=[pltpu.VMEM((2, 2, rows, ys.shape[1]), ys.dtype),
                        pltpu.SemaphoreType.DMA((2,))],
        compiler_params=_params("arbitrary"), name="combine_rows",
    )(*operands)


def _rmsnorm_kernel(x_ref, g_ref, o_ref):
    o_ref[...] = _rms(x_ref[...], g_ref[...])


def rmsnorm(x, g, *, tm):
    t, d = x.shape
    return pl.pallas_call(
        _rmsnorm_kernel, grid=(t // tm,),
        in_specs=[pl.BlockSpec((tm, d), lambda i: (i, 0)), pl.BlockSpec((1, d), lambda i: (0, 0))],
        out_specs=pl.BlockSpec((tm, d), lambda i: (i, 0)),
        out_shape=jax.ShapeDtypeStruct((t, d), F32),
        compiler_params=_params("parallel"), name="final_rmsnorm",
    )(x, g)


def _rope_cos_sin(positions, dim):
    inv_freq = 1.0 / jnp.power(ROPE_THETA, jnp.arange(0, dim, 2, dtype=F32) / dim)
    ang = positions.reshape(-1).astype(F32)[:, None] * inv_freq
    return jnp.cos(ang), jnp.sin(ang)


def _diff_tables(positions):
    cos, sin = _rope_cos_sin(positions, C_HEAD_DIM)
    cos_l = jnp.tile(cos, (1, 4))
    sin_l = jnp.concatenate([-sin, -sin, sin, sin], axis=1)
    k_tab = jnp.stack([cos_l, sin_l])
    return jnp.stack([k_tab * (C_HEAD_DIM ** -0.5 * LOG2E), k_tab])


def _diff_head_order(w, n_cols):
    d = w.shape[0]
    half = C_HEAD_DIM // 2
    head = w[:, :n_cols].reshape(d, n_cols // LANES, 2, 2, half)
    head = head.transpose(0, 1, 3, 2, 4).reshape(d, n_cols)
    return jnp.concatenate([head, w[:, n_cols:]], axis=1)


def _mla_tables(positions):
    cos, sin = _rope_cos_sin(positions, B_ROPE_DIM)
    t = cos.shape[0]
    scale = (B_NOPE_DIM + B_ROPE_DIM) ** -0.5 * LOG2E
    ones = lambda n: jnp.ones((t, n), F32)
    zeros = lambda n: jnp.zeros((t, n), F32)
    q_cos = jnp.concatenate([ones(B_NOPE_DIM), cos, cos, ones(32)], axis=1) * scale
    q_sin = jnp.concatenate([zeros(B_NOPE_DIM), -sin, sin, zeros(32)], axis=1) * scale
    k_cos = jnp.concatenate([cos, cos, ones(96)], axis=1)
    k_sin = jnp.concatenate([-sin, sin, zeros(96)], axis=1)
    return jnp.stack([q_cos, q_sin, k_cos, k_sin])


def _routing_plan(idx_out, counts, *, tm, n_tiles):
    e0, e1, r0, r1 = idx_out[:, 0], idx_out[:, 1], idx_out[:, 2], idx_out[:, 3]
    padded = ((counts + tm - 1) // tm) * tm
    ends = jnp.cumsum(padded)
    offs = ends - padded
    experts = jnp.arange(N_EXPERTS, dtype=I32)[None, :]
    pos0 = jnp.sum(jnp.where(e0[:, None] == experts, offs[None, :], 0), axis=1) + r0
    pos1 = jnp.sum(jnp.where(e1[:, None] == experts, offs[None, :], 0), axis=1) + r1
    n_valid = (ends[-1] // tm).astype(I32)
    tile = jnp.minimum(jnp.arange(n_tiles, dtype=I32), n_valid - 1)
    tile_expert = jnp.minimum(jnp.sum((tile * tm)[:, None] >= ends[None, :], axis=1),
                              N_EXPERTS - 1)
    return pos0.astype(I32), pos1.astype(I32), tile_expert.astype(I32), n_valid.reshape(1)


def _pad_cols(w, n):
    return jnp.pad(w, ((0, 0), (0, n - w.shape[1])))


def kernel(x, positions, mix_norm_g, ffn_norm_g, w_in_even, rel_bias_a, q_norm_b, w_uq_b, kv_norm_b, w_ukv_b, w_out_even, w_in_odd, lambda_q1, lambda_k1, lambda_q2, lambda_k2, subln_g, w_out_odd, w_gu_dense, w_down_dense, w_router, w_gu_moe, w_down_moe, final_norm_g):
    b, s, d = x.shape
    t = b * s
    depth = mix_norm_g.shape[0]
    tm = ROW_TILE
    tm_moe = ROW_TILE
    n_tiles_moe = (2 * t) // tm_moe + N_EXPERTS
    a_width = w_out_even.shape[1] - w_uq_b.shape[2] // (B_NOPE_DIM + B_ROPE_DIM) * B_V_DIM
    q_lora, kv_lora = w_uq_b.shape[1], w_ukv_b.shape[1]
    even_in = -(-w_in_even.shape[2] // PROJ_COL_TILE) * PROJ_COL_TILE
    cq_blk = 3 * a_width // q_lora
    ckv_blk = (3 * a_width + q_lora) // kv_lora
    kr_blk = (3 * a_width + q_lora + kv_lora) // LANES

    w_gu_all = w_gu_moe.reshape((-1,) + w_gu_moe.shape[2:]).astype(BF16)
    w_down_all = w_down_moe.reshape((-1,) + w_down_moe.shape[2:]).astype(BF16)
    diff_tab = _diff_tables(positions)
    mla_tab = _mla_tables(positions)
    row = lambda v: v.reshape(1, -1).astype(F32)

    h = x.reshape(t, d).astype(F32)
    for layer in range(depth):
        i = layer // 2
        if layer % 2 == 0:
            w_in = _pad_cols(w_in_even[i], even_in).astype(BF16)
            proj = rms_matmul(h, row(mix_norm_g[layer]), w_in, tm=tm, tn=PROJ_COL_TILE)
            proj3 = proj.reshape(b, s, -1)
            a_pairs = a_width // LANES
            out_a = band_attention(proj3, band_bias_tiles(rel_bias_a[i]),
                                   q_blk0=0, k_blk0=a_pairs, v_blk0=2 * a_pairs)
            n_heads = w_uq_b.shape[2] // (B_NOPE_DIM + B_ROPE_DIM)
            wq = jnp.pad(w_uq_b[i].reshape(-1, n_heads, B_NOPE_DIM + B_ROPE_DIM),
                         ((0, 0), (0, 0), (0, LANES - B_NOPE_DIM - B_ROPE_DIM)))
            wq = wq.reshape(-1, n_heads * LANES).astype(BF16)
            wkv = w_ukv_b[i].reshape(-1, n_heads, B_NOPE_DIM + B_V_DIM)
            wk = jnp.pad(wkv[:, :, :B_NOPE_DIM], ((0, 0), (0, 0), (0, LANES - B_NOPE_DIM)))
            wk = wk.reshape(-1, n_heads * LANES).astype(BF16)
            wv = wkv[:, :, B_NOPE_DIM:].reshape(-1, n_heads * B_V_DIM).astype(BF16)
            q_b, k_b, v_b = mla_prep(proj, row(q_norm_b[i]), row(kv_norm_b[i]), wq, wk, wv, mla_tab,
                                     tm=tm, cq_blk=cq_blk, ckv_blk=ckv_blk, kr_blk=kr_blk)
            out_b = flash_attention(q_b.reshape(b, s, -1), k_b.reshape(b, s, -1),
                                    v_b.reshape(b, s, -1), mode="mla", q_blk0=0, k_blk0=0,
                                    v_blk0=0, n_groups=n_heads // 2, tq=FLASH_TQ,
                                    groups=FLASH_GROUPS)
            w_out = w_out_even[i].astype(BF16)
            h = matmul_residual([out_a.reshape(t, -1), out_b.reshape(t, -1)],
                                [w_out[:a_width], w_out[a_width:]], h, tm=OUT_PROJ_ROWS)
            n_tiles = t // tm
            h = grouped_swiglu(h, row(ffn_norm_g[layer]), w_gu_dense.astype(BF16),
                               w_down_dense.astype(BF16), jnp.full((n_tiles,), i, I32),
                               jnp.full((1,), n_tiles, I32), tm=tm,
                               tf=w_down_dense.shape[1] // FF_STEPS, add_res=True)
        else:
            c_width = w_out_odd.shape[1]
            n_heads = c_width // LANES
            w_in = _diff_head_order(w_in_odd[i], 2 * c_width).astype(BF16)
            qkv = rms_matmul(h, row(mix_norm_g[layer]), w_in, tm=tm, tn=PROJ_COL_TILE,
                             rope=(diff_tab, c_width // PROJ_COL_TILE, LANES // 2))
            lam_init = 0.8 - 0.6 * math.exp(-0.3 * layer)
            lam = (jnp.exp(jnp.sum(lambda_q1[i].astype(F32) * lambda_k1[i].astype(F32)))
                   - jnp.exp(jnp.sum(lambda_q2[i].astype(F32) * lambda_k2[i].astype(F32)))
                   + lam_init).reshape(1).astype(F32)
            qkv3 = qkv.reshape(b, s, -1)
            out_c = flash_attention(qkv3, qkv3, qkv3, mode="diff", q_blk0=0, k_blk0=n_heads,
                                    v_blk0=2 * n_heads, n_groups=n_heads, tq=FLASH_TQ,
                                    groups=FLASH_GROUPS, lam=lam,
                                    subln_g=row(subln_g[i]), out_scale=1.0 - lam_init)
            h = matmul_residual([out_c.reshape(t, -1)], [w_out_odd[i].astype(BF16)], h,
                                tm=OUT_PROJ_ROWS)

            g_ffn = row(ffn_norm_g[layer])
            wr = _pad_cols(w_router[i].astype(F32), LANES)
            wr_hi = wr.astype(BF16)
            wr_split = jnp.stack([wr_hi, (wr - wr_hi.astype(F32)).astype(BF16)])
            gates, idx_out, cnt, n_packed = router(h, g_ffn, wr_split, tb=ROW_TILE)
            pos0, pos1, tile_expert, n_valid = _routing_plan(
                idx_out, cnt[0, :N_EXPERTS], tm=tm_moe, n_tiles=n_tiles_moe)
            xs = scatter_rows(n_packed, pos0, pos1, n_tiles_moe * tm_moe, rows=DMA_ROWS)
            ys = grouped_swiglu(xs, None, w_gu_all, w_down_all, tile_expert + i * N_EXPERTS,
                                n_valid, tm=tm_moe, tf=w_down_all.shape[1] // FF_STEPS)
            last = layer == depth - 1
            h = combine_rows(h, gates, ys, pos0, pos1, rows=DMA_ROWS,
                             norm_g=row(final_norm_g) if last else None)

    if depth % 2 == 1:
        h = rmsnorm(h, row(final_norm_g), tm=tm)
    return h.reshape(b, s, d)
```
